```python
import math, functools
import jax, jax.numpy as jnp
from jax import lax
import numpy as np

D_MODEL = 1024
BATCH = 8
SEQ = 2048
DEPTH = 1
DEC_BATCH = 32
DEC_SEQ = 1
PAST_LEN = 16384
PAGE_SIZE = 128

SSM_EXPAND = 2
D_INNER = SSM_EXPAND * D_MODEL
SSM_HEAD_DIM = 64
N_SSM_HEADS = D_INNER // SSM_HEAD_DIM
N_SSM_GROUPS = 4
SSM_STATE = 128
CONV_WIDTH = 4
CONV_DIM = D_INNER + 2 * N_SSM_GROUPS * SSM_STATE
SSD_CHUNK = 128
ATTN_HEAD_DIM = 64
N_ATTN_HEADS = D_MODEL // ATTN_HEAD_DIM
ATTN_DIM = N_ATTN_HEADS * ATTN_HEAD_DIM
MOBA_BLOCK = 256
MOBA_TOPK = 3
QUERY_BLOCK = 128
D_FF = ((8 * D_MODEL // 3 + 127) // 128) * 128
IN_SPLITS = (D_INNER, D_INNER + CONV_DIM, D_INNER + CONV_DIM + N_SSM_HEADS,
             D_INNER + CONV_DIM + N_SSM_HEADS + ATTN_DIM,
             D_INNER + CONV_DIM + N_SSM_HEADS + 2 * ATTN_DIM,
             D_INNER + CONV_DIM + N_SSM_HEADS + 3 * ATTN_DIM,
             D_INNER + CONV_DIM + N_SSM_HEADS + 3 * ATTN_DIM + D_MODEL)
IN_COLS = D_INNER + CONV_DIM + N_SSM_HEADS + 3 * ATTN_DIM + 2 * D_MODEL
DEEPNORM_ALPHA = (2.0 * DEPTH) ** 0.25
DEEPNORM_BETA = (8.0 * DEPTH) ** -0.25
LN_EPS = 1e-5
RMS_EPS = 1e-5
NEG_INF = -1e30

kernel_name = 'hybrid_ssd_moba_macaron_step'


def layer_norm(x, g, b):
    xf = x.astype(jnp.float32)
    mu = jnp.mean(xf, axis=-1, keepdims=True)
    var = jnp.mean(jnp.square(xf - mu), axis=-1, keepdims=True)
    return ((xf - mu) * lax.rsqrt(var + LN_EPS)).astype(x.dtype) * g + b


def modulate(x, shift, scale):
    return x * (1 + scale) + shift


def swiglu(h, w_gu, w_down):
    gate, up = jnp.split(h @ w_gu, 2, axis=-1)
    return (jax.nn.silu(gate) * up) @ w_down


def causal_conv(xbc, buf, w, b):
    t = xbc.shape[1]
    xp = jnp.concatenate([buf.astype(xbc.dtype), xbc], axis=1)
    y = b + sum(xp[:, j:j + t] * w[j] for j in range(CONV_WIDTH))
    return jax.nn.silu(y), xp[:, t:]


def ssd_chunked(xh, dt, a, bm, cm, h0, chunk):
    nb_, t = xh.shape[:2]
    nc = t // chunk
    g, e = N_SSM_GROUPS, N_SSM_HEADS // N_SSM_GROUPS
    xc = xh.reshape(nb_, nc, chunk, g, e, SSM_HEAD_DIM)
    dtc = dt.reshape(nb_, nc, chunk, g, e)
    bc = bm.reshape(nb_, nc, chunk, g, SSM_STATE)
    cc = cm.reshape(nb_, nc, chunk, g, SSM_STATE)
    dtc_t = jnp.moveaxis(dtc, 2, -1)
    a_cum = jnp.cumsum(dtc_t * a.reshape(g, e)[:, :, None], axis=-1)
    causal = jnp.tril(jnp.ones((chunk, chunk), dtype=bool))
    seg = a_cum[..., :, None] - a_cum[..., None, :]
    decay = jnp.exp(jnp.where(causal, seg, -jnp.inf))
    cb = jnp.einsum('bcign,bcjgn->bcgij', cc, bc)
    w_intra = cb[:, :, :, None] * decay * dtc_t[..., None, :]
    y_intra = jnp.einsum('bcgeij,bcjgep->bcigep', w_intra, xc)
    decay_end = jnp.exp(a_cum[..., -1:] - a_cum)
    chunk_states = jnp.einsum('bcjgn,bcgej,bcjgep->bcgepn', bc, decay_end * dtc_t, xc)
    chunk_decay = jnp.exp(a_cum[..., -1])

    def step(h, inp):
        dec, st = inp
        return dec[..., None, None] * h + st, h

    h0g = h0.reshape(nb_, g, e, SSM_HEAD_DIM, SSM_STATE).astype(jnp.float32)
    h_fin, h_start = lax.scan(step, h0g, (jnp.moveaxis(chunk_decay, 1, 0), jnp.moveaxis(chunk_states, 1, 0)))
    h_start = jnp.moveaxis(h_start, 0, 1)
    y_inter = jnp.einsum('bcign,bcgei,bcgepn->bcigep', cc, jnp.exp(a_cum), h_start)
    y = (y_intra + y_inter).reshape(nb_, t, N_SSM_HEADS, SSM_HEAD_DIM)
    return y.astype(xh.dtype), h_fin.reshape(nb_, N_SSM_HEADS, SSM_HEAD_DIM, SSM_STATE).astype(h0.dtype)


def gated_rmsnorm(y, z, g):
    nb_, t, _ = y.shape
    yz = (y * jax.nn.silu(z)).astype(jnp.float32).reshape(nb_, t, N_SSM_GROUPS, D_INNER // N_SSM_GROUPS)
    yz = yz * lax.rsqrt(jnp.mean(yz * yz, axis=-1, keepdims=True) + RMS_EPS)
    return yz.reshape(nb_, t, D_INNER).astype(y.dtype) * g


def ssd_branch(z, xbc, dt_raw, conv_buf, h0, chunk, conv_w, conv_b, dt_bias, a_log, d_skip, norm_g):
    nb_, t, _ = xbc.shape
    xbc, new_buf = causal_conv(xbc, conv_buf, conv_w, conv_b)
    xs, bm, cm = jnp.split(xbc, [D_INNER, D_INNER + N_SSM_GROUPS * SSM_STATE], axis=-1)
    xh = xs.reshape(nb_, t, N_SSM_HEADS, SSM_HEAD_DIM)
    bm = bm.reshape(nb_, t, N_SSM_GROUPS, SSM_STATE)
    cm = cm.reshape(nb_, t, N_SSM_GROUPS, SSM_STATE)
    dt = jax.nn.softplus((dt_raw + dt_bias).astype(jnp.float32))
    a = -jnp.exp(a_log.astype(jnp.float32))
    y, h_fin = ssd_chunked(xh, dt, a, bm, cm, h0, chunk)
    y = y + d_skip[:, None] * xh
    return gated_rmsnorm(y.reshape(nb_, t, D_INNER), z, norm_g), new_buf, h_fin


def moba_select(q, block_means, q_pos):
    nb = block_means.shape[1]
    own = q_pos // MOBA_BLOCK
    scores = jnp.einsum('bshd,bjhd->bshj', q.astype(jnp.float32), block_means.astype(jnp.float32))
    fully_past = jnp.arange(nb)[None, :] < own[:, None]
    scores = jnp.where(fully_past[None, :, None, :], scores, NEG_INF)
    _, top = lax.top_k(scores, min(MOBA_TOPK, nb))
    own_b = jnp.broadcast_to(own[None, :, None, None], top.shape[:-1] + (1,))
    sel = jnp.concatenate([top, own_b], axis=-1)
    ok = jnp.concatenate([top < own[None, :, None, None], jnp.ones(own_b.shape, dtype=bool)], axis=-1)
    return sel, ok


def expand_rows(sel, ok):
    rows = sel[..., None] * MOBA_BLOCK + jnp.arange(MOBA_BLOCK)
    ok = jnp.broadcast_to(ok[..., None], rows.shape)
    shp = sel.shape[:-1] + (-1,)
    return rows.reshape(shp), ok.reshape(shp)


def moba_attend(q, k_sel, v_sel, row_pos, row_ok, q_pos):
    logits = jnp.einsum('...hd,...hld->...hl', q.astype(jnp.float32), k_sel.astype(jnp.float32)) * (ATTN_HEAD_DIM ** -0.5)
    mask = row_ok & (row_pos <= q_pos)
    p = jax.nn.softmax(jnp.where(mask, logits, NEG_INF), axis=-1)
    return jnp.einsum('...hl,...hld->...hd', p, v_sel.astype(jnp.float32)).astype(q.dtype)


def moba_prompt(q, k, v):
    nb_, s = q.shape[:2]
    nb = -(-s // MOBA_BLOCK)
    k_pad = jnp.pad(k.astype(jnp.float32), ((0, 0), (0, nb * MOBA_BLOCK - s), (0, 0), (0, 0)))
    means = jnp.mean(k_pad.reshape(nb_, nb, MOBA_BLOCK, N_ATTN_HEADS, ATTN_HEAD_DIM), axis=2)
    sel, ok = moba_select(q, means, jnp.arange(s))
    nqb = s // QUERY_BLOCK
    h_idx = jnp.arange(N_ATTN_HEADS)[None, :, None]

    def one_block(inp):
        bi, qb, q_blk, sel_blk, ok_blk = inp
        pos = qb * QUERY_BLOCK + jnp.arange(QUERY_BLOCK)
        rows, row_ok = expand_rows(sel_blk, ok_blk)
        idx = jnp.clip(rows, 0, s - 1)
        k_sel = k[bi][idx, h_idx]
        v_sel = v[bi][idx, h_idx]
        return moba_attend(q_blk, k_sel, v_sel, rows, row_ok, pos[:, None, None])

    ns = sel.shape[-1]
    xs = (jnp.repeat(jnp.arange(nb_), nqb), jnp.tile(jnp.arange(nqb), nb_),
          q.reshape(nb_ * nqb, QUERY_BLOCK, N_ATTN_HEADS, ATTN_HEAD_DIM),
          sel.reshape(nb_ * nqb, QUERY_BLOCK, N_ATTN_HEADS, ns),
          ok.reshape(nb_ * nqb, QUERY_BLOCK, N_ATTN_HEADS, ns))
    out = lax.map(one_block, xs)
    return out.reshape(nb_, s, N_ATTN_HEADS, ATTN_HEAD_DIM)


def moba_sample(q, k_new, v_new, cache_k, cache_v, layer, page_table):
    nb_, s = q.shape[:2]
    n_pages = page_table.shape[1]
    past = n_pages * PAGE_SIZE
    nb = -(-(past + s) // MOBA_BLOCK)
    page_sums = jnp.sum(cache_k[layer, page_table].astype(jnp.float32), axis=2)
    page_blk = jax.nn.one_hot((jnp.arange(n_pages) * PAGE_SIZE) // MOBA_BLOCK, nb, dtype=jnp.float32)
    new_blk = jax.nn.one_hot((past + jnp.arange(s)) // MOBA_BLOCK, nb, dtype=jnp.float32)
    means = (jnp.einsum('pj,bphd->bjhd', page_blk, page_sums)
             + jnp.einsum('sj,bshd->bjhd', new_blk, k_new.astype(jnp.float32))) / MOBA_BLOCK
    q_pos = past + jnp.arange(s)
    sel, ok = moba_select(q, means, q_pos)
    rows, row_ok = expand_rows(sel, ok)
    b_idx = jnp.arange(nb_)[:, None, None, None]
    h_idx = jnp.arange(N_ATTN_HEADS)[None, None, :, None]
    in_past = (rows < past)[..., None]
    phys = page_table[b_idx, jnp.clip(rows // PAGE_SIZE, 0, n_pages - 1)]
    flat = phys * PAGE_SIZE + rows % PAGE_SIZE
    new_i = jnp.clip(rows - past, 0, s - 1)
    pool_k = cache_k.reshape(cache_k.shape[0], -1, N_ATTN_HEADS, ATTN_HEAD_DIM)
    pool_v = cache_v.reshape(cache_v.shape[0], -1, N_ATTN_HEADS, ATTN_HEAD_DIM)
    k_sel = jnp.where(in_past, pool_k[layer, flat, h_idx], k_new[b_idx, new_i, h_idx])
    v_sel = jnp.where(in_past, pool_v[layer, flat, h_idx], v_new[b_idx, new_i, h_idx])
    return moba_attend(q, k_sel, v_sel, rows, row_ok, q_pos[None, :, None, None])


def mixer(h, conv_buf, h0, chunk, moba_fn, mix_w):
    w_in, conv_w, conv_b, dt_bias, a_log, d_skip, norm_g, w_branch_ssm, w_branch_attn, w_out = mix_w
    nb_, t, _ = h.shape
    z, xbc, dt_raw, q, k, v, g_a, g_b = jnp.split(h @ w_in, list(IN_SPLITS), axis=-1)
    y_ssm, new_buf, h_fin = ssd_branch(z, xbc, dt_raw, conv_buf, h0, chunk, conv_w, conv_b, dt_bias, a_log, d_skip, norm_g)
    qh = q.reshape(nb_, t, N_ATTN_HEADS, ATTN_HEAD_DIM)
    kh = k.reshape(nb_, t, N_ATTN_HEADS, ATTN_HEAD_DIM)
    vh = v.reshape(nb_, t, N_ATTN_HEADS, ATTN_HEAD_DIM)
    y_attn = moba_fn(qh, kh, vh).reshape(nb_, t, ATTN_DIM)
    merged = jax.nn.sigmoid(g_a) * (y_ssm @ w_branch_ssm) + jax.nn.sigmoid(g_b) * (y_attn @ w_branch_attn)
    return merged @ w_out, (kh, vh, new_buf, h_fin)


def trunk_layer(x, c, mix_fn, w_ada, b_ada, ln_g, ln_b, w1_gu, w1_down, w2_gu, w2_down):
    mods = (jax.nn.silu(c) @ w_ada + b_ada)[:, None, :]
    sh1, sc1, g1, sh2, sc2, g2, sh3, sc3, g3 = jnp.split(mods, 9, axis=-1)
    x = layer_norm(DEEPNORM_ALPHA * x + 0.5 * g1 * swiglu(modulate(x, sh1, sc1), w1_gu, w1_down), ln_g[0], ln_b[0])
    mix_out, new_state = mix_fn(modulate(x, sh2, sc2))
    x = layer_norm(DEEPNORM_ALPHA * x + g2 * mix_out, ln_g[1], ln_b[1])
    x = layer_norm(DEEPNORM_ALPHA * x + 0.5 * g3 * swiglu(modulate(x, sh3, sc3), w2_gu, w2_down), ln_g[2], ln_b[2])
    return x, new_state


def setup_inputs(seed: int = 0) -> dict:
    key = jax.random.key(seed)
    ks = jax.random.split(key, 32)
    n_pages = PAST_LEN // PAGE_SIZE
    n_used = DEC_BATCH * n_pages
    n_phys = n_used + max(1, n_used // 4)
    f32 = jnp.float32

    def nrm(k, shape, scale):
        return jax.random.normal(k, shape, f32) * scale

    beta = DEEPNORM_BETA
    v_start = IN_SPLITS[4]
    col_scale = jnp.ones((IN_COLS,), f32).at[v_start:v_start + ATTN_DIM].set(beta)
    w_in = nrm(ks[0], (DEPTH, D_MODEL, IN_COLS), D_MODEL ** -0.5) * col_scale
    dt0 = jnp.exp(jax.random.uniform(ks[1], (DEPTH, N_SSM_HEADS), f32, minval=math.log(1e-3), maxval=math.log(1e-1)))
    dt_bias = dt0 + jnp.log(-jnp.expm1(-dt0))
    a_log = jnp.log(jax.random.uniform(ks[2], (DEPTH, N_SSM_HEADS), f32, minval=1.0, maxval=16.0))
    page_table = jax.random.permutation(ks[3], n_phys)[:n_used].reshape(DEC_BATCH, n_pages).astype(jnp.int32)
    return {
        'x_prompt': nrm(ks[4], (BATCH, SEQ, D_MODEL), 1.0),
        'x_sample': nrm(ks[5], (DEC_BATCH, DEC_SEQ, D_MODEL), 1.0),
        'cache_k': nrm(ks[6], (DEPTH, n_phys, PAGE_SIZE, N_ATTN_HEADS, ATTN_HEAD_DIM), 1.0),
        'cache_v': nrm(ks[7], (DEPTH, n_phys, PAGE_SIZE, N_ATTN_HEADS, ATTN_HEAD_DIM), 1.0),
        'state_conv': nrm(ks[8], (DEPTH, DEC_BATCH, CONV_WIDTH - 1, CONV_DIM), 1.0),
        'state_ssm': nrm(ks[9], (DEPTH, DEC_BATCH, N_SSM_HEADS, SSM_HEAD_DIM, SSM_STATE), 0.1),
        'page_table': page_table,
        'c_prompt': nrm(ks[10], (BATCH, D_MODEL), 1.0),
        'c_sample': nrm(ks[11], (DEC_BATCH, D_MODEL), 1.0),
        'w_ada': nrm(ks[12], (DEPTH, D_MODEL, 9 * D_MODEL), 0.5 * D_MODEL ** -0.5),
        'b_ada': nrm(ks[13], (DEPTH, 9 * D_MODEL), 0.01),
        'ln_g': 1.0 + nrm(ks[14], (DEPTH, 3, D_MODEL), 0.01),
        'ln_b': nrm(ks[15], (DEPTH, 3, D_MODEL), 0.01),
        'w_ffn1_gu': nrm(ks[16], (DEPTH, D_MODEL, 2 * D_FF), D_MODEL ** -0.5),
        'w_ffn1_down': nrm(ks[17], (DEPTH, D_FF, D_MODEL), beta * D_FF ** -0.5),
        'w_ffn2_gu': nrm(ks[18], (DEPTH, D_MODEL, 2 * D_FF), D_MODEL ** -0.5),
        'w_ffn2_down': nrm(ks[19], (DEPTH, D_FF, D_MODEL), beta * D_FF ** -0.5),
        'w_in': w_in,
        'conv_w': nrm(ks[20], (DEPTH, CONV_WIDTH, CONV_DIM), CONV_WIDTH ** -0.5),
        'conv_b': nrm(ks[21], (DEPTH, CONV_DIM), 0.01),
        'dt_bias': dt_bias,
        'a_log': a_log,
        'd_skip': 1.0 + nrm(ks[22], (DEPTH, N_SSM_HEADS), 0.1),
        'ssm_norm_g': 1.0 + nrm(ks[23], (DEPTH, D_INNER), 0.01),
        'w_branch_ssm': nrm(ks[24], (DEPTH, D_INNER, D_MODEL), D_INNER ** -0.5),
        'w_branch_attn': nrm(ks[25], (DEPTH, ATTN_DIM, D_MODEL), ATTN_DIM ** -0.5),
        'w_out': nrm(ks[26], (DEPTH, D_MODEL, D_MODEL), beta * D_MODEL ** -0.5),
    }


def reference(x_prompt, x_sample, cache_k, cache_v, state_conv, state_ssm, page_table, c_prompt, c_sample,
              w_ada, b_ada, ln_g, ln_b, w_ffn1_gu, w_ffn1_down, w_ffn2_gu, w_ffn2_down,
              w_in, conv_w, conv_b, dt_bias, a_log, d_skip, ssm_norm_g, w_branch_ssm, w_branch_attn, w_out):
    y_p, y_s = x_prompt, x_sample
    states_p, states_s = [], []
    for l in range(DEPTH):
        mix_w = (w_in[l], conv_w[l], conv_b[l], dt_bias[l], a_log[l], d_skip[l], ssm_norm_g[l],
                 w_branch_ssm[l], w_branch_attn[l], w_out[l])
        chan_w = (w_ada[l], b_ada[l], ln_g[l], ln_b[l], w_ffn1_gu[l], w_ffn1_down[l], w_ffn2_gu[l], w_ffn2_down[l])
        conv0 = jnp.zeros((y_p.shape[0], CONV_WIDTH - 1, CONV_DIM), y_p.dtype)
        ssm0 = jnp.zeros((y_p.shape[0], N_SSM_HEADS, SSM_HEAD_DIM, SSM_STATE), jnp.float32)
        prompt_mix = functools.partial(mixer, conv_buf=conv0, h0=ssm0, chunk=SSD_CHUNK,
                                       moba_fn=moba_prompt, mix_w=mix_w)
        sample_moba = functools.partial(moba_sample, cache_k=cache_k, cache_v=cache_v, layer=l, page_table=page_table)
        sample_mix = functools.partial(mixer, conv_buf=state_conv[l], h0=state_ssm[l], chunk=y_s.shape[1],
                                       moba_fn=sample_moba, mix_w=mix_w)
        y_p, st_p = trunk_layer(y_p, c_prompt, prompt_mix, *chan_w)
        y_s, st_s = trunk_layer(y_s, c_sample, sample_mix, *chan_w)
        states_p.append(st_p)
        states_s.append(st_s)
    k_prompt = jnp.stack([st[0] for st in states_p])
    v_prompt = jnp.stack([st[1] for st in states_p])
    conv_prompt = jnp.stack([st[2] for st in states_p])
    ssm_prompt = jnp.stack([st[3] for st in states_p])
    k_sample = jnp.stack([st[0] for st in states_s])
    v_sample = jnp.stack([st[1] for st in states_s])
    conv_sample = jnp.stack([st[2] for st in states_s])
    ssm_sample = jnp.stack([st[3] for st in states_s])
    return (y_p, y_s, k_prompt, v_prompt, conv_prompt, ssm_prompt, k_sample, v_sample, conv_sample, ssm_sample)
```

```python
import functools

import jax
import jax.numpy as jnp
from jax import lax
from jax.experimental import pallas as pl
from jax.experimental.pallas import tpu as pltpu

F32 = jnp.float32
BF16 = jnp.bfloat16

D_MODEL = 1024
D_INNER = 2048
SSM_HEAD_DIM = 64
N_SSM_HEADS = 32
N_SSM_GROUPS = 4
SSM_STATE = 128
CONV_WIDTH = 4
CONV_DIM = D_INNER + 2 * N_SSM_GROUPS * SSM_STATE
SSD_CHUNK = 128
ATTN_HEAD_DIM = 64
N_ATTN_HEADS = 16
ATTN_DIM = 1024
MOBA_BLOCK = 256
MOBA_TOPK = 3
PAGE_SIZE = 128
D_FF = 2816
DEPTH = 1
DEEPNORM_ALPHA = (2.0 * DEPTH) ** 0.25
LN_EPS = 1e-5
RMS_EPS = 1e-5
NEG_INF = -1e30
ATTN_SCALE = ATTN_HEAD_DIM ** -0.5

LANES = 128
SUBLANES = 8
VMEM_LIMIT = 56 * 1024 * 1024


def _dot(a, b):
    return jnp.dot(a, b, preferred_element_type=F32)


def _dot_nt(a, b):
    return lax.dot_general(a, b, (((1,), (1,)), ((), ())), preferred_element_type=F32)


def _split2(a):
    hi = a.astype(BF16)
    lo = (a - hi.astype(F32)).astype(BF16)
    return hi, lo


def _split3(a):
    hi = a.astype(BF16)
    r = a - hi.astype(F32)
    mid = r.astype(BF16)
    lo = (r - mid.astype(F32)).astype(BF16)
    return hi, mid, lo


def _dot_sel(a, sel_bf16):
    hi, mid, lo = _split3(a)
    return _dot(hi, sel_bf16) + _dot(mid, sel_bf16) + _dot(lo, sel_bf16)


def _dot_hp(a, b):
    ah, al = _split2(a)
    bh, bl = _split2(b)
    return _dot(ah, bh) + _dot(al, bh) + _dot(ah, bl)


def _dot_nt_hp(a, b):
    ah, al = _split2(a)
    bh, bl = _split2(b)
    return _dot_nt(ah, bh) + _dot_nt(al, bh) + _dot_nt(ah, bl)


def _silu(x):
    return x * jax.nn.sigmoid(x)


def _softplus(x):
    return jnp.maximum(x, 0.0) + jnp.log1p(jnp.exp(-jnp.abs(x)))


def _layer_norm(y, g, b):
    mu = jnp.mean(y, axis=-1, keepdims=True)
    yc = y - mu
    var = jnp.mean(yc * yc, axis=-1, keepdims=True)
    return yc * lax.rsqrt(var + LN_EPS) * g + b


def _resident(shape):
    nd = len(shape)
    return pl.BlockSpec(shape, lambda *_: (0,) * nd, pipeline_mode=pl.Buffered(1))


def _params(n_axes):
    return pltpu.CompilerParams(dimension_semantics=("arbitrary",) * n_axes, vmem_limit_bytes=VMEM_LIMIT)


def _ada_kernel(c_ref, w_ref, b_ref, o_ref):
    s = _silu(c_ref[...])
    o_ref[...] = _dot_hp(s, w_ref[...]) + b_ref[...]


def _ada_mods(c, w_ada, b_ada):
    m, d = c.shape
    n = w_ada.shape[1]
    tn = 1024
    return pl.pallas_call(
        _ada_kernel,
        out_shape=jax.ShapeDtypeStruct((m, n), F32),
        grid=(n // tn,),
        in_specs=[pl.BlockSpec((m, d), lambda j: (0, 0)),
                  pl.BlockSpec((d, tn), lambda j: (0, j)),
                  pl.BlockSpec((1, tn), lambda j: (0, j))],
        out_specs=pl.BlockSpec((m, tn), lambda j: (0, j)),
        compiler_params=_params(1),
        name="ada_mods",
    )(c, w_ada, b_ada.reshape(1, n))


def _mod_specs(mods, ks, tm, rows_per_batch):
    if mods.ndim == 3:
        tiles_per_batch = rows_per_batch // tm
        return [pl.BlockSpec((None, 1, D_MODEL), functools.partial(
            lambda i, k: ((i // tiles_per_batch) * 9 + k, 0, 0), k=k)) for k in ks]
    return [pl.BlockSpec((tm, D_MODEL), functools.partial(lambda i, k: (i, k), k=k)) for k in ks]


FF_CHUNK = 1408


def _ffn_ln_kernel(x_ref, sh_ref, sc_ref, g_ref, wgu_ref, wd_ref, lng_ref, lnb_ref, o_ref):
    x = x_ref[...]
    h = (x * (1.0 + sc_ref[...]) + sh_ref[...]).astype(BF16)
    acc = jnp.zeros(x.shape, F32)
    for j in range(D_FF // FF_CHUNK):
        lo = j * FF_CHUNK
        gate = _dot(h, wgu_ref[:, lo:lo + FF_CHUNK])
        up = _dot(h, wgu_ref[:, D_FF + lo:D_FF + lo + FF_CHUNK])
        act = (_silu(gate) * up).astype(BF16)
        acc = acc + _dot(act, wd_ref[lo:lo + FF_CHUNK, :])
    y = DEEPNORM_ALPHA * x + 0.5 * g_ref[...] * acc
    o_ref[...] = _layer_norm(y, lng_ref[...], lnb_ref[...])


def _ffn_ln(x, mods, ks, wgu, wd, lng, lnb, tm, rows_per_batch):
    n, d = x.shape
    row = pl.BlockSpec((tm, d), lambda i: (i, 0))
    return pl.pallas_call(
        _ffn_ln_kernel,
        out_shape=jax.ShapeDtypeStruct((n, d), F32),
        grid=(n // tm,),
        in_specs=[row] + _mod_specs(mods, ks, tm, rows_per_batch)
        + [_resident(wgu.shape), _resident(wd.shape), _resident((1, d)), _resident((1, d))],
        out_specs=row,
        compiler_params=_params(1),
        name="ffn_ln",
    )(x, mods, mods, mods, wgu, wd, lng.reshape(1, d), lnb.reshape(1, d))


def _in_proj_kernel(x_ref, sh_ref, sc_ref, *refs, transposed):
    n_out = len(refs) // 2
    h = (x_ref[...] * (1.0 + sc_ref[...]) + sh_ref[...]).astype(BF16)
    for w_ref, o_ref, t in zip(refs[:n_out], refs[n_out:], transposed):
        o_ref[...] = (_dot_nt(w_ref[...], h) if t else _dot(h, w_ref[...])).astype(o_ref.dtype)


def _in_proj(x, mods, ks, weights, transposed, tm, rows_per_batch):
    n, d = x.shape
    nb = n // rows_per_batch
    tiles_per_batch = rows_per_batch // tm if any(transposed) else 1
    row = pl.BlockSpec((tm, d), lambda i: (i, 0))
    out_shape, out_specs = [], []
    for w, t in zip(weights, transposed):
        if t:
            out_shape.append(jax.ShapeDtypeStruct((nb, w.shape[0], rows_per_batch), F32))
            out_specs.append(pl.BlockSpec((None, w.shape[0], tm),
                                          lambda i: (i // tiles_per_batch, 0, i % tiles_per_batch)))
        else:
            out_shape.append(jax.ShapeDtypeStruct((n, w.shape[1]), F32))
            out_specs.append(pl.BlockSpec((tm, w.shape[1]), lambda i: (i, 0)))
    return pl.pallas_call(
        functools.partial(_in_proj_kernel, transposed=tuple(transposed)),
        out_shape=out_shape,
        grid=(n // tm,),
        in_specs=[row] + _mod_specs(mods, ks, tm, rows_per_batch) + [_resident(w.shape) for w in weights],
        out_specs=out_specs,
        compiler_params=_params(1),
        name="in_proj",
    )(x, mods, mods, *weights)


GROUP_W = D_INNER // N_SSM_GROUPS
CARRY = SUBLANES


def _gated_rmsnorm(y, z, norm_g):
    yz = y * _silu(z)
    outs = []
    for g in range(N_SSM_GROUPS):
        blk = yz[:, g * GROUP_W:(g + 1) * GROUP_W]
        ms = jnp.mean(blk * blk, axis=-1, keepdims=True)
        outs.append(blk * lax.rsqrt(ms + RMS_EPS))
    return jnp.concatenate(outs, axis=-1) * norm_g


def _ssd_prompt_kernel(xbc_ref, z_ref, dt_ref, cw_ref, cb_ref, dtb_ref, alog_ref, dskip_ref, ng_ref, exp_ref,
                       y_ref, conv_ref, ssm_ref, buf_s, xc_s, st_s):
    c = pl.program_id(1)
    q = SSD_CHUNK

    @pl.when(c == 0)
    def _():
        buf_s[0:CARRY, :] = jnp.zeros((CARRY, CONV_DIM), F32)
        st_s[...] = jnp.zeros(st_s.shape, F32)

    buf_s[CARRY:CARRY + q, :] = xbc_ref[...]
    for lo in range(0, CONV_DIM, GROUP_W):
        cols = slice(lo, lo + GROUP_W)
        acc = cb_ref[:, cols]
        for j in range(CONV_WIDTH):
            off = CARRY - (CONV_WIDTH - 1) + j
            acc = acc + cw_ref[j:j + 1, cols] * buf_s[off:off + q, cols]
        xc_s[:, cols] = _silu(acc)

    @pl.when(c == pl.num_programs(1) - 1)
    def _():
        conv_ref[...] = buf_s[CARRY + q - (CONV_WIDTH - 1):CARRY + q, :]

    buf_s[0:CARRY, :] = buf_s[q:q + CARRY, :]

    expand = exp_ref[...]
    dt = _softplus(dt_ref[...] + dtb_ref[...])
    da = dt * (-jnp.exp(alog_ref[...]))
    row_i = lax.broadcasted_iota(jnp.int32, (q, q), 0)
    col_i = lax.broadcasted_iota(jnp.int32, (q, q), 1)
    causal = row_i >= col_i
    tril = jnp.where(causal, 1.0, 0.0).astype(BF16)
    a_cum = _dot_tril(tril, da)
    a_cum_t = a_cum.T
    a_last = a_cum[q - 1:q, :]
    dt_x = _dot_sel(dt, expand)
    dec_end_x = _dot_sel(jnp.exp(a_last - a_cum), expand)
    exp_a_x = _dot_sel(jnp.exp(a_cum), expand)
    chunk_decay_x = _dot_sel(jnp.broadcast_to(jnp.exp(a_last), (SUBLANES, LANES)), expand)[0:1, :]

    xs = xc_s[:, 0:D_INNER]
    xdt = xs * dt_x
    xdt_b = xdt.astype(BF16)
    xdec_b = (xdt * dec_end_x).astype(BF16)
    lane = lax.broadcasted_iota(jnp.int32, (1, LANES), 1)
    zero_b = jnp.zeros((), BF16)

    y_parts = []
    for g in range(N_SSM_GROUPS):
        b_g = xc_s[:, D_INNER + g * SSM_STATE:D_INNER + (g + 1) * SSM_STATE]
        c_g = xc_s[:, D_INNER + (N_SSM_GROUPS + g) * SSM_STATE:D_INNER + (N_SSM_GROUPS + g + 1) * SSM_STATE]
        b_gb = b_g.astype(BF16)
        c_gb = c_g.astype(BF16)
        cb = _dot_nt(c_gb, b_gb)
        gcols = slice(g * GROUP_W, (g + 1) * GROUP_W)
        st_old = st_s[:, gcols]
        y_inter = _dot(c_gb, st_old.astype(BF16)) * exp_a_x[:, gcols]
        st_s[:, gcols] = chunk_decay_x[:, gcols] * st_old + _dot(b_g.T.astype(BF16), xdec_b[:, gcols])
        pair_out = []
        heads_per_group = N_SSM_HEADS // N_SSM_GROUPS
        for pr in range(heads_per_group // 2):
            pcols = slice(g * GROUP_W + pr * LANES, g * GROUP_W + (pr + 1) * LANES)
            x_pair = xdt_b[:, pcols]
            acc = None
            for hh in range(2):
                h = g * heads_per_group + pr * 2 + hh
                seg = a_cum[:, h:h + 1] - a_cum_t[h:h + 1, :]
                w = (cb * jnp.exp(jnp.where(causal, seg, -jnp.inf))).astype(BF16)
                x_h = jnp.where((lane // SSM_HEAD_DIM) == hh, x_pair, zero_b)
                t = _dot(w, x_h)
                acc = t if acc is None else acc + t
            pair_out.append(acc)
        y_parts.append(jnp.concatenate(pair_out, axis=-1) + y_inter)
    y = jnp.concatenate(y_parts, axis=-1) + dskip_ref[...] * xs
    y_ref[...] = _gated_rmsnorm(y, z_ref[...], ng_ref[...]).astype(y_ref.dtype)

    @pl.when(c == pl.num_programs(1) - 1)
    def _():
        for r in range(D_INNER // LANES):
            ssm_ref[r * LANES:(r + 1) * LANES, :] = st_s[:, r * LANES:(r + 1) * LANES].T


def _dot_tril(tril_bf16, a):
    hi, mid, lo = _split3(a)
    return _dot(tril_bf16, hi) + _dot(tril_bf16, mid) + _dot(tril_bf16, lo)


def _pad_lanes(v):
    return jnp.zeros((1, LANES), F32).at[0, :v.shape[0]].set(v)


def _expand_matrix():
    h = jnp.arange(LANES)[:, None]
    ch = jnp.arange(D_INNER)[None, :] // SSM_HEAD_DIM
    return (h == ch).astype(BF16)


def _ssd_prompt(xbc, z, dt, nb, seq, conv_w, conv_b, dt_bias, a_log, d_skip, norm_g):
    nc = seq // SSD_CHUNK
    rows = lambda w: pl.BlockSpec((SSD_CHUNK, w), lambda b, c: (b * nc + c, 0))
    return pl.pallas_call(
        _ssd_prompt_kernel,
        out_shape=[jax.ShapeDtypeStruct((nb * seq, D_INNER), BF16),
                   jax.ShapeDtypeStruct((nb, CONV_WIDTH - 1, CONV_DIM), F32),
                   jax.ShapeDtypeStruct((nb, D_INNER, SSM_STATE), F32)],
        grid=(nb, nc),
        in_specs=[rows(CONV_DIM), rows(D_INNER), rows(LANES),
                  _resident((CONV_WIDTH, CONV_DIM)), _resident((1, CONV_DIM)), _resident((1, LANES)),
                  _resident((1, LANES)), _resident((1, D_INNER)), _resident((1, D_INNER)),
                  _resident((LANES, D_INNER))],
        out_specs=[rows(D_INNER),
                   pl.BlockSpec((None, CONV_WIDTH - 1, CONV_DIM), lambda b, c: (b, 0, 0)),
                   pl.BlockSpec((None, D_INNER, SSM_STATE), lambda b, c: (b, 0, 0))],
        scratch_shapes=[pltpu.VMEM((CARRY + SSD_CHUNK, CONV_DIM), F32),
                        pltpu.VMEM((SSD_CHUNK, CONV_DIM), F32),
                        pltpu.VMEM((SSM_STATE, D_INNER), F32)],
        compiler_params=_params(2),
        name="ssd_prompt",
    )(xbc, z, dt, conv_w, conv_b.reshape(1, CONV_DIM), _pad_lanes(dt_bias), _pad_lanes(a_log),
      jnp.repeat(d_skip, SSM_HEAD_DIM).reshape(1, D_INNER), norm_g.reshape(1, D_INNER), _expand_matrix())


def _column_block(row_vec, r):
    return jnp.broadcast_to(row_vec[:, r * LANES:(r + 1) * LANES], (LANES, LANES)).T


def _ssd_sample_kernel(xbc_ref, z_ref, dt_ref, buf_ref, h0_ref, cw_ref, cb_ref, dtb_ref, alog_ref, dskip_ref,
                       ng_ref, exp_ref, y_ref, conv_ref, ssm_ref):
    xr = xbc_ref[...]
    acc = cb_ref[...] + cw_ref[CONV_WIDTH - 1:CONV_WIDTH, :] * xr
    for j in range(CONV_WIDTH - 1):
        acc = acc + cw_ref[j:j + 1, :] * buf_ref[j:j + 1, :]
    xc = _silu(acc)
    for j in range(CONV_WIDTH - 2):
        conv_ref[j:j + 1, :] = buf_ref[j + 1:j + 2, :]
    conv_ref[CONV_WIDTH - 2:CONV_WIDTH - 1, :] = xr

    expand = exp_ref[...]
    dt = _softplus(dt_ref[...] + dtb_ref[...])
    dec = jnp.exp(dt * (-jnp.exp(alog_ref[...])))
    both = jnp.concatenate([jnp.broadcast_to(dt, (SUBLANES // 2, LANES)),
                            jnp.broadcast_to(dec, (SUBLANES // 2, LANES))], axis=0)
    both_x = _dot_sel(both, expand)
    dt_x = both_x[0:1, :]
    dec_x = both_x[SUBLANES // 2:SUBLANES // 2 + 1, :]
    xs = xc[:, 0:D_INNER]
    xdt = xs * dt_x

    y_parts = []
    blocks_per_group = GROUP_W // LANES
    for g in range(N_SSM_GROUPS):
        b_g = xc[:, D_INNER + g * SSM_STATE:D_INNER + (g + 1) * SSM_STATE]
        c_g = xc[:, D_INNER + (N_SSM_GROUPS + g) * SSM_STATE:D_INNER + (N_SSM_GROUPS + g + 1) * SSM_STATE]
        c8 = jnp.broadcast_to(c_g, (SUBLANES, SSM_STATE)).astype(BF16)
        for rb in range(blocks_per_group):
            r = g * blocks_per_group + rb
            rows = slice(r * LANES, (r + 1) * LANES)
            h_new = _column_block(dec_x, r) * h0_ref[rows, :] + _column_block(xdt, r) * b_g
            ssm_ref[rows, :] = h_new
            y_parts.append(_dot_nt(c8, h_new.astype(BF16))[0:1, :])
    y = jnp.concatenate(y_parts, axis=-1) + dskip_ref[...] * xs
    y_ref[...] = _gated_rmsnorm(y, z_ref[...], ng_ref[...]).astype(y_ref.dtype)


def _ssd_sample(xbc, z, dt, state_conv, state_ssm, conv_w, conv_b, dt_bias, a_log, d_skip, norm_g):
    nb = xbc.shape[0]
    per_b = lambda *s: pl.BlockSpec((None,) + s, lambda b: (b,) + (0,) * len(s))
    return pl.pallas_call(
        _ssd_sample_kernel,
        out_shape=[jax.ShapeDtypeStruct((nb, 1, D_INNER), BF16),
                   jax.ShapeDtypeStruct((nb, CONV_WIDTH - 1, CONV_DIM), F32),
                   jax.ShapeDtypeStruct((nb, D_INNER, SSM_STATE), F32)],
        grid=(nb,),
        in_specs=[per_b(1, CONV_DIM), per_b(1, D_INNER), per_b(1, LANES),
                  per_b(CONV_WIDTH - 1, CONV_DIM), per_b(D_INNER, SSM_STATE),
                  _resident((CONV_WIDTH, CONV_DIM)), _resident((1, CONV_DIM)), _resident((1, LANES)),
                  _resident((1, LANES)), _resident((1, D_INNER)), _resident((1, D_INNER)),
                  _resident((LANES, D_INNER))],
        out_specs=[per_b(1, D_INNER), per_b(CONV_WIDTH - 1, CONV_DIM), per_b(D_INNER, SSM_STATE)],
        compiler_params=_params(1),
        name="ssd_sample",
    )(xbc.reshape(nb, 1, CONV_DIM), z.reshape(nb, 1, D_INNER), dt.reshape(nb, 1, LANES), state_conv,
      state_ssm.reshape(nb, D_INNER, SSM_STATE), conv_w, conv_b.reshape(1, CONV_DIM), _pad_lanes(dt_bias),
      _pad_lanes(a_log), jnp.repeat(d_skip, SSM_HEAD_DIM).reshape(1, D_INNER), norm_g.reshape(1, D_INNER),
      _expand_matrix())


HEADS_PER_VREG = LANES // ATTN_HEAD_DIM


def _moba_prompt_kernel(q_ref, k_ref, v_ref, o_ref, kb_s, vt_s, means_s, sel_s):
    tq = MOBA_BLOCK
    nblk = kb_s.shape[0]
    own = pl.program_id(2)

    @pl.when(own == 0)
    def _():
        for j in range(nblk):
            ksum = jnp.zeros((1, LANES), F32)
            for t in range(MOBA_BLOCK // LANES):
                cols = slice(j * MOBA_BLOCK + t * LANES, j * MOBA_BLOCK + (t + 1) * LANES)
                kblk = k_ref[:, cols].T
                kb_s[j, t * LANES:(t + 1) * LANES, :] = kblk.astype(BF16)
                ksum = ksum + jnp.sum(kblk, axis=0, keepdims=True)
            means_s[j:j + 1, :] = ksum * (1.0 / MOBA_BLOCK)
            vt_s[j] = v_ref[:, j * MOBA_BLOCK:(j + 1) * MOBA_BLOCK].astype(BF16)

    q = q_ref[...]
    means = means_s[...]
    lane = lax.broadcasted_iota(jnp.int32, (1, LANES), 1)
    blk_i = lax.broadcasted_iota(jnp.int32, (nblk, tq), 0)
    kpos = lax.broadcasted_iota(jnp.int32, (MOBA_BLOCK, tq), 0)
    qpos = lax.broadcasted_iota(jnp.int32, (MOBA_BLOCK, tq), 1)
    outs = []
    for hh in range(HEADS_PER_VREG):
        qm = jnp.where((lane // ATTN_HEAD_DIM) == hh, q, 0.0)
        s_t = jnp.where(blk_i < own, _dot_nt_hp(means, qm), NEG_INF)
        for j in range(nblk):
            sj = s_t[j:j + 1, :]
            beats = jnp.where(s_t > sj, 1.0, 0.0) + jnp.where((s_t == sj) & (blk_i < j), 1.0, 0.0)
            rank = jnp.sum(beats, axis=0, keepdims=True)
            picked = jnp.where(rank < float(MOBA_TOPK), 1.0, 0.0) * jnp.where(j < own, 1.0, 0.0)
            sel_s[hh * nblk + j] = picked

        qb = (qm * ATTN_SCALE).astype(BF16)
        lg = jnp.where(kpos <= qpos, _dot_nt(kb_s[own], qb), NEG_INF)
        m = jnp.max(lg, axis=0, keepdims=True)
        p = jnp.exp(lg - m)
        l = jnp.sum(p, axis=0, keepdims=True)
        acc = _dot(vt_s[own], p.astype(BF16))

        def body(j, carry, hh=hh, qb=qb):
            m, l, acc = carry
            lg = jnp.where(sel_s[hh * nblk + j] > 0.5, _dot_nt(kb_s[j], qb), NEG_INF)
            m_new = jnp.maximum(m, jnp.max(lg, axis=0, keepdims=True))
            alpha = jnp.exp(m - m_new)
            p = jnp.exp(lg - m_new)
            l = alpha * l + jnp.sum(p, axis=0, keepdims=True)
            acc = alpha * acc + _dot(vt_s[j], p.astype(BF16))
            return m_new, l, acc

        m, l, acc = lax.fori_loop(0, own, body, (m, l, acc))
        outs.append(acc / l)
    d_i = lax.broadcasted_iota(jnp.int32, (LANES, tq), 0)
    o_t = jnp.where(d_i < ATTN_HEAD_DIM, outs[0], outs[1])
    o_ref[...] = o_t.T.astype(o_ref.dtype)


def _moba_prompt(q, k_t, v_t, nb, seq):
    assert seq % MOBA_BLOCK == 0
    nblk = seq // MOBA_BLOCK
    n_pairs = ATTN_DIM // LANES
    q3, k3, v3 = q.reshape(nb, seq, ATTN_DIM), k_t, v_t
    kv_spec = pl.BlockSpec((None, LANES, seq), lambda b, hp, t: (b, hp, 0))
    q_spec = pl.BlockSpec((None, MOBA_BLOCK, LANES), lambda b, hp, t: (b, t, hp))
    out = pl.pallas_call(
        _moba_prompt_kernel,
        out_shape=jax.ShapeDtypeStruct((nb, seq, ATTN_DIM), BF16),
        grid=(nb, n_pairs, nblk),
        in_specs=[q_spec, kv_spec, kv_spec],
        out_specs=q_spec,
        scratch_shapes=[pltpu.VMEM((nblk, MOBA_BLOCK, LANES), BF16),
                        pltpu.VMEM((nblk, LANES, MOBA_BLOCK), BF16),
                        pltpu.VMEM((nblk, LANES), F32),
                        pltpu.VMEM((HEADS_PER_VREG * nblk, 1, MOBA_BLOCK), F32)],
        compiler_params=_params(3),
        name="moba_prompt",
    )(q3, k3, v3)
    return out.reshape(nb * seq, ATTN_DIM)


PAGES_PER_STEP = 16
PAGES_PER_BLOCK = MOBA_BLOCK // PAGE_SIZE


def _page_sums_kernel(pt_ref, *refs):
    o_ref = refs[-1]
    i = pl.program_id(1)
    blocks_per_step = PAGES_PER_STEP // PAGES_PER_BLOCK

    @pl.when(i == 0)
    def _():
        o_ref[...] = jnp.zeros(o_ref.shape, F32)

    lane = lax.broadcasted_iota(jnp.int32, (1, LANES), 1)
    for lo in range(0, ATTN_DIM, LANES):
        rows = slice(lo, lo + LANES)
        acc = o_ref[rows, :]
        for t in range(blocks_per_step):
            pages = refs[t * PAGES_PER_BLOCK][rows, :]
            for u in range(1, PAGES_PER_BLOCK):
                pages = pages + refs[t * PAGES_PER_BLOCK + u][rows, :]
            acc = jnp.where(lane == i * blocks_per_step + t, jnp.sum(pages, axis=-1, keepdims=True), acc)
        o_ref[rows, :] = acc


def _page_sums(cache_t, page_table):
    nb, n_pages = page_table.shape
    assert n_pages % PAGES_PER_STEP == 0 and n_pages // PAGES_PER_BLOCK <= LANES
    specs = [pl.BlockSpec((None, ATTN_DIM, PAGE_SIZE), functools.partial(
        lambda b, i, pt, r: (pt[b, i * PAGES_PER_STEP + r], 0, 0), r=r)) for r in range(PAGES_PER_STEP)]
    return pl.pallas_call(
        _page_sums_kernel,
        out_shape=jax.ShapeDtypeStruct((nb, ATTN_DIM, LANES), F32),
        grid_spec=pltpu.PrefetchScalarGridSpec(
            num_scalar_prefetch=1, grid=(nb, n_pages // PAGES_PER_STEP), in_specs=specs,
            out_specs=pl.BlockSpec((None, ATTN_DIM, LANES), lambda b, i, pt: (b, 0, 0))),
        compiler_params=_params(2),
        name="page_sums",
    )(page_table, *([cache_t] * PAGES_PER_STEP))


def _sample_select_kernel(bsum_ref, q_ref, knew_ref, sel_ref, *, n_past):
    q = q_ref[...]
    head = lax.broadcasted_iota(jnp.int32, (N_ATTN_HEADS, ATTN_DIM), 0)
    chan_head = lax.broadcasted_iota(jnp.int32, (N_ATTN_HEADS, ATTN_DIM), 1) // ATTN_HEAD_DIM
    q_bd = jnp.where(head == chan_head, q, 0.0)
    s = _dot_hp(q_bd, bsum_ref[...] * (1.0 / MOBA_BLOCK))
    s_new = jnp.sum(q_bd * (knew_ref[...] * (1.0 / MOBA_BLOCK)), axis=-1, keepdims=True)
    lane = lax.broadcasted_iota(jnp.int32, s.shape, 1)
    own = n_past
    s = jnp.where(lane == n_past, s_new, s)
    s = jnp.where(lane < own, s, NEG_INF)
    removed = -jnp.inf
    s = jnp.where(lane <= n_past, s, removed)
    lane_f = lane.astype(F32)
    out = jnp.zeros(s.shape, jnp.int32)
    for r in range(MOBA_TOPK):
        mx = jnp.max(s, axis=-1, keepdims=True)
        idx = jnp.min(jnp.where(s == mx, lane_f, float(LANES)), axis=-1, keepdims=True)
        out = jnp.where(lane == r, idx.astype(jnp.int32), out)
        s = jnp.where(lane_f == idx, removed, s)
    sel_ref[...] = out


def _sample_select(bsum, q, k_new, n_past):
    nb = bsum.shape[0]
    assert n_past < LANES
    per_b = lambda *s: pl.BlockSpec((None,) + s, lambda b: (b,) + (0,) * len(s))
    return pl.pallas_call(
        functools.partial(_sample_select_kernel, n_past=n_past),
        out_shape=jax.ShapeDtypeStruct((nb, N_ATTN_HEADS, LANES), jnp.int32),
        grid=(nb,),
        in_specs=[per_b(ATTN_DIM, LANES), per_b(1, ATTN_DIM), per_b(1, ATTN_DIM)],
        out_specs=per_b(N_ATTN_HEADS, LANES),
        compiler_params=_params(1),
        name="moba_sample_select",
    )(bsum, q.reshape(nb, 1, ATTN_DIM), k_new.reshape(nb, 1, ATTN_DIM))


N_SEL_PAGES = MOBA_TOPK * PAGES_PER_BLOCK


def _sample_attend_kernel(pg_ref, ok_ref, q_ref, knew_ref, vnew_ref, *refs):
    o_ref = refs[-1]
    k_refs = refs[:HEADS_PER_VREG * N_SEL_PAGES]
    v_refs = refs[HEADS_PER_VREG * N_SEL_PAGES:2 * HEADS_PER_VREG * N_SEL_PAGES]
    b = pl.program_id(0)
    hp = pl.program_id(1)
    outs = []
    for hh in range(HEADS_PER_VREG):
        chans = slice(hh * ATTN_HEAD_DIM, (hh + 1) * ATTN_HEAD_DIM)
        qh = q_ref[:, chans] * ATTN_SCALE
        q8 = jnp.broadcast_to(qh, (SUBLANES, ATTN_HEAD_DIM)).astype(BF16)
        lgs = []
        for r in range(N_SEL_PAGES):
            ok = ok_ref[(b * N_ATTN_HEADS + hp * HEADS_PER_VREG + hh) * MOBA_TOPK + r // PAGES_PER_BLOCK]
            lg = _dot(q8, k_refs[hh * N_SEL_PAGES + r][...].astype(BF16))[0:1, :]
            lgs.append(jnp.where(ok > 0, lg, NEG_INF))
        lg_new = jnp.sum(qh * knew_ref[:, chans], axis=-1, keepdims=True)
        m = lg_new
        for lg in lgs:
            m = jnp.maximum(m, jnp.max(lg, axis=-1, keepdims=True))
        p_new = jnp.exp(lg_new - m)
        l = p_new
        acc = p_new * vnew_ref[:, chans]
        for r, lg in enumerate(lgs):
            p = jnp.exp(lg - m)
            l = l + jnp.sum(p, axis=-1, keepdims=True)
            p8 = jnp.broadcast_to(p, (SUBLANES, PAGE_SIZE)).astype(BF16)
            acc = acc + _dot_nt(p8, v_refs[hh * N_SEL_PAGES + r][...].astype(BF16))[0:1, :]
        outs.append(acc / l)
    o_ref[...] = jnp.concatenate(outs, axis=-1).astype(o_ref.dtype)


def _sample_attend(q, k_new, v_new, cache_k, cache_v, pages, ok):
    nb = q.shape[0]
    n_pairs = ATTN_DIM // LANES
    row = pl.BlockSpec((None, 1, LANES), lambda b, hp, pg, ok: (b, 0, hp))

    def page_spec(hh, r):
        return pl.BlockSpec((None, ATTN_HEAD_DIM, PAGE_SIZE), lambda b, hp, pg, ok: (
            pg[(b * N_ATTN_HEADS + hp * HEADS_PER_VREG + hh) * N_SEL_PAGES + r] * N_ATTN_HEADS
            + hp * HEADS_PER_VREG + hh, 0, 0))

    page_specs = [page_spec(hh, r) for hh in range(HEADS_PER_VREG) for r in range(N_SEL_PAGES)]
    out = pl.pallas_call(
        _sample_attend_kernel,
        out_shape=jax.ShapeDtypeStruct((nb, 1, ATTN_DIM), BF16),
        grid_spec=pltpu.PrefetchScalarGridSpec(
            num_scalar_prefetch=2, grid=(nb, n_pairs),
            in_specs=[row, row, row] + page_specs + page_specs, out_specs=row),
        compiler_params=_params(2),
        name="moba_sample_attend",
    )(pages, ok, q.reshape(nb, 1, ATTN_DIM), k_new.reshape(nb, 1, ATTN_DIM), v_new.reshape(nb, 1, ATTN_DIM),
      *([cache_k] * len(page_specs)), *([cache_v] * len(page_specs)))
    return out.reshape(nb, ATTN_DIM)


def _moba_sample(q, k_new, v_new, cache_k, cache_v, page_table):
    nb, n_pages = page_table.shape
    ck = jnp.transpose(cache_k, (0, 1, 3, 4, 2))
    cv = jnp.transpose(cache_v, (0, 1, 3, 4, 2))
    n_past = n_pages * PAGE_SIZE // MOBA_BLOCK
    assert n_pages * PAGE_SIZE == n_past * MOBA_BLOCK
    bsum = _page_sums(ck.reshape(-1, ATTN_DIM, PAGE_SIZE), page_table)
    sel = _sample_select(bsum, q, k_new, n_past)[:, :, :MOBA_TOPK]
    ck = ck.reshape(-1, ATTN_HEAD_DIM, PAGE_SIZE)
    cv = cv.reshape(-1, ATTN_HEAD_DIM, PAGE_SIZE)
    ok = (sel < n_past).astype(jnp.int32)
    blk = jnp.minimum(sel, n_past - 1)
    logical = blk[..., None] * PAGES_PER_BLOCK + jnp.arange(PAGES_PER_BLOCK)
    pages = jnp.take_along_axis(page_table, logical.reshape(nb, -1), axis=1)
    return _sample_attend(q, k_new, v_new, ck, cv, pages.reshape(-1), ok.reshape(-1))


def _merge_ln_kernel(x_ref, g_ref, ys_ref, ya_ref, ga_ref, gb_ref, ws_ref, wa_ref, wo_ref, lng_ref, lnb_ref, o_ref):
    merged = (jax.nn.sigmoid(ga_ref[...]) * _dot(ys_ref[...], ws_ref[...])
              + jax.nn.sigmoid(gb_ref[...]) * _dot(ya_ref[...], wa_ref[...]))
    mix = _dot(merged.astype(BF16), wo_ref[...])
    y = DEEPNORM_ALPHA * x_ref[...] + g_ref[...] * mix
    o_ref[...] = _layer_norm(y, lng_ref[...], lnb_ref[...])


def _merge_ln(x, mods, k, y_ssm, y_attn, ga, gb, ws, wa, wo, lng, lnb, tm, rows_per_batch):
    n, d = x.shape
    rows = lambda w: pl.BlockSpec((tm, w), lambda i: (i, 0))
    return pl.pallas_call(
        _merge_ln_kernel,
        out_shape=jax.ShapeDtypeStruct((n, d), F32),
        grid=(n // tm,),
        in_specs=[rows(d)] + _mod_specs(mods, (k,), tm, rows_per_batch)
        + [rows(D_INNER), rows(ATTN_DIM), rows(d), rows(d),
           _resident(ws.shape), _resident(wa.shape), _resident(wo.shape), _resident((1, d)), _resident((1, d))],
        out_specs=rows(d),
        compiler_params=_params(1),
        name="merge_ln",
    )(x, mods, y_ssm, y_attn, ga, gb, ws, wa, wo, lng.reshape(1, d), lnb.reshape(1, d))


PROMPT_TM = 512
IN_PROJ_TM = 256


def _split_w_in(w_in):
    edges = (0, D_INNER, D_INNER + CONV_DIM, D_INNER + CONV_DIM + N_SSM_HEADS)
    edges = edges + tuple(edges[-1] + i * ATTN_DIM for i in range(1, 6))
    parts = [w_in[:, a:b] for a, b in zip(edges[:-1], edges[1:])]
    parts[2] = jnp.pad(parts[2], ((0, 0), (0, LANES - N_SSM_HEADS)))
    return [p.astype(BF16) for p in parts]


def kernel(x_prompt, x_sample, cache_k, cache_v, state_conv, state_ssm, page_table, c_prompt, c_sample, w_ada, b_ada, ln_g, ln_b, w_ffn1_gu, w_ffn1_down, w_ffn2_gu, w_ffn2_down, w_in, conv_w, conv_b, dt_bias, a_log, d_skip, ssm_norm_g, w_branch_ssm, w_branch_attn, w_out):
    assert w_in.shape[0] == DEPTH == 1
    nb, seq, d = x_prompt.shape
    nbs, dec_seq, _ = x_sample.shape
    assert dec_seq == 1
    l = 0
    w1gu, w1d = w_ffn1_gu[l].astype(BF16), w_ffn1_down[l].astype(BF16)
    w2gu, w2d = w_ffn2_gu[l].astype(BF16), w_ffn2_down[l].astype(BF16)
    w_parts = _split_w_in(w_in[l])
    ws, wa, wo = w_branch_ssm[l].astype(BF16), w_branch_attn[l].astype(BF16), w_out[l].astype(BF16)
    ssd_w = (conv_w[l], conv_b[l], dt_bias[l], a_log[l], d_skip[l], ssm_norm_g[l])

    mods = _ada_mods(jnp.concatenate([c_prompt, c_sample], axis=0), w_ada[l], b_ada[l])
    mods_p = mods[:nb].reshape(nb * 9, 1, d)
    mods_s = mods[nb:]

    xp = x_prompt.reshape(nb * seq, d)
    xp = _ffn_ln(xp, mods_p, (0, 1, 2), w1gu, w1d, ln_g[l, 0], ln_b[l, 0], PROMPT_TM, seq)
    kv_t = (False, False, False, False, True, True, False, False)
    w_parts_p = [w.T if t else w for w, t in zip(w_parts, kv_t)]
    z, xbc, dt, q, k_t, v_t, ga, gb = _in_proj(xp, mods_p, (3, 4), w_parts_p, kv_t, IN_PROJ_TM, seq)
    y_ssm, conv_p, ssm_p = _ssd_prompt(xbc, z, dt, nb, seq, *ssd_w)
    y_attn = _moba_prompt(q, k_t, v_t, nb, seq)
    xp = _merge_ln(xp, mods_p, 5, y_ssm, y_attn, ga, gb, ws, wa, wo, ln_g[l, 1], ln_b[l, 1], PROMPT_TM, seq)
    xp = _ffn_ln(xp, mods_p, (6, 7, 8), w2gu, w2d, ln_g[l, 2], ln_b[l, 2], PROMPT_TM, seq)

    xs = x_sample.reshape(nbs, d)
    xs = _ffn_ln(xs, mods_s, (0, 1, 2), w1gu, w1d, ln_g[l, 0], ln_b[l, 0], nbs, 1)
    zs, xbcs, dts, qs, ks, vs, gas, gbs = _in_proj(xs, mods_s, (3, 4), w_parts, (False,) * len(w_parts), nbs, 1)
    y_ssm_s, conv_s, ssm_s = _ssd_sample(xbcs, zs, dts, state_conv[l], state_ssm[l], *ssd_w)
    y_attn_s = _moba_sample(qs, ks, vs, cache_k, cache_v, page_table)
    xs = _merge_ln(xs, mods_s, 5, y_ssm_s.reshape(nbs, D_INNER), y_attn_s, gas, gbs, ws, wa, wo,
                   ln_g[l, 1], ln_b[l, 1], nbs, 1)
    xs = _ffn_ln(xs, mods_s, (6, 7, 8), w2gu, w2d, ln_g[l, 2], ln_b[l, 2], nbs, 1)

    heads = (N_ATTN_HEADS, ATTN_HEAD_DIM)
    state = (N_SSM_HEADS, SSM_HEAD_DIM, SSM_STATE)
    to_rows = lambda a_t: jnp.transpose(a_t.reshape((1, nb) + heads + (seq,)), (0, 1, 4, 2, 3))
    return (xp.reshape(nb, seq, d), xs.reshape(nbs, 1, d), to_rows(k_t), to_rows(v_t),
            conv_p[None], ssm_p.reshape((1, nb) + state),
            ks.reshape((1, nbs, 1) + heads), vs.reshape((1, nbs, 1) + heads),
            conv_s[None], ssm_s.reshape((1, nbs) + state))
```

```python
import functools

import jax
import jax.numpy as jnp
from jax import lax
from jax.experimental import pallas as pl
from jax.experimental.pallas import tpu as pltpu

F32 = jnp.float32
BF16 = jnp.bfloat16

D_MODEL = 1024
D_INNER = 2048
SSM_HEAD_DIM = 64
N_SSM_HEADS = 32
N_SSM_GROUPS = 4
SSM_STATE = 128
CONV_WIDTH = 4
CONV_DIM = D_INNER + 2 * N_SSM_GROUPS * SSM_STATE
SSD_CHUNK = 128
ATTN_HEAD_DIM = 64
N_ATTN_HEADS = 16
ATTN_DIM = 1024
MOBA_BLOCK = 256
MOBA_TOPK = 3
PAGE_SIZE = 128
D_FF = 2816
DEPTH = 1
DEEPNORM_ALPHA = (2.0 * DEPTH) ** 0.25
LN_EPS = 1e-5
RMS_EPS = 1e-5
NEG_INF = -1e30
ATTN_SCALE = ATTN_HEAD_DIM ** -0.5

LANES = 128
SUBLANES = 8
VMEM_LIMIT = 56 * 1024 * 1024


def _dot(a, b):
    return jnp.dot(a, b, preferred_element_type=F32)


def _dot_nt(a, b):
    return lax.dot_general(a, b, (((1,), (1,)), ((), ())), preferred_element_type=F32)


def _split2(a):
    hi = a.astype(BF16)
    lo = (a - hi.astype(F32)).astype(BF16)
    return hi, lo


def _split3(a):
    hi = a.astype(BF16)
    r = a - hi.astype(F32)
    mid = r.astype(BF16)
    lo = (r - mid.astype(F32)).astype(BF16)
    return hi, mid, lo


def _dot_sel(a, sel_bf16):
    hi, mid, lo = _split3(a)
    return _dot(hi, sel_bf16) + _dot(mid, sel_bf16) + _dot(lo, sel_bf16)


def _dot_sel2(a, sel_bf16):
    hi, lo = _split2(a)
    return _dot(hi, sel_bf16) + _dot(lo, sel_bf16)


def _dot_hp(a, b):
    ah, al = _split2(a)
    bh, bl = _split2(b)
    return _dot(ah, bh) + _dot(al, bh) + _dot(ah, bl)


def _dot_nt_hp(a, b):
    ah, al = _split2(a)
    bh, bl = _split2(b)
    return _dot_nt(ah, bh) + _dot_nt(al, bh) + _dot_nt(ah, bl)


def _silu(x):
    return x * jax.nn.sigmoid(x)


def _softplus(x):
    return jnp.maximum(x, 0.0) + jnp.log1p(jnp.exp(-jnp.abs(x)))


def _layer_norm(y, g, b):
    mu = jnp.mean(y, axis=-1, keepdims=True)
    yc = y - mu
    var = jnp.mean(yc * yc, axis=-1, keepdims=True)
    return yc * lax.rsqrt(var + LN_EPS) * g + b


def _resident(shape):
    nd = len(shape)
    return pl.BlockSpec(shape, lambda *_: (0,) * nd, pipeline_mode=pl.Buffered(1))


def _params(n_axes):
    return pltpu.CompilerParams(dimension_semantics=("arbitrary",) * n_axes, vmem_limit_bytes=VMEM_LIMIT)


def _ada_kernel(c_ref, w_ref, b_ref, o_ref):
    s = _silu(c_ref[...])
    o_ref[...] = _dot_hp(s, w_ref[...]) + b_ref[...]


def _ada_mods(c, w_ada, b_ada):
    m, d = c.shape
    n = w_ada.shape[1]
    tn = 1024
    return pl.pallas_call(
        _ada_kernel,
        out_shape=jax.ShapeDtypeStruct((m, n), F32),
        grid=(n // tn,),
        in_specs=[pl.BlockSpec((m, d), lambda j: (0, 0)),
                  pl.BlockSpec((d, tn), lambda j: (0, j)),
                  pl.BlockSpec((1, tn), lambda j: (0, j))],
        out_specs=pl.BlockSpec((m, tn), lambda j: (0, j)),
        compiler_params=_params(1),
        name="ada_mods",
    )(c, w_ada, b_ada.reshape(1, n))


def _mod_specs(mods, ks, tm, rows_per_batch):
    if mods.ndim == 3:
        tiles_per_batch = rows_per_batch // tm
        return [pl.BlockSpec((None, 1, D_MODEL), functools.partial(
            lambda i, k: ((i // tiles_per_batch) * 9 + k, 0, 0), k=k)) for k in ks]
    return [pl.BlockSpec((tm, D_MODEL), functools.partial(lambda i, k: (i, k), k=k)) for k in ks]


FF_CHUNK = 1408


def _ffn_ln_kernel(x_ref, sh_ref, sc_ref, g_ref, wgu_ref, wd_ref, lng_ref, lnb_ref, o_ref):
    x = x_ref[...]
    h = (x * (1.0 + sc_ref[...]) + sh_ref[...]).astype(BF16)
    acc = jnp.zeros(x.shape, F32)
    for j in range(D_FF // FF_CHUNK):
        lo = j * FF_CHUNK
        gate = _dot(h, wgu_ref[:, lo:lo + FF_CHUNK])
        up = _dot(h, wgu_ref[:, D_FF + lo:D_FF + lo + FF_CHUNK])
        act = (_silu(gate) * up).astype(BF16)
        acc = acc + _dot(act, wd_ref[lo:lo + FF_CHUNK, :])
    y = DEEPNORM_ALPHA * x + 0.5 * g_ref[...] * acc
    o_ref[...] = _layer_norm(y, lng_ref[...], lnb_ref[...])


def _ffn_ln(x, mods, ks, wgu, wd, lng, lnb, tm, rows_per_batch):
    n, d = x.shape
    row = pl.BlockSpec((tm, d), lambda i: (i, 0))
    return pl.pallas_call(
        _ffn_ln_kernel,
        out_shape=jax.ShapeDtypeStruct((n, d), F32),
        grid=(n // tm,),
        in_specs=[row] + _mod_specs(mods, ks, tm, rows_per_batch)
        + [_resident(wgu.shape), _resident(wd.shape), _resident((1, d)), _resident((1, d))],
        out_specs=row,
        compiler_params=_params(1),
        name="ffn_ln",
    )(x, mods, mods, mods, wgu, wd, lng.reshape(1, d), lnb.reshape(1, d))


CARRY = SUBLANES
CONV_COLS = 256
PROJ_PIECE = 512


def _in_proj_kernel(x_ref, sh_ref, sc_ref, *refs, transposed, conv_index, tiles_per_batch):
    n_out = len(transposed)
    w_refs, o_refs = refs[:n_out], refs[n_out + 2 * (conv_index is not None):]
    h = (x_ref[...] * (1.0 + sc_ref[...]) + sh_ref[...]).astype(BF16)

    def project(idx, lo=None, hi=None):
        if transposed[idx]:
            return _dot_nt(w_refs[idx][lo:hi, :], h)
        return _dot(h, w_refs[idx][:, lo:hi])

    conv_chunks = []
    if conv_index is not None:
        cw_ref, cb_ref = refs[n_out:n_out + 2]
        state_ref, buf_s = refs[-2:]
        tm = x_ref.shape[0]
        tile = pl.program_id(0) % tiles_per_batch

        @pl.when(tile == 0)
        def _():
            buf_s[0:CARRY, :] = jnp.zeros((CARRY, CONV_DIM), F32)

        buf_s[CARRY:CARRY + tm, :] = project(conv_index)

        def conv_chunk(lo):
            cols = slice(lo, lo + CONV_COLS)
            acc = cb_ref[:, cols]
            for j in range(CONV_WIDTH):
                off = CARRY - (CONV_WIDTH - 1) + j
                acc = acc + cw_ref[j:j + 1, cols] * buf_s[off:off + tm, cols]
            o_refs[conv_index][:, cols] = _silu(acc)

        conv_chunks = list(range(0, CONV_DIM, CONV_COLS))

    for idx in range(n_out):
        if idx == conv_index:
            continue
        width = w_refs[idx].shape[0] if transposed[idx] else w_refs[idx].shape[1]
        for lo in range(0, width, PROJ_PIECE):
            hi = min(lo + PROJ_PIECE, width)
            piece = project(idx, lo, hi).astype(o_refs[idx].dtype)
            if transposed[idx]:
                o_refs[idx][lo:hi, :] = piece
            else:
                o_refs[idx][:, lo:hi] = piece
            if conv_chunks:
                conv_chunk(conv_chunks.pop(0))
    for lo in conv_chunks:
        conv_chunk(lo)

    if conv_index is not None:
        @pl.when(tile == tiles_per_batch - 1)
        def _():
            state_ref[...] = buf_s[CARRY + tm - (CONV_WIDTH - 1):CARRY + tm, :]

        buf_s[0:CARRY, :] = buf_s[tm:tm + CARRY, :]


def _in_proj(x, mods, ks, weights, transposed, tm, rows_per_batch, conv=None):
    n, d = x.shape
    nb = n // rows_per_batch
    tiles_per_batch = max(rows_per_batch // tm, 1)
    row = pl.BlockSpec((tm, d), lambda i: (i, 0))
    out_shape, out_specs = [], []
    for w, t in zip(weights, transposed):
        if t:
            out_shape.append(jax.ShapeDtypeStruct((nb, w.shape[0], rows_per_batch), F32))
            out_specs.append(pl.BlockSpec((None, w.shape[0], tm),
                                          lambda i: (i // tiles_per_batch, 0, i % tiles_per_batch)))
        else:
            out_shape.append(jax.ShapeDtypeStruct((n, w.shape[1]), F32))
            out_specs.append(pl.BlockSpec((tm, w.shape[1]), lambda i: (i, 0)))
    in_specs = [row] + _mod_specs(mods, ks, tm, rows_per_batch) + [_resident(w.shape) for w in weights]
    args = [x, mods, mods, *weights]
    scratch = []
    if conv is not None:
        conv_index, conv_w, conv_b = conv
        in_specs += [_resident((CONV_WIDTH, CONV_DIM)), _resident((1, CONV_DIM))]
        args += [conv_w, conv_b.reshape(1, CONV_DIM)]
        out_shape.append(jax.ShapeDtypeStruct((nb, CONV_WIDTH - 1, CONV_DIM), F32))
        out_specs.append(pl.BlockSpec((None, CONV_WIDTH - 1, CONV_DIM), lambda i: (i // tiles_per_batch, 0, 0)))
        scratch.append(pltpu.VMEM((CARRY + tm, CONV_DIM), F32))
    return pl.pallas_call(
        functools.partial(_in_proj_kernel, transposed=tuple(transposed),
                          conv_index=None if conv is None else conv[0], tiles_per_batch=tiles_per_batch),
        out_shape=out_shape,
        grid=(n // tm,),
        in_specs=in_specs,
        out_specs=out_specs,
        scratch_shapes=scratch,
        compiler_params=_params(1),
        name="in_proj",
    )(*args)


GROUP_W = D_INNER // N_SSM_GROUPS


def _gated_rmsnorm(y, z, norm_g):
    yz = y * _silu(z)
    outs = []
    for g in range(N_SSM_GROUPS):
        blk = yz[:, g * GROUP_W:(g + 1) * GROUP_W]
        ms = jnp.mean(blk * blk, axis=-1, keepdims=True)
        outs.append(blk * lax.rsqrt(ms + RMS_EPS))
    return jnp.concatenate(outs, axis=-1) * norm_g


def _ssd_prompt_kernel(xc_s, z_ref, dt_ref, dtb_ref, alog_ref, dskip_ref, ng_ref, exp_ref, y_ref, ssm_ref, st_s):
    c = pl.program_id(1)
    q = SSD_CHUNK

    @pl.when(c == 0)
    def _():
        st_s[...] = jnp.zeros(st_s.shape, F32)

    expand = exp_ref[...]
    dt = _softplus(dt_ref[...] + dtb_ref[...])
    da = dt * (-jnp.exp(alog_ref[...]))
    row_i = lax.broadcasted_iota(jnp.int32, (q, q), 0)
    col_i = lax.broadcasted_iota(jnp.int32, (q, q), 1)
    causal = row_i >= col_i
    tril = jnp.where(causal, 1.0, 0.0).astype(BF16)
    a_cum = _dot_tril(tril, da)
    a_src = a_cum - jnp.log(dt)
    a_src_t = a_src.T
    a_last = a_cum[q - 1:q, :]
    dec_end_x = _dot_sel2(jnp.exp(a_last - a_src), expand)
    exp_a_x = _dot_sel2(jnp.exp(a_cum), expand)
    chunk_decay_x = _dot_sel(jnp.broadcast_to(jnp.exp(a_last), (SUBLANES, LANES)), expand)[0:1, :]

    xs = xc_s[:, 0:D_INNER]
    xs_b = xs.astype(BF16)
    xdec_b = (xs * dec_end_x).astype(BF16)
    lane = lax.broadcasted_iota(jnp.int32, (1, LANES), 1)
    zero_b = jnp.zeros((), BF16)

    y_parts = []
    for g in range(N_SSM_GROUPS):
        b_g = xc_s[:, D_INNER + g * SSM_STATE:D_INNER + (g + 1) * SSM_STATE]
        c_g = xc_s[:, D_INNER + (N_SSM_GROUPS + g) * SSM_STATE:D_INNER + (N_SSM_GROUPS + g + 1) * SSM_STATE]
        b_gb = b_g.astype(BF16)
        c_gb = c_g.astype(BF16)
        cb = _dot_nt(c_gb, b_gb)
        gcols = slice(g * GROUP_W, (g + 1) * GROUP_W)
        st_old = st_s[:, gcols]
        y_inter = _dot(c_gb, st_old.astype(BF16)) * exp_a_x[:, gcols]
        st_s[:, gcols] = chunk_decay_x[:, gcols] * st_old + _dot(b_g.T.astype(BF16), xdec_b[:, gcols])
        pair_out = []
        heads_per_group = N_SSM_HEADS // N_SSM_GROUPS
        for pr in range(heads_per_group // 2):
            pcols = slice(g * GROUP_W + pr * LANES, g * GROUP_W + (pr + 1) * LANES)
            x_pair = xs_b[:, pcols]
            acc = None
            for hh in range(2):
                h = g * heads_per_group + pr * 2 + hh
                seg = a_cum[:, h:h + 1] - a_src_t[h:h + 1, :]
                w = (cb * jnp.exp(jnp.where(causal, seg, -jnp.inf))).astype(BF16)
                x_h = jnp.where((lane // SSM_HEAD_DIM) == hh, x_pair, zero_b)
                t = _dot(w, x_h)
                acc = t if acc is None else acc + t
            pair_out.append(acc)
        y_parts.append(jnp.concatenate(pair_out, axis=-1) + y_inter)
    y = jnp.concatenate(y_parts, axis=-1) + dskip_ref[...] * xs
    y_ref[...] = _gated_rmsnorm(y, z_ref[...], ng_ref[...]).astype(y_ref.dtype)

    @pl.when(c == pl.num_programs(1) - 1)
    def _():
        for r in range(D_INNER // LANES):
            ssm_ref[r * LANES:(r + 1) * LANES, :] = st_s[:, r * LANES:(r + 1) * LANES].T


def _dot_tril(tril_bf16, a):
    hi, mid, lo = _split3(a)
    return _dot(tril_bf16, hi) + _dot(tril_bf16, mid) + _dot(tril_bf16, lo)


def _pad_lanes(v):
    return jnp.zeros((1, LANES), F32).at[0, :v.shape[0]].set(v)


def _expand_matrix():
    h = jnp.arange(LANES)[:, None]
    ch = jnp.arange(D_INNER)[None, :] // SSM_HEAD_DIM
    return (h == ch).astype(BF16)


def _ssd_prompt(xc, z, dt, nb, seq, dt_bias, a_log, d_skip, norm_g):
    nc = seq // SSD_CHUNK
    rows = lambda w: pl.BlockSpec((SSD_CHUNK, w), lambda b, c: (b * nc + c, 0))
    return pl.pallas_call(
        _ssd_prompt_kernel,
        out_shape=[jax.ShapeDtypeStruct((nb * seq, D_INNER), BF16),
                   jax.ShapeDtypeStruct((nb, D_INNER, SSM_STATE), F32)],
        grid=(nb, nc),
        in_specs=[rows(CONV_DIM), rows(D_INNER), rows(LANES), _resident((1, LANES)),
                  _resident((1, LANES)), _resident((1, D_INNER)), _resident((1, D_INNER)),
                  _resident((LANES, D_INNER))],
        out_specs=[rows(D_INNER), pl.BlockSpec((None, D_INNER, SSM_STATE), lambda b, c: (b, 0, 0))],
        scratch_shapes=[pltpu.VMEM((SSM_STATE, D_INNER), F32)],
        compiler_params=_params(2),
        name="ssd_prompt",
    )(xc, z, dt, _pad_lanes(dt_bias), _pad_lanes(a_log),
      jnp.repeat(d_skip, SSM_HEAD_DIM).reshape(1, D_INNER), norm_g.reshape(1, D_INNER), _expand_matrix())


def _column_block(row_vec, r):
    return jnp.broadcast_to(row_vec[:, r * LANES:(r + 1) * LANES], (LANES, LANES)).T


def _ssd_sample_kernel(xbc_ref, z_ref, dt_ref, buf_ref, h0_ref, cw_ref, cb_ref, dtb_ref, alog_ref, dskip_ref,
                       ng_ref, exp_ref, y_ref, conv_ref, ssm_ref):
    xr = xbc_ref[...]
    acc = cb_ref[...] + cw_ref[CONV_WIDTH - 1:CONV_WIDTH, :] * xr
    for j in range(CONV_WIDTH - 1):
        acc = acc + cw_ref[j:j + 1, :] * buf_ref[j:j + 1, :]
    xc = _silu(acc)
    for j in range(CONV_WIDTH - 2):
        conv_ref[j:j + 1, :] = buf_ref[j + 1:j + 2, :]
    conv_ref[CONV_WIDTH - 2:CONV_WIDTH - 1, :] = xr

    expand = exp_ref[...]
    dt = _softplus(dt_ref[...] + dtb_ref[...])
    dec = jnp.exp(dt * (-jnp.exp(alog_ref[...])))
    both = jnp.concatenate([jnp.broadcast_to(dt, (SUBLANES // 2, LANES)),
                            jnp.broadcast_to(dec, (SUBLANES // 2, LANES))], axis=0)
    both_x = _dot_sel(both, expand)
    dt_x = both_x[0:1, :]
    dec_x = both_x[SUBLANES // 2:SUBLANES // 2 + 1, :]
    xs = xc[:, 0:D_INNER]
    xdt = xs * dt_x

    y_parts = []
    blocks_per_group = GROUP_W // LANES
    for g in range(N_SSM_GROUPS):
        b_g = xc[:, D_INNER + g * SSM_STATE:D_INNER + (g + 1) * SSM_STATE]
        c_g = xc[:, D_INNER + (N_SSM_GROUPS + g) * SSM_STATE:D_INNER + (N_SSM_GROUPS + g + 1) * SSM_STATE]
        c8 = jnp.broadcast_to(c_g, (SUBLANES, SSM_STATE)).astype(BF16)
        for rb in range(blocks_per_group):
            r = g * blocks_per_group + rb
            rows = slice(r * LANES, (r + 1) * LANES)
            h_new = _column_block(dec_x, r) * h0_ref[rows, :] + _column_block(xdt, r) * b_g
            ssm_ref[rows, :] = h_new
            y_parts.append(_dot_nt(c8, h_new.astype(BF16))[0:1, :])
    y = jnp.concatenate(y_parts, axis=-1) + dskip_ref[...] * xs
    y_ref[...] = _gated_rmsnorm(y, z_ref[...], ng_ref[...]).astype(y_ref.dtype)


def _ssd_sample(xbc, z, dt, state_conv, state_ssm, conv_w, conv_b, dt_bias, a_log, d_skip, norm_g):
    nb = xbc.shape[0]
    per_b = lambda *s: pl.BlockSpec((None,) + s, lambda b: (b,) + (0,) * len(s))
    return pl.pallas_call(
        _ssd_sample_kernel,
        out_shape=[jax.ShapeDtypeStruct((nb, 1, D_INNER), BF16),
                   jax.ShapeDtypeStruct((nb, CONV_WIDTH - 1, CONV_DIM), F32),
                   jax.ShapeDtypeStruct((nb, D_INNER, SSM_STATE), F32)],
        grid=(nb,),
        in_specs=[per_b(1, CONV_DIM), per_b(1, D_INNER), per_b(1, LANES),
                  per_b(CONV_WIDTH - 1, CONV_DIM), per_b(D_INNER, SSM_STATE),
                  _resident((CONV_WIDTH, CONV_DIM)), _resident((1, CONV_DIM)), _resident((1, LANES)),
                  _resident((1, LANES)), _resident((1, D_INNER)), _resident((1, D_INNER)),
                  _resident((LANES, D_INNER))],
        out_specs=[per_b(1, D_INNER), per_b(CONV_WIDTH - 1, CONV_DIM), per_b(D_INNER, SSM_STATE)],
        compiler_params=_params(1),
        name="ssd_sample",
    )(xbc.reshape(nb, 1, CONV_DIM), z.reshape(nb, 1, D_INNER), dt.reshape(nb, 1, LANES), state_conv,
      state_ssm.reshape(nb, D_INNER, SSM_STATE), conv_w, conv_b.reshape(1, CONV_DIM), _pad_lanes(dt_bias),
      _pad_lanes(a_log), jnp.repeat(d_skip, SSM_HEAD_DIM).reshape(1, D_INNER), norm_g.reshape(1, D_INNER),
      _expand_matrix())


HEADS_PER_VREG = LANES // ATTN_HEAD_DIM
MOBA_PAIRS_PER_STEP = 2
MOBA_HEADS_PER_STEP = MOBA_PAIRS_PER_STEP * HEADS_PER_VREG
MOBA_STEP_CHANNELS = MOBA_PAIRS_PER_STEP * LANES


def _moba_prompt_kernel(q_ref, k_ref, v_ref, o_ref, kb_s, vt_s, means_s, sel_s, qb_s, m_s, l_s, acc_s):
    tq = MOBA_BLOCK
    nblk = kb_s.shape[0]
    own = pl.program_id(2)
    pair_cols = [slice(pr * LANES, (pr + 1) * LANES) for pr in range(MOBA_PAIRS_PER_STEP)]
    head_rows = [slice(h * ATTN_HEAD_DIM, (h + 1) * ATTN_HEAD_DIM) for h in range(MOBA_HEADS_PER_STEP)]

    @pl.when(own == 0)
    def _():
        for j in range(nblk):
            for pr in range(MOBA_PAIRS_PER_STEP):
                ksum = jnp.zeros((1, LANES), F32)
                for t in range(MOBA_BLOCK // LANES):
                    cols = slice(j * MOBA_BLOCK + t * LANES, j * MOBA_BLOCK + (t + 1) * LANES)
                    kblk = k_ref[pair_cols[pr], cols].T
                    kb_s[j, t * LANES:(t + 1) * LANES, pair_cols[pr]] = kblk.astype(BF16)
                    ksum = ksum + jnp.sum(kblk, axis=0, keepdims=True)
                means_s[j:j + 1, pair_cols[pr]] = ksum * (1.0 / MOBA_BLOCK)
            vt_s[j] = v_ref[:, j * MOBA_BLOCK:(j + 1) * MOBA_BLOCK].astype(BF16)

    lane = lax.broadcasted_iota(jnp.int32, (1, LANES), 1)
    blk_i = lax.broadcasted_iota(jnp.int32, (nblk, tq), 0)
    causal = (lax.broadcasted_iota(jnp.int32, (MOBA_BLOCK, tq), 0)
              <= lax.broadcasted_iota(jnp.int32, (MOBA_BLOCK, tq), 1))
    heads = range(MOBA_HEADS_PER_STEP)
    cols_of = [pair_cols[h // HEADS_PER_VREG] for h in heads]
    qms = [jnp.where((lane // ATTN_HEAD_DIM) == h % HEADS_PER_VREG, q_ref[:, cols_of[h]], 0.0) for h in heads]
    qbs = [(qm * ATTN_SCALE).astype(BF16) for qm in qms]
    for h in heads:
        qb_s[h] = qbs[h]
    raw_own = [_dot_nt(kb_s[own, :, cols_of[h]], qbs[h]) for h in heads]
    scores = [_dot_nt_hp(means_s[:, cols_of[h]], qms[h]) for h in heads]
    for h in heads:
        s_t = jnp.where(blk_i < own, scores[h], NEG_INF)
        for j in range(nblk):
            sj = s_t[j:j + 1, :]
            beats = jnp.where(s_t > sj, 1.0, 0.0) + jnp.where((s_t == sj) & (blk_i < j), 1.0, 0.0)
            rank = jnp.sum(beats, axis=0, keepdims=True)
            sel_s[h * nblk + j] = jnp.where(rank < float(MOBA_TOPK), 1.0, 0.0) * jnp.where(j < own, 1.0, 0.0)
    ps = []
    for h in heads:
        lg = jnp.where(causal, raw_own[h], NEG_INF)
        m = jnp.max(lg, axis=0, keepdims=True)
        p = jnp.exp(lg - m)
        m_s[h] = m
        l_s[h] = jnp.sum(p, axis=0, keepdims=True)
        ps.append(p.astype(BF16))
    for h in heads:
        acc_s[h] = _dot(vt_s[own, head_rows[h], :], ps[h])

    def body(j, carry):
        raw = [_dot_nt(kb_s[j, :, cols_of[h]], qb_s[h]) for h in heads]
        ps, alphas = [], []
        for h in heads:
            lg = jnp.where(sel_s[h * nblk + j] > 0.5, raw[h], NEG_INF)
            m_old = m_s[h]
            m_new = jnp.maximum(m_old, jnp.max(lg, axis=0, keepdims=True))
            alpha = jnp.exp(m_old - m_new)
            p = jnp.exp(lg - m_new)
            m_s[h] = m_new
            l_s[h] = alpha * l_s[h] + jnp.sum(p, axis=0, keepdims=True)
            ps.append(p.astype(BF16))
            alphas.append(alpha)
        pvs = [_dot(vt_s[j, head_rows[h], :], ps[h]) for h in heads]
        for h in heads:
            acc_s[h] = alphas[h] * acc_s[h] + pvs[h]
        return carry

    lax.fori_loop(0, own, body, 0)
    for pr in range(MOBA_PAIRS_PER_STEP):
        heads = range(pr * HEADS_PER_VREG, (pr + 1) * HEADS_PER_VREG)
        o_t = jnp.concatenate([acc_s[h] / l_s[h] for h in heads], axis=0)
        o_ref[:, pair_cols[pr]] = o_t.T.astype(o_ref.dtype)


def _moba_prompt(q, k_t, v_t, nb, seq):
    assert seq % MOBA_BLOCK == 0
    nblk = seq // MOBA_BLOCK
    gw = MOBA_STEP_CHANNELS
    q3, k3, v3 = q.reshape(nb, seq, ATTN_DIM), k_t, v_t
    kv_spec = pl.BlockSpec((None, gw, seq), lambda b, g, t: (b, g, 0))
    q_spec = pl.BlockSpec((None, MOBA_BLOCK, gw), lambda b, g, t: (b, t, g))
    out = pl.pallas_call(
        _moba_prompt_kernel,
        out_shape=jax.ShapeDtypeStruct((nb, seq, ATTN_DIM), BF16),
        grid=(nb, ATTN_DIM // gw, nblk),
        in_specs=[q_spec, kv_spec, kv_spec],
        out_specs=q_spec,
        scratch_shapes=[pltpu.VMEM((nblk, MOBA_BLOCK, gw), BF16),
                        pltpu.VMEM((nblk, gw, MOBA_BLOCK), BF16),
                        pltpu.VMEM((nblk, gw), F32),
                        pltpu.VMEM((MOBA_HEADS_PER_STEP * nblk, 1, MOBA_BLOCK), F32),
                        pltpu.VMEM((MOBA_HEADS_PER_STEP, MOBA_BLOCK, LANES), BF16),
                        pltpu.VMEM((MOBA_HEADS_PER_STEP, 1, MOBA_BLOCK), F32),
                        pltpu.VMEM((MOBA_HEADS_PER_STEP, 1, MOBA_BLOCK), F32),
                        pltpu.VMEM((MOBA_HEADS_PER_STEP, ATTN_HEAD_DIM, MOBA_BLOCK), F32)],
        compiler_params=_params(3),
        name="moba_prompt",
    )(q3, k3, v3)
    return out.reshape(nb * seq, ATTN_DIM)


PAGES_PER_STEP = 16
PAGES_PER_BLOCK = MOBA_BLOCK // PAGE_SIZE


def _page_sums_kernel(pt_ref, *refs):
    o_ref = refs[-1]
    i = pl.program_id(1)
    blocks_per_step = PAGES_PER_STEP // PAGES_PER_BLOCK

    @pl.when(i == 0)
    def _():
        o_ref[...] = jnp.zeros(o_ref.shape, F32)

    lane = lax.broadcasted_iota(jnp.int32, (1, LANES), 1)
    for lo in range(0, ATTN_DIM, LANES):
        rows = slice(lo, lo + LANES)
        acc = o_ref[rows, :]
        for t in range(blocks_per_step):
            pages = refs[t * PAGES_PER_BLOCK][rows, :]
            for u in range(1, PAGES_PER_BLOCK):
                pages = pages + refs[t * PAGES_PER_BLOCK + u][rows, :]
            acc = jnp.where(lane == i * blocks_per_step + t, jnp.sum(pages, axis=-1, keepdims=True), acc)
        o_ref[rows, :] = acc


def _page_sums(cache_t, page_table):
    nb, n_pages = page_table.shape
    assert n_pages % PAGES_PER_STEP == 0 and n_pages // PAGES_PER_BLOCK <= LANES
    specs = [pl.BlockSpec((None, ATTN_DIM, PAGE_SIZE), functools.partial(
        lambda b, i, pt, r: (pt[b, i * PAGES_PER_STEP + r], 0, 0), r=r)) for r in range(PAGES_PER_STEP)]
    return pl.pallas_call(
        _page_sums_kernel,
        out_shape=jax.ShapeDtypeStruct((nb, ATTN_DIM, LANES), F32),
        grid_spec=pltpu.PrefetchScalarGridSpec(
            num_scalar_prefetch=1, grid=(nb, n_pages // PAGES_PER_STEP), in_specs=specs,
            out_specs=pl.BlockSpec((None, ATTN_DIM, LANES), lambda b, i, pt: (b, 0, 0))),
        compiler_params=_params(2),
        name="page_sums",
    )(page_table, *([cache_t] * PAGES_PER_STEP))


def _sample_select_kernel(bsum_ref, q_ref, knew_ref, sel_ref, *, n_past):
    q = q_ref[...]
    head = lax.broadcasted_iota(jnp.int32, (N_ATTN_HEADS, ATTN_DIM), 0)
    chan_head = lax.broadcasted_iota(jnp.int32, (N_ATTN_HEADS, ATTN_DIM), 1) // ATTN_HEAD_DIM
    q_bd = jnp.where(head == chan_head, q, 0.0)
    s = _dot_hp(q_bd, bsum_ref[...] * (1.0 / MOBA_BLOCK))
    s_new = jnp.sum(q_bd * (knew_ref[...] * (1.0 / MOBA_BLOCK)), axis=-1, keepdims=True)
    lane = lax.broadcasted_iota(jnp.int32, s.shape, 1)
    own = n_past
    s = jnp.where(lane == n_past, s_new, s)
    s = jnp.where(lane < own, s, NEG_INF)
    removed = -jnp.inf
    s = jnp.where(lane <= n_past, s, removed)
    lane_f = lane.astype(F32)
    out = jnp.zeros(s.shape, jnp.int32)
    for r in range(MOBA_TOPK):
        mx = jnp.max(s, axis=-1, keepdims=True)
        idx = jnp.min(jnp.where(s == mx, lane_f, float(LANES)), axis=-1, keepdims=True)
        out = jnp.where(lane == r, idx.astype(jnp.int32), out)
        s = jnp.where(lane_f == idx, removed, s)
    sel_ref[...] = out


def _sample_select(bsum, q, k_new, n_past):
    nb = bsum.shape[0]
    assert n_past < LANES
    per_b = lambda *s: pl.BlockSpec((None,) + s, lambda b: (b,) + (0,) * len(s))
    return pl.pallas_call(
        functools.partial(_sample_select_kernel, n_past=n_past),
        out_shape=jax.ShapeDtypeStruct((nb, N_ATTN_HEADS, LANES), jnp.int32),
        grid=(nb,),
        in_specs=[per_b(ATTN_DIM, LANES), per_b(1, ATTN_DIM), per_b(1, ATTN_DIM)],
        out_specs=per_b(N_ATTN_HEADS, LANES),
        compiler_params=_params(1),
        name="moba_sample_select",
    )(bsum, q.reshape(nb, 1, ATTN_DIM), k_new.reshape(nb, 1, ATTN_DIM))


N_SEL_PAGES = MOBA_TOPK * PAGES_PER_BLOCK


def _sample_attend_kernel(pg_ref, ok_ref, q_ref, knew_ref, vnew_ref, *refs):
    o_ref = refs[-1]
    k_refs = refs[:HEADS_PER_VREG * N_SEL_PAGES]
    v_refs = refs[HEADS_PER_VREG * N_SEL_PAGES:2 * HEADS_PER_VREG * N_SEL_PAGES]
    b = pl.program_id(0)
    hp = pl.program_id(1)
    outs = []
    for hh in range(HEADS_PER_VREG):
        chans = slice(hh * ATTN_HEAD_DIM, (hh + 1) * ATTN_HEAD_DIM)
        qh = q_ref[:, chans] * ATTN_SCALE
        q8 = jnp.broadcast_to(qh, (SUBLANES, ATTN_HEAD_DIM)).astype(BF16)
        lgs = []
        for r in range(N_SEL_PAGES):
            ok = ok_ref[(b * N_ATTN_HEADS + hp * HEADS_PER_VREG + hh) * MOBA_TOPK + r // PAGES_PER_BLOCK]
            lg = _dot(q8, k_refs[hh * N_SEL_PAGES + r][...].astype(BF16))[0:1, :]
            lgs.append(jnp.where(ok > 0, lg, NEG_INF))
        lg_new = jnp.sum(qh * knew_ref[:, chans], axis=-1, keepdims=True)
        m = lg_new
        for lg in lgs:
            m = jnp.maximum(m, jnp.max(lg, axis=-1, keepdims=True))
        p_new = jnp.exp(lg_new - m)
        l = p_new
        acc = p_new * vnew_ref[:, chans]
        for r, lg in enumerate(lgs):
            p = jnp.exp(lg - m)
            l = l + jnp.sum(p, axis=-1, keepdims=True)
            p8 = jnp.broadcast_to(p, (SUBLANES, PAGE_SIZE)).astype(BF16)
            acc = acc + _dot_nt(p8, v_refs[hh * N_SEL_PAGES + r][...].astype(BF16))[0:1, :]
        outs.append(acc / l)
    o_ref[...] = jnp.concatenate(outs, axis=-1).astype(o_ref.dtype)


def _sample_attend(q, k_new, v_new, cache_k, cache_v, pages, ok):
    nb = q.shape[0]
    n_pairs = ATTN_DIM // LANES
    row = pl.BlockSpec((None, 1, LANES), lambda b, hp, pg, ok: (b, 0, hp))

    def page_spec(hh, r):
        return pl.BlockSpec((None, ATTN_HEAD_DIM, PAGE_SIZE), lambda b, hp, pg, ok: (
            pg[(b * N_ATTN_HEADS + hp * HEADS_PER_VREG + hh) * N_SEL_PAGES + r] * N_ATTN_HEADS
            + hp * HEADS_PER_VREG + hh, 0, 0))

    page_specs = [page_spec(hh, r) for hh in range(HEADS_PER_VREG) for r in range(N_SEL_PAGES)]
    out = pl.pallas_call(
        _sample_attend_kernel,
        out_shape=jax.ShapeDtypeStruct((nb, 1, ATTN_DIM), BF16),
        grid_spec=pltpu.PrefetchScalarGridSpec(
            num_scalar_prefetch=2, grid=(nb, n_pairs),
            in_specs=[row, row, row] + page_specs + page_specs, out_specs=row),
        compiler_params=_params(2),
        name="moba_sample_attend",
    )(pages, ok, q.reshape(nb, 1, ATTN_DIM), k_new.reshape(nb, 1, ATTN_DIM), v_new.reshape(nb, 1, ATTN_DIM),
      *([cache_k] * len(page_specs)), *([cache_v] * len(page_specs)))
    return out.reshape(nb, ATTN_DIM)


def _moba_sample(q, k_new, v_new, cache_k, cache_v, page_table):
    nb, n_pages = page_table.shape
    ck = jnp.transpose(cache_k, (0, 1, 3, 4, 2))
    cv = jnp.transpose(cache_v, (0, 1, 3, 4, 2))
    n_past = n_pages * PAGE_SIZE // MOBA_BLOCK
    assert n_pages * PAGE_SIZE == n_past * MOBA_BLOCK
    bsum = _page_sums(ck.reshape(-1, ATTN_DIM, PAGE_SIZE), page_table)
    sel = _sample_select(bsum, q, k_new, n_past)[:, :, :MOBA_TOPK]
    ck = ck.reshape(-1, ATTN_HEAD_DIM, PAGE_SIZE)
    cv = cv.reshape(-1, ATTN_HEAD_DIM, PAGE_SIZE)
    ok = (sel < n_past).astype(jnp.int32)
    blk = jnp.minimum(sel, n_past - 1)
    logical = blk[..., None] * PAGES_PER_BLOCK + jnp.arange(PAGES_PER_BLOCK)
    pages = jnp.take_along_axis(page_table, logical.reshape(nb, -1), axis=1)
    return _sample_attend(q, k_new, v_new, ck, cv, pages.reshape(-1), ok.reshape(-1))


def _merge_ln_kernel(x_ref, g_ref, ys_ref, ya_ref, ga_ref, gb_ref, ws_ref, wa_ref, wo_ref, lng_ref, lnb_ref, o_ref):
    merged = (jax.nn.sigmoid(ga_ref[...]) * _dot(ys_ref[...], ws_ref[...])
              + jax.nn.sigmoid(gb_ref[...]) * _dot(ya_ref[...], wa_ref[...]))
    mix = _dot(merged.astype(BF16), wo_ref[...])
    y = DEEPNORM_ALPHA * x_ref[...] + g_ref[...] * mix
    o_ref[...] = _layer_norm(y, lng_ref[...], lnb_ref[...])


def _merge_ln(x, mods, k, y_ssm, y_attn, ga, gb, ws, wa, wo, lng, lnb, tm, rows_per_batch):
    n, d = x.shape
    rows = lambda w: pl.BlockSpec((tm, w), lambda i: (i, 0))
    return pl.pallas_call(
        _merge_ln_kernel,
        out_shape=jax.ShapeDtypeStruct((n, d), F32),
        grid=(n // tm,),
        in_specs=[rows(d)] + _mod_specs(mods, (k,), tm, rows_per_batch)
        + [rows(D_INNER), rows(ATTN_DIM), rows(d), rows(d),
           _resident(ws.shape), _resident(wa.shape), _resident(wo.shape), _resident((1, d)), _resident((1, d))],
        out_specs=rows(d),
        compiler_params=_params(1),
        name="merge_ln",
    )(x, mods, y_ssm, y_attn, ga, gb, ws, wa, wo, lng.reshape(1, d), lnb.reshape(1, d))


PROMPT_TM = 512
IN_PROJ_TM = 256


def _split_w_in(w_in):
    edges = (0, D_INNER, D_INNER + CONV_DIM, D_INNER + CONV_DIM + N_SSM_HEADS)
    edges = edges + tuple(edges[-1] + i * ATTN_DIM for i in range(1, 6))
    parts = [w_in[:, a:b] for a, b in zip(edges[:-1], edges[1:])]
    parts[2] = jnp.pad(parts[2], ((0, 0), (0, LANES - N_SSM_HEADS)))
    return [p.astype(BF16) for p in parts]


def kernel(x_prompt, x_sample, cache_k, cache_v, state_conv, state_ssm, page_table, c_prompt, c_sample, w_ada, b_ada, ln_g, ln_b, w_ffn1_gu, w_ffn1_down, w_ffn2_gu, w_ffn2_down, w_in, conv_w, conv_b, dt_bias, a_log, d_skip, ssm_norm_g, w_branch_ssm, w_branch_attn, w_out):
    assert w_in.shape[0] == DEPTH == 1
    nb, seq, d = x_prompt.shape
    nbs, dec_seq, _ = x_sample.shape
    assert dec_seq == 1
    l = 0
    w1gu, w1d = w_ffn1_gu[l].astype(BF16), w_ffn1_down[l].astype(BF16)
    w2gu, w2d = w_ffn2_gu[l].astype(BF16), w_ffn2_down[l].astype(BF16)
    w_parts = _split_w_in(w_in[l])
    ws, wa, wo = w_branch_ssm[l].astype(BF16), w_branch_attn[l].astype(BF16), w_out[l].astype(BF16)
    ssd_w = (conv_w[l], conv_b[l], dt_bias[l], a_log[l], d_skip[l], ssm_norm_g[l])

    mods = _ada_mods(jnp.concatenate([c_prompt, c_sample], axis=0), w_ada[l], b_ada[l])
    mods_p = mods[:nb].reshape(nb * 9, 1, d)
    mods_s = mods[nb:]

    xp = x_prompt.reshape(nb * seq, d)
    xp = _ffn_ln(xp, mods_p, (0, 1, 2), w1gu, w1d, ln_g[l, 0], ln_b[l, 0], PROMPT_TM, seq)
    kv_t = (False, False, False, False, True, True, False, False)
    w_parts_p = [w.T if t else w for w, t in zip(w_parts, kv_t)]
    z, xc, dt, q, k_t, v_t, ga, gb, conv_p = _in_proj(xp, mods_p, (3, 4), w_parts_p, kv_t, IN_PROJ_TM, seq,
                                                      conv=(1, conv_w[l], conv_b[l]))
    y_ssm, ssm_p = _ssd_prompt(xc, z, dt, nb, seq, *ssd_w[2:])
    y_attn = _moba_prompt(q, k_t, v_t, nb, seq)
    xp = _merge_ln(xp, mods_p, 5, y_ssm, y_attn, ga, gb, ws, wa, wo, ln_g[l, 1], ln_b[l, 1], PROMPT_TM, seq)
    xp = _ffn_ln(xp, mods_p, (6, 7, 8), w2gu, w2d, ln_g[l, 2], ln_b[l, 2], PROMPT_TM, seq)

    xs = x_sample.reshape(nbs, d)
    xs = _ffn_ln(xs, mods_s, (0, 1, 2), w1gu, w1d, ln_g[l, 0], ln_b[l, 0], nbs, 1)
    zs, xbcs, dts, qs, ks, vs, gas, gbs = _in_proj(xs, mods_s, (3, 4), w_parts, (False,) * len(w_parts), nbs, 1)
    y_ssm_s, conv_s, ssm_s = _ssd_sample(xbcs, zs, dts, state_conv[l], state_ssm[l], *ssd_w)
    y_attn_s = _moba_sample(qs, ks, vs, cache_k, cache_v, page_table)
    xs = _merge_ln(xs, mods_s, 5, y_ssm_s.reshape(nbs, D_INNER), y_attn_s, gas, gbs, ws, wa, wo,
                   ln_g[l, 1], ln_b[l, 1], nbs, 1)
    xs = _ffn_ln(xs, mods_s, (6, 7, 8), w2gu, w2d, ln_g[l, 2], ln_b[l, 2], nbs, 1)

    heads = (N_ATTN_HEADS, ATTN_HEAD_DIM)
    state = (N_SSM_HEADS, SSM_HEAD_DIM, SSM_STATE)
    to_rows = lambda a_t: jnp.transpose(a_t.reshape((1, nb) + heads + (seq,)), (0, 1, 4, 2, 3))
    return (xp.reshape(nb, seq, d), xs.reshape(nbs, 1, d), to_rows(k_t), to_rows(v_t),
            conv_p[None], ssm_p.reshape((1, nb) + state),
            ks.reshape((1, nbs, 1) + heads), vs.reshape((1, nbs, 1) + heads),
            conv_s[None], ssm_s.reshape((1, nbs) + state))
```

```python
import functools

import jax
import jax.numpy as jnp
from jax import lax
from jax.experimental import pallas as pl
from jax.experimental.pallas import tpu as pltpu

F32 = jnp.float32
BF16 = jnp.bfloat16

D_MODEL = 1024
D_INNER = 2048
SSM_HEAD_DIM = 64
N_SSM_HEADS = 32
N_SSM_GROUPS = 4
SSM_STATE = 128
CONV_WIDTH = 4
CONV_DIM = D_INNER + 2 * N_SSM_GROUPS * SSM_STATE
SSD_CHUNK = 128
ATTN_HEAD_DIM = 64
N_ATTN_HEADS = 16
ATTN_DIM = 1024
MOBA_BLOCK = 256
MOBA_TOPK = 3
PAGE_SIZE = 128
D_FF = 2816
DEPTH = 1
DEEPNORM_ALPHA = (2.0 * DEPTH) ** 0.25
LN_EPS = 1e-5
RMS_EPS = 1e-5
NEG_INF = -1e30
ATTN_SCALE = ATTN_HEAD_DIM ** -0.5

LANES = 128
SUBLANES = 8
VMEM_LIMIT = 56 * 1024 * 1024


def _dot(a, b):
    return jnp.dot(a, b, preferred_element_type=F32)


def _dot_nt(a, b):
    return lax.dot_general(a, b, (((1,), (1,)), ((), ())), preferred_element_type=F32)


def _split2(a):
    hi = a.astype(BF16)
    lo = (a - hi.astype(F32)).astype(BF16)
    return hi, lo


def _split3(a):
    hi = a.astype(BF16)
    r = a - hi.astype(F32)
    mid = r.astype(BF16)
    lo = (r - mid.astype(F32)).astype(BF16)
    return hi, mid, lo


def _dot_sel(a, sel_bf16):
    hi, mid, lo = _split3(a)
    return _dot(hi, sel_bf16) + _dot(mid, sel_bf16) + _dot(lo, sel_bf16)


def _dot_sel2(a, sel_bf16):
    hi, lo = _split2(a)
    return _dot(hi, sel_bf16) + _dot(lo, sel_bf16)


def _dot_hp(a, b):
    ah, al = _split2(a)
    bh, bl = _split2(b)
    return _dot(ah, bh) + _dot(al, bh) + _dot(ah, bl)


def _dot_nt_hp(a, b):
    ah, al = _split2(a)
    bh, bl = _split2(b)
    return _dot_nt(ah, bh) + _dot_nt(al, bh) + _dot_nt(ah, bl)


def _silu(x):
    return x * jax.nn.sigmoid(x)


def _softplus(x):
    return jnp.maximum(x, 0.0) + jnp.log1p(jnp.exp(-jnp.abs(x)))


def _layer_norm(y, g, b):
    mu = jnp.mean(y, axis=-1, keepdims=True)
    yc = y - mu
    var = jnp.mean(yc * yc, axis=-1, keepdims=True)
    return yc * lax.rsqrt(var + LN_EPS) * g + b


def _resident(shape):
    nd = len(shape)
    return pl.BlockSpec(shape, lambda *_: (0,) * nd, pipeline_mode=pl.Buffered(1))


def _params(n_axes):
    return pltpu.CompilerParams(dimension_semantics=("arbitrary",) * n_axes, vmem_limit_bytes=VMEM_LIMIT)


def _ada_kernel(c_ref, w_ref, b_ref, o_ref):
    s = _silu(c_ref[...])
    o_ref[...] = _dot_hp(s, w_ref[...]) + b_ref[...]


def _ada_mods(c, w_ada, b_ada):
    m, d = c.shape
    n = w_ada.shape[1]
    tn = 1024
    return pl.pallas_call(
        _ada_kernel,
        out_shape=jax.ShapeDtypeStruct((m, n), F32),
        grid=(n // tn,),
        in_specs=[pl.BlockSpec((m, d), lambda j: (0, 0)),
                  pl.BlockSpec((d, tn), lambda j: (0, j)),
                  pl.BlockSpec((1, tn), lambda j: (0, j))],
        out_specs=pl.BlockSpec((m, tn), lambda j: (0, j)),
        compiler_params=_params(1),
        name="ada_mods",
    )(c, w_ada, b_ada.reshape(1, n))


def _mod_specs(mods, ks, tm, rows_per_batch):
    if mods.ndim == 3:
        tiles_per_batch = rows_per_batch // tm
        return [pl.BlockSpec((None, 1, D_MODEL), functools.partial(
            lambda i, k: ((i // tiles_per_batch) * 9 + k, 0, 0), k=k)) for k in ks]
    return [pl.BlockSpec((tm, D_MODEL), functools.partial(lambda i, k: (i, k), k=k)) for k in ks]


FF_CHUNK = 1408


def _ffn_ln_kernel(x_ref, sh_ref, sc_ref, g_ref, wgu_ref, wd_ref, lng_ref, lnb_ref, o_ref):
    x = x_ref[...]
    h = (x * (1.0 + sc_ref[...]) + sh_ref[...]).astype(BF16)
    acc = jnp.zeros(x.shape, F32)
    for j in range(D_FF // FF_CHUNK):
        lo = j * FF_CHUNK
        gate = _dot(h, wgu_ref[:, lo:lo + FF_CHUNK])
        up = _dot(h, wgu_ref[:, D_FF + lo:D_FF + lo + FF_CHUNK])
        act = (_silu(gate) * up).astype(BF16)
        acc = acc + _dot(act, wd_ref[lo:lo + FF_CHUNK, :])
    y = DEEPNORM_ALPHA * x + 0.5 * g_ref[...] * acc
    o_ref[...] = _layer_norm(y, lng_ref[...], lnb_ref[...])


def _ffn_ln(x, mods, ks, wgu, wd, lng, lnb, tm, rows_per_batch):
    n, d = x.shape
    row = pl.BlockSpec((tm, d), lambda i: (i, 0))
    return pl.pallas_call(
        _ffn_ln_kernel,
        out_shape=jax.ShapeDtypeStruct((n, d), F32),
        grid=(n // tm,),
        in_specs=[row] + _mod_specs(mods, ks, tm, rows_per_batch)
        + [_resident(wgu.shape), _resident(wd.shape), _resident((1, d)), _resident((1, d))],
        out_specs=row,
        compiler_params=_params(1),
        name="ffn_ln",
    )(x, mods, mods, mods, wgu, wd, lng.reshape(1, d), lnb.reshape(1, d))


CARRY = SUBLANES
CONV_COLS = 256
PROJ_PIECE = 512


def _in_proj_kernel(x_ref, sh_ref, sc_ref, *refs, transposed, conv_index, tiles_per_batch):
    n_out = len(transposed)
    w_refs, o_refs = refs[:n_out], refs[n_out + 2 * (conv_index is not None):]
    h = (x_ref[...] * (1.0 + sc_ref[...]) + sh_ref[...]).astype(BF16)

    def project(idx, lo=None, hi=None):
        if transposed[idx]:
            return _dot_nt(w_refs[idx][lo:hi, :], h)
        return _dot(h, w_refs[idx][:, lo:hi])

    conv_chunks = []
    if conv_index is not None:
        cw_ref, cb_ref = refs[n_out:n_out + 2]
        state_ref, buf_s = refs[-2:]
        tm = x_ref.shape[0]
        tile = pl.program_id(0) % tiles_per_batch

        @pl.when(tile == 0)
        def _():
            buf_s[0:CARRY, :] = jnp.zeros((CARRY, CONV_DIM), F32)

        buf_s[CARRY:CARRY + tm, :] = project(conv_index)

        def conv_chunk(lo):
            cols = slice(lo, lo + CONV_COLS)
            acc = cb_ref[:, cols]
            for j in range(CONV_WIDTH):
                off = CARRY - (CONV_WIDTH - 1) + j
                acc = acc + cw_ref[j:j + 1, cols] * buf_s[off:off + tm, cols]
            o_refs[conv_index][:, cols] = _silu(acc)

        conv_chunks = list(range(0, CONV_DIM, CONV_COLS))

    for idx in range(n_out):
        if idx == conv_index:
            continue
        width = w_refs[idx].shape[0] if transposed[idx] else w_refs[idx].shape[1]
        for lo in range(0, width, PROJ_PIECE):
            hi = min(lo + PROJ_PIECE, width)
            piece = project(idx, lo, hi).astype(o_refs[idx].dtype)
            if transposed[idx]:
                o_refs[idx][lo:hi, :] = piece
            else:
                o_refs[idx][:, lo:hi] = piece
            if conv_chunks:
                conv_chunk(conv_chunks.pop(0))
    for lo in conv_chunks:
        conv_chunk(lo)

    if conv_index is not None:
        @pl.when(tile == tiles_per_batch - 1)
        def _():
            state_ref[...] = buf_s[CARRY + tm - (CONV_WIDTH - 1):CARRY + tm, :]

        buf_s[0:CARRY, :] = buf_s[tm:tm + CARRY, :]


def _in_proj(x, mods, ks, weights, transposed, tm, rows_per_batch, conv=None):
    n, d = x.shape
    nb = n // rows_per_batch
    tiles_per_batch = max(rows_per_batch // tm, 1)
    row = pl.BlockSpec((tm, d), lambda i: (i, 0))
    out_shape, out_specs = [], []
    for w, t in zip(weights, transposed):
        if t:
            out_shape.append(jax.ShapeDtypeStruct((nb, w.shape[0], rows_per_batch), F32))
            out_specs.append(pl.BlockSpec((None, w.shape[0], tm),
                                          lambda i: (i // tiles_per_batch, 0, i % tiles_per_batch)))
        else:
            out_shape.append(jax.ShapeDtypeStruct((n, w.shape[1]), F32))
            out_specs.append(pl.BlockSpec((tm, w.shape[1]), lambda i: (i, 0)))
    in_specs = [row] + _mod_specs(mods, ks, tm, rows_per_batch) + [_resident(w.shape) for w in weights]
    args = [x, mods, mods, *weights]
    scratch = []
    if conv is not None:
        conv_index, conv_w, conv_b = conv
        in_specs += [_resident((CONV_WIDTH, CONV_DIM)), _resident((1, CONV_DIM))]
        args += [conv_w, conv_b.reshape(1, CONV_DIM)]
        out_shape.append(jax.ShapeDtypeStruct((nb, CONV_WIDTH - 1, CONV_DIM), F32))
        out_specs.append(pl.BlockSpec((None, CONV_WIDTH - 1, CONV_DIM), lambda i: (i // tiles_per_batch, 0, 0)))
        scratch.append(pltpu.VMEM((CARRY + tm, CONV_DIM), F32))
    return pl.pallas_call(
        functools.partial(_in_proj_kernel, transposed=tuple(transposed),
                          conv_index=None if conv is None else conv[0], tiles_per_batch=tiles_per_batch),
        out_shape=out_shape,
        grid=(n // tm,),
        in_specs=in_specs,
        out_specs=out_specs,
        scratch_shapes=scratch,
        compiler_params=_params(1),
        name="in_proj",
    )(*args)


GROUP_W = D_INNER // N_SSM_GROUPS


def _gated_rmsnorm(y, z, norm_g):
    yz = y * _silu(z)
    outs = []
    for g in range(N_SSM_GROUPS):
        blk = yz[:, g * GROUP_W:(g + 1) * GROUP_W]
        ms = jnp.mean(blk * blk, axis=-1, keepdims=True)
        outs.append(blk * lax.rsqrt(ms + RMS_EPS))
    return jnp.concatenate(outs, axis=-1) * norm_g


def _ssd_prompt_kernel(xc_s, z_ref, dt_ref, dtb_ref, alog_ref, dskip_ref, ng_ref, exp_ref, y_ref, ssm_ref, st_s):
    c = pl.program_id(1)
    q = SSD_CHUNK

    @pl.when(c == 0)
    def _():
        st_s[...] = jnp.zeros(st_s.shape, F32)

    expand = exp_ref[...]
    dt = _softplus(dt_ref[...] + dtb_ref[...])
    da = dt * (-jnp.exp(alog_ref[...]))
    row_i = lax.broadcasted_iota(jnp.int32, (q, q), 0)
    col_i = lax.broadcasted_iota(jnp.int32, (q, q), 1)
    causal = row_i >= col_i
    tril = jnp.where(causal, 1.0, 0.0).astype(BF16)
    a_cum = _dot_tril(tril, da)
    a_src = a_cum - jnp.log(dt)
    a_src_t = a_src.T
    a_last = a_cum[q - 1:q, :]
    dec_end_x = _dot_sel2(jnp.exp(a_last - a_src), expand)
    exp_a_x = _dot_sel2(jnp.exp(a_cum), expand)
    chunk_decay_x = _dot_sel(jnp.broadcast_to(jnp.exp(a_last), (SUBLANES, LANES)), expand)[0:1, :]

    xs = xc_s[:, 0:D_INNER]
    xs_b = xs.astype(BF16)
    xdec_b = (xs * dec_end_x).astype(BF16)
    lane = lax.broadcasted_iota(jnp.int32, (1, LANES), 1)
    zero_b = jnp.zeros((), BF16)

    y_parts = []
    for g in range(N_SSM_GROUPS):
        b_g = xc_s[:, D_INNER + g * SSM_STATE:D_INNER + (g + 1) * SSM_STATE]
        c_g = xc_s[:, D_INNER + (N_SSM_GROUPS + g) * SSM_STATE:D_INNER + (N_SSM_GROUPS + g + 1) * SSM_STATE]
        b_gb = b_g.astype(BF16)
        c_gb = c_g.astype(BF16)
        cb = _dot_nt(c_gb, b_gb)
        gcols = slice(g * GROUP_W, (g + 1) * GROUP_W)
        st_old = st_s[:, gcols]
        y_inter = _dot(c_gb, st_old.astype(BF16)) * exp_a_x[:, gcols]
        st_s[:, gcols] = chunk_decay_x[:, gcols] * st_old + _dot(b_g.T.astype(BF16), xdec_b[:, gcols])
        pair_out = []
        heads_per_group = N_SSM_HEADS // N_SSM_GROUPS
        for pr in range(heads_per_group // 2):
            pcols = slice(g * GROUP_W + pr * LANES, g * GROUP_W + (pr + 1) * LANES)
            x_pair = xs_b[:, pcols]
            acc = None
            for hh in range(2):
                h = g * heads_per_group + pr * 2 + hh
                seg = a_cum[:, h:h + 1] - a_src_t[h:h + 1, :]
                w = (cb * jnp.exp(jnp.where(causal, seg, -jnp.inf))).astype(BF16)
                x_h = jnp.where((lane // SSM_HEAD_DIM) == hh, x_pair, zero_b)
                t = _dot(w, x_h)
                acc = t if acc is None else acc + t
            pair_out.append(acc)
        y_parts.append(jnp.concatenate(pair_out, axis=-1) + y_inter)
    y = jnp.concatenate(y_parts, axis=-1) + dskip_ref[...] * xs
    y_ref[...] = _gated_rmsnorm(y, z_ref[...], ng_ref[...]).astype(y_ref.dtype)

    @pl.when(c == pl.num_programs(1) - 1)
    def _():
        for r in range(D_INNER // LANES):
            ssm_ref[r * LANES:(r + 1) * LANES, :] = st_s[:, r * LANES:(r + 1) * LANES].T


def _dot_tril(tril_bf16, a):
    hi, mid, lo = _split3(a)
    return _dot(tril_bf16, hi) + _dot(tril_bf16, mid) + _dot(tril_bf16, lo)


def _pad_lanes(v):
    return jnp.zeros((1, LANES), F32).at[0, :v.shape[0]].set(v)


def _expand_matrix():
    h = jnp.arange(LANES)[:, None]
    ch = jnp.arange(D_INNER)[None, :] // SSM_HEAD_DIM
    return (h == ch).astype(BF16)


def _ssd_prompt(xc, z, dt, nb, seq, dt_bias, a_log, d_skip, norm_g):
    nc = seq // SSD_CHUNK
    rows = lambda w: pl.BlockSpec((SSD_CHUNK, w), lambda b, c: (b * nc + c, 0))
    return pl.pallas_call(
        _ssd_prompt_kernel,
        out_shape=[jax.ShapeDtypeStruct((nb * seq, D_INNER), BF16),
                   jax.ShapeDtypeStruct((nb, D_INNER, SSM_STATE), F32)],
        grid=(nb, nc),
        in_specs=[rows(CONV_DIM), rows(D_INNER), rows(LANES), _resident((1, LANES)),
                  _resident((1, LANES)), _resident((1, D_INNER)), _resident((1, D_INNER)),
                  _resident((LANES, D_INNER))],
        out_specs=[rows(D_INNER), pl.BlockSpec((None, D_INNER, SSM_STATE), lambda b, c: (b, 0, 0))],
        scratch_shapes=[pltpu.VMEM((SSM_STATE, D_INNER), F32)],
        compiler_params=_params(2),
        name="ssd_prompt",
    )(xc, z, dt, _pad_lanes(dt_bias), _pad_lanes(a_log),
      jnp.repeat(d_skip, SSM_HEAD_DIM).reshape(1, D_INNER), norm_g.reshape(1, D_INNER), _expand_matrix())


def _column_block(row_vec, r):
    return jnp.broadcast_to(row_vec[:, r * LANES:(r + 1) * LANES], (LANES, LANES)).T


def _ssd_sample_kernel(xbc_ref, z_ref, dt_ref, buf_ref, h0_ref, cw_ref, cb_ref, dtb_ref, alog_ref, dskip_ref,
                       ng_ref, exp_ref, y_ref, conv_ref, ssm_ref):
    xr = xbc_ref[...]
    acc = cb_ref[...] + cw_ref[CONV_WIDTH - 1:CONV_WIDTH, :] * xr
    for j in range(CONV_WIDTH - 1):
        acc = acc + cw_ref[j:j + 1, :] * buf_ref[j:j + 1, :]
    xc = _silu(acc)
    for j in range(CONV_WIDTH - 2):
        conv_ref[j:j + 1, :] = buf_ref[j + 1:j + 2, :]
    conv_ref[CONV_WIDTH - 2:CONV_WIDTH - 1, :] = xr

    expand = exp_ref[...]
    dt = _softplus(dt_ref[...] + dtb_ref[...])
    dec = jnp.exp(dt * (-jnp.exp(alog_ref[...])))
    both = jnp.concatenate([jnp.broadcast_to(dt, (SUBLANES // 2, LANES)),
                            jnp.broadcast_to(dec, (SUBLANES // 2, LANES))], axis=0)
    both_x = _dot_sel(both, expand)
    dt_x = both_x[0:1, :]
    dec_x = both_x[SUBLANES // 2:SUBLANES // 2 + 1, :]
    xs = xc[:, 0:D_INNER]
    xdt = xs * dt_x

    y_parts = []
    blocks_per_group = GROUP_W // LANES
    for g in range(N_SSM_GROUPS):
        b_g = xc[:, D_INNER + g * SSM_STATE:D_INNER + (g + 1) * SSM_STATE]
        c_g = xc[:, D_INNER + (N_SSM_GROUPS + g) * SSM_STATE:D_INNER + (N_SSM_GROUPS + g + 1) * SSM_STATE]
        c8 = jnp.broadcast_to(c_g, (SUBLANES, SSM_STATE)).astype(BF16)
        for rb in range(blocks_per_group):
            r = g * blocks_per_group + rb
            rows = slice(r * LANES, (r + 1) * LANES)
            h_new = _column_block(dec_x, r) * h0_ref[rows, :] + _column_block(xdt, r) * b_g
            ssm_ref[rows, :] = h_new
            y_parts.append(_dot_nt(c8, h_new.astype(BF16))[0:1, :])
    y = jnp.concatenate(y_parts, axis=-1) + dskip_ref[...] * xs
    y_ref[...] = _gated_rmsnorm(y, z_ref[...], ng_ref[...]).astype(y_ref.dtype)


def _ssd_sample(xbc, z, dt, state_conv, state_ssm, conv_w, conv_b, dt_bias, a_log, d_skip, norm_g):
    nb = xbc.shape[0]
    per_b = lambda *s: pl.BlockSpec((None,) + s, lambda b: (b,) + (0,) * len(s))
    return pl.pallas_call(
        _ssd_sample_kernel,
        out_shape=[jax.ShapeDtypeStruct((nb, 1, D_INNER), BF16),
                   jax.ShapeDtypeStruct((nb, CONV_WIDTH - 1, CONV_DIM), F32),
                   jax.ShapeDtypeStruct((nb, D_INNER, SSM_STATE), F32)],
        grid=(nb,),
        in_specs=[per_b(1, CONV_DIM), per_b(1, D_INNER), per_b(1, LANES),
                  per_b(CONV_WIDTH - 1, CONV_DIM), per_b(D_INNER, SSM_STATE),
                  _resident((CONV_WIDTH, CONV_DIM)), _resident((1, CONV_DIM)), _resident((1, LANES)),
                  _resident((1, LANES)), _resident((1, D_INNER)), _resident((1, D_INNER)),
                  _resident((LANES, D_INNER))],
        out_specs=[per_b(1, D_INNER), per_b(CONV_WIDTH - 1, CONV_DIM), per_b(D_INNER, SSM_STATE)],
        compiler_params=_params(1),
        name="ssd_sample",
    )(xbc.reshape(nb, 1, CONV_DIM), z.reshape(nb, 1, D_INNER), dt.reshape(nb, 1, LANES), state_conv,
      state_ssm.reshape(nb, D_INNER, SSM_STATE), conv_w, conv_b.reshape(1, CONV_DIM), _pad_lanes(dt_bias),
      _pad_lanes(a_log), jnp.repeat(d_skip, SSM_HEAD_DIM).reshape(1, D_INNER), norm_g.reshape(1, D_INNER),
      _expand_matrix())


HEADS_PER_VREG = LANES // ATTN_HEAD_DIM
MOBA_PAIRS_PER_STEP = 2
MOBA_HEADS_PER_STEP = MOBA_PAIRS_PER_STEP * HEADS_PER_VREG
MOBA_STEP_CHANNELS = MOBA_PAIRS_PER_STEP * LANES


def _moba_prompt_kernel(*refs, page_steps):
    page_tasks = []
    if page_steps is not None:
        page_refs, bsum_ref = refs[4:4 + PAGES_PER_STEP], refs[5 + PAGES_PER_STEP]
        q_ref, k_ref, v_ref, o_ref = refs[1:4] + (refs[4 + PAGES_PER_STEP],)
        kb_s, vt_s, means_s, sel_s, qb_s, m_s, l_s, acc_s = refs[6 + PAGES_PER_STEP:]
        step = ((pl.program_id(0) * pl.num_programs(1) + pl.program_id(1)) * pl.num_programs(2)
                + pl.program_id(2))
        page_step = step % page_steps

        @pl.when(page_step == 0)
        def _():
            bsum_ref[...] = jnp.zeros(bsum_ref.shape, F32)

        page_tasks = list(range(0, ATTN_DIM, LANES))
    else:
        q_ref, k_ref, v_ref, o_ref, kb_s, vt_s, means_s, sel_s, qb_s, m_s, l_s, acc_s = refs

    def run_page_tasks(n):
        for lo in page_tasks[:n]:
            _page_sum_rows(page_refs, bsum_ref, page_step, lo)
        del page_tasks[:n]

    tq = MOBA_BLOCK
    nblk = kb_s.shape[0]
    own = pl.program_id(2)
    pair_cols = [slice(pr * LANES, (pr + 1) * LANES) for pr in range(MOBA_PAIRS_PER_STEP)]
    head_rows = [slice(h * ATTN_HEAD_DIM, (h + 1) * ATTN_HEAD_DIM) for h in range(MOBA_HEADS_PER_STEP)]

    @pl.when(own == 0)
    def _():
        for j in range(nblk):
            for pr in range(MOBA_PAIRS_PER_STEP):
                ksum = jnp.zeros((1, LANES), F32)
                for t in range(MOBA_BLOCK // LANES):
                    cols = slice(j * MOBA_BLOCK + t * LANES, j * MOBA_BLOCK + (t + 1) * LANES)
                    kblk = k_ref[pair_cols[pr], cols].T
                    kb_s[j, t * LANES:(t + 1) * LANES, pair_cols[pr]] = kblk.astype(BF16)
                    ksum = ksum + jnp.sum(kblk, axis=0, keepdims=True)
                means_s[j:j + 1, pair_cols[pr]] = ksum * (1.0 / MOBA_BLOCK)
            vt_s[j] = v_ref[:, j * MOBA_BLOCK:(j + 1) * MOBA_BLOCK].astype(BF16)

    lane = lax.broadcasted_iota(jnp.int32, (1, LANES), 1)
    blk_i = lax.broadcasted_iota(jnp.int32, (nblk, tq), 0)
    causal = (lax.broadcasted_iota(jnp.int32, (MOBA_BLOCK, tq), 0)
              <= lax.broadcasted_iota(jnp.int32, (MOBA_BLOCK, tq), 1))
    heads = range(MOBA_HEADS_PER_STEP)
    cols_of = [pair_cols[h // HEADS_PER_VREG] for h in heads]
    qms = [jnp.where((lane // ATTN_HEAD_DIM) == h % HEADS_PER_VREG, q_ref[:, cols_of[h]], 0.0) for h in heads]
    qbs = [(qm * ATTN_SCALE).astype(BF16) for qm in qms]
    for h in heads:
        qb_s[h] = qbs[h]
    raw_own = [_dot_nt(kb_s[own, :, cols_of[h]], qbs[h]) for h in heads]
    run_page_tasks(2)
    scores = [_dot_nt_hp(means_s[:, cols_of[h]], qms[h]) for h in heads]
    run_page_tasks(2)
    for h in heads:
        s_t = jnp.where(blk_i < own, scores[h], NEG_INF)
        for j in range(nblk):
            sj = s_t[j:j + 1, :]
            beats = jnp.where(s_t > sj, 1.0, 0.0) + jnp.where((s_t == sj) & (blk_i < j), 1.0, 0.0)
            rank = jnp.sum(beats, axis=0, keepdims=True)
            sel_s[h * nblk + j] = jnp.where(rank < float(MOBA_TOPK), 1.0, 0.0) * jnp.where(j < own, 1.0, 0.0)
    run_page_tasks(2)
    ps = []
    for h in heads:
        lg = jnp.where(causal, raw_own[h], NEG_INF)
        m = jnp.max(lg, axis=0, keepdims=True)
        p = jnp.exp(lg - m)
        m_s[h] = m
        l_s[h] = jnp.sum(p, axis=0, keepdims=True)
        ps.append(p.astype(BF16))
    run_page_tasks(len(page_tasks))
    for h in heads:
        acc_s[h] = _dot(vt_s[own, head_rows[h], :], ps[h])

    def past_blocks(js):
        raw = [[_dot_nt(kb_s[j, :, cols_of[h]], qb_s[h]) for h in heads] for j in js]
        ps, alphas = [], []
        for h in heads:
            lgs = [jnp.where(sel_s[h * nblk + j] > 0.5, raw[i][h], NEG_INF) for i, j in enumerate(js)]
            m_old = m_s[h]
            m_new = m_old
            for lg in lgs:
                m_new = jnp.maximum(m_new, jnp.max(lg, axis=0, keepdims=True))
            alpha = jnp.exp(m_old - m_new)
            pp = [jnp.exp(lg - m_new) for lg in lgs]
            l = alpha * l_s[h]
            for p in pp:
                l = l + jnp.sum(p, axis=0, keepdims=True)
            m_s[h] = m_new
            l_s[h] = l
            ps.append([p.astype(BF16) for p in pp])
            alphas.append(alpha)
        pvs = [[_dot(vt_s[j, head_rows[h], :], ps[h][i]) for i, j in enumerate(js)] for h in heads]
        for h in heads:
            acc = alphas[h] * acc_s[h]
            for pv in pvs[h]:
                acc = acc + pv
            acc_s[h] = acc

    def pair_body(jj, carry):
        past_blocks([2 * jj, 2 * jj + 1])
        return carry

    lax.fori_loop(0, own // 2, pair_body, 0)

    @pl.when(own % 2 == 1)
    def _():
        past_blocks([own - 1])
    for pr in range(MOBA_PAIRS_PER_STEP):
        heads = range(pr * HEADS_PER_VREG, (pr + 1) * HEADS_PER_VREG)
        o_t = jnp.concatenate([acc_s[h] / l_s[h] for h in heads], axis=0)
        o_ref[:, pair_cols[pr]] = o_t.T.astype(o_ref.dtype)


def _moba_prompt(q, k_t, v_t, nb, seq, paged=None):
    assert seq % MOBA_BLOCK == 0
    nblk = seq // MOBA_BLOCK
    gw = MOBA_STEP_CHANNELS
    n_groups = ATTN_DIM // gw
    q3, k3, v3 = q.reshape(nb, seq, ATTN_DIM), k_t, v_t
    kv_spec = pl.BlockSpec((None, gw, seq), lambda b, g, t, *_: (b, g, 0))
    q_spec = pl.BlockSpec((None, MOBA_BLOCK, gw), lambda b, g, t, *_: (b, t, g))
    scratch = [pltpu.VMEM((nblk, MOBA_BLOCK, gw), BF16),
               pltpu.VMEM((nblk, gw, MOBA_BLOCK), BF16),
               pltpu.VMEM((nblk, gw), F32),
               pltpu.VMEM((MOBA_HEADS_PER_STEP * nblk, 1, MOBA_BLOCK), F32),
               pltpu.VMEM((MOBA_HEADS_PER_STEP, MOBA_BLOCK, LANES), BF16),
               pltpu.VMEM((MOBA_HEADS_PER_STEP, 1, MOBA_BLOCK), F32),
               pltpu.VMEM((MOBA_HEADS_PER_STEP, 1, MOBA_BLOCK), F32),
               pltpu.VMEM((MOBA_HEADS_PER_STEP, ATTN_HEAD_DIM, MOBA_BLOCK), F32)]
    grid = (nb, n_groups, nblk)
    out_attn = jax.ShapeDtypeStruct((nb, seq, ATTN_DIM), BF16)
    if paged is None:
        out = pl.pallas_call(
            functools.partial(_moba_prompt_kernel, page_steps=None),
            out_shape=out_attn, grid=grid, in_specs=[q_spec, kv_spec, kv_spec], out_specs=q_spec,
            scratch_shapes=scratch, compiler_params=_params(3), name="moba_prompt",
        )(q3, k3, v3)
        return out.reshape(nb * seq, ATTN_DIM)
    cache_t, page_table = paged
    nbs, n_pages = page_table.shape
    page_steps = n_pages // PAGES_PER_STEP
    assert nbs * page_steps == nb * n_groups * nblk and n_pages // PAGES_PER_BLOCK <= LANES

    def step_of(b, g, t):
        return (b * n_groups + g) * nblk + t

    page_specs = _page_specs(lambda b, g, t, pt, r: pt[step_of(b, g, t) // page_steps,
                                                      (step_of(b, g, t) % page_steps) * PAGES_PER_STEP + r])
    out, bsum = pl.pallas_call(
        functools.partial(_moba_prompt_kernel, page_steps=page_steps),
        out_shape=[out_attn, jax.ShapeDtypeStruct((nbs, ATTN_DIM, LANES), F32)],
        grid_spec=pltpu.PrefetchScalarGridSpec(
            num_scalar_prefetch=1, grid=grid, in_specs=[q_spec, kv_spec, kv_spec] + page_specs,
            out_specs=[q_spec, pl.BlockSpec((None, ATTN_DIM, LANES),
                                            lambda b, g, t, pt: (step_of(b, g, t) // page_steps, 0, 0))],
            scratch_shapes=scratch),
        compiler_params=_params(3),
        name="moba_prompt_pages",
    )(page_table, q3, k3, v3, *([cache_t] * PAGES_PER_STEP))
    return out.reshape(nb * seq, ATTN_DIM), bsum


PAGES_PER_STEP = 16
PAGES_PER_BLOCK = MOBA_BLOCK // PAGE_SIZE


def _page_sum_rows(page_refs, o_ref, step, lo):
    blocks_per_step = PAGES_PER_STEP // PAGES_PER_BLOCK
    lane = lax.broadcasted_iota(jnp.int32, (1, LANES), 1)
    rows = slice(lo, lo + LANES)
    acc = o_ref[rows, :]
    for t in range(blocks_per_step):
        pages = page_refs[t * PAGES_PER_BLOCK][rows, :]
        for u in range(1, PAGES_PER_BLOCK):
            pages = pages + page_refs[t * PAGES_PER_BLOCK + u][rows, :]
        acc = jnp.where(lane == step * blocks_per_step + t, jnp.sum(pages, axis=-1, keepdims=True), acc)
    o_ref[rows, :] = acc


def _page_sums_kernel(pt_ref, *refs):
    o_ref = refs[-1]
    i = pl.program_id(1)

    @pl.when(i == 0)
    def _():
        o_ref[...] = jnp.zeros(o_ref.shape, F32)

    for lo in range(0, ATTN_DIM, LANES):
        _page_sum_rows(refs[:-1], o_ref, i, lo)


def _page_specs(index_of):
    return [pl.BlockSpec((None, ATTN_DIM, PAGE_SIZE), functools.partial(
        lambda *a, r: (index_of(*a, r), 0, 0), r=r)) for r in range(PAGES_PER_STEP)]


def _page_sums(cache_t, page_table):
    nb, n_pages = page_table.shape
    assert n_pages % PAGES_PER_STEP == 0 and n_pages // PAGES_PER_BLOCK <= LANES
    specs = _page_specs(lambda b, i, pt, r: pt[b, i * PAGES_PER_STEP + r])
    return pl.pallas_call(
        _page_sums_kernel,
        out_shape=jax.ShapeDtypeStruct((nb, ATTN_DIM, LANES), F32),
        grid_spec=pltpu.PrefetchScalarGridSpec(
            num_scalar_prefetch=1, grid=(nb, n_pages // PAGES_PER_STEP), in_specs=specs,
            out_specs=pl.BlockSpec((None, ATTN_DIM, LANES), lambda b, i, pt: (b, 0, 0))),
        compiler_params=_params(2),
        name="page_sums",
    )(page_table, *([cache_t] * PAGES_PER_STEP))


def _sample_select_kernel(bsum_ref, q_ref, knew_ref, sel_ref, *, n_past):
    q = q_ref[...]
    head = lax.broadcasted_iota(jnp.int32, (N_ATTN_HEADS, ATTN_DIM), 0)
    chan_head = lax.broadcasted_iota(jnp.int32, (N_ATTN_HEADS, ATTN_DIM), 1) // ATTN_HEAD_DIM
    q_bd = jnp.where(head == chan_head, q, 0.0)
    s = _dot_hp(q_bd, bsum_ref[...] * (1.0 / MOBA_BLOCK))
    s_new = jnp.sum(q_bd * (knew_ref[...] * (1.0 / MOBA_BLOCK)), axis=-1, keepdims=True)
    lane = lax.broadcasted_iota(jnp.int32, s.shape, 1)
    own = n_past
    s = jnp.where(lane == n_past, s_new, s)
    s = jnp.where(lane < own, s, NEG_INF)
    removed = -jnp.inf
    s = jnp.where(lane <= n_past, s, removed)
    lane_f = lane.astype(F32)
    out = jnp.zeros(s.shape, jnp.int32)
    for r in range(MOBA_TOPK):
        mx = jnp.max(s, axis=-1, keepdims=True)
        idx = jnp.min(jnp.where(s == mx, lane_f, float(LANES)), axis=-1, keepdims=True)
        out = jnp.where(lane == r, idx.astype(jnp.int32), out)
        s = jnp.where(lane_f == idx, removed, s)
    sel_ref[...] = out


def _sample_select(bsum, q, k_new, n_past):
    nb = bsum.shape[0]
    assert n_past < LANES
    per_b = lambda *s: pl.BlockSpec((None,) + s, lambda b: (b,) + (0,) * len(s))
    return pl.pallas_call(
        functools.partial(_sample_select_kernel, n_past=n_past),
        out_shape=jax.ShapeDtypeStruct((nb, N_ATTN_HEADS, LANES), jnp.int32),
        grid=(nb,),
        in_specs=[per_b(ATTN_DIM, LANES), per_b(1, ATTN_DIM), per_b(1, ATTN_DIM)],
        out_specs=per_b(N_ATTN_HEADS, LANES),
        compiler_params=_params(1),
        name="moba_sample_select",
    )(bsum, q.reshape(nb, 1, ATTN_DIM), k_new.reshape(nb, 1, ATTN_DIM))


N_SEL_PAGES = MOBA_TOPK * PAGES_PER_BLOCK


def _sample_attend_kernel(pg_ref, ok_ref, q_ref, knew_ref, vnew_ref, *refs):
    o_ref = refs[-1]
    k_refs = refs[:HEADS_PER_VREG * N_SEL_PAGES]
    v_refs = refs[HEADS_PER_VREG * N_SEL_PAGES:2 * HEADS_PER_VREG * N_SEL_PAGES]
    step = pl.program_id(0) * pl.num_programs(1) + pl.program_id(1)
    heads = range(HEADS_PER_VREG)
    chans = [slice(hh * ATTN_HEAD_DIM, (hh + 1) * ATTN_HEAD_DIM) for hh in heads]
    qhs = [q_ref[:, chans[hh]] * ATTN_SCALE for hh in heads]
    q8s = [jnp.broadcast_to(qh, (SUBLANES, ATTN_HEAD_DIM)).astype(BF16) for qh in qhs]
    raw = [[_dot(q8s[hh], k_refs[hh * N_SEL_PAGES + r][...].astype(BF16))[0:1, :]
            for r in range(N_SEL_PAGES)] for hh in heads]
    ps, ls, p_news = [], [], []
    for hh in heads:
        oks = [ok_ref[(step * HEADS_PER_VREG + hh) * MOBA_TOPK + t] for t in range(MOBA_TOPK)]
        lgs = [jnp.where(oks[r // PAGES_PER_BLOCK] > 0, raw[hh][r], NEG_INF) for r in range(N_SEL_PAGES)]
        lg_new = jnp.sum(qhs[hh] * knew_ref[:, chans[hh]], axis=-1, keepdims=True)
        m = lg_new
        for lg in lgs:
            m = jnp.maximum(m, jnp.max(lg, axis=-1, keepdims=True))
        p_new = jnp.exp(lg_new - m)
        pp = [jnp.exp(lg - m) for lg in lgs]
        l = p_new
        for p in pp:
            l = l + jnp.sum(p, axis=-1, keepdims=True)
        ps.append([jnp.broadcast_to(p, (SUBLANES, PAGE_SIZE)).astype(BF16) for p in pp])
        ls.append(l)
        p_news.append(p_new)
    pvs = [[_dot_nt(ps[hh][r], v_refs[hh * N_SEL_PAGES + r][...].astype(BF16))[0:1, :]
            for r in range(N_SEL_PAGES)] for hh in heads]
    outs = []
    for hh in heads:
        acc = p_news[hh] * vnew_ref[:, chans[hh]]
        for pv in pvs[hh]:
            acc = acc + pv
        outs.append(acc / ls[hh])
    o_ref[...] = jnp.concatenate(outs, axis=-1).astype(o_ref.dtype)


def _sample_attend(q, k_new, v_new, cache_k, cache_v, pages, ok):
    nb = q.shape[0]
    n_pairs = ATTN_DIM // LANES
    row = pl.BlockSpec((None, 1, LANES), lambda b, hp, pg, ok: (b, 0, hp))
    per_step = HEADS_PER_VREG * N_SEL_PAGES

    def page_spec(i):
        return pl.BlockSpec((None, ATTN_HEAD_DIM, PAGE_SIZE),
                            lambda b, hp, pg, ok: (pg[(b * n_pairs + hp) * per_step + i], 0, 0))

    page_specs = [page_spec(i) for i in range(per_step)]
    out = pl.pallas_call(
        _sample_attend_kernel,
        out_shape=jax.ShapeDtypeStruct((nb, 1, ATTN_DIM), BF16),
        grid_spec=pltpu.PrefetchScalarGridSpec(
            num_scalar_prefetch=2, grid=(nb, n_pairs),
            in_specs=[row, row, row] + page_specs + page_specs, out_specs=row),
        compiler_params=_params(2),
        name="moba_sample_attend",
    )(pages, ok, q.reshape(nb, 1, ATTN_DIM), k_new.reshape(nb, 1, ATTN_DIM), v_new.reshape(nb, 1, ATTN_DIM),
      *([cache_k] * len(page_specs)), *([cache_v] * len(page_specs)))
    return out.reshape(nb, ATTN_DIM)


def _channel_major_pool(cache):
    return jnp.transpose(cache, (0, 1, 3, 4, 2)).reshape(-1, N_ATTN_HEADS, ATTN_HEAD_DIM, PAGE_SIZE)


def _moba_sample(q, k_new, v_new, ck, cv, page_table, bsum):
    nb, n_pages = page_table.shape
    n_past = n_pages * PAGE_SIZE // MOBA_BLOCK
    assert n_pages * PAGE_SIZE == n_past * MOBA_BLOCK
    sel = _sample_select(bsum, q, k_new, n_past)[:, :, :MOBA_TOPK]
    ck = ck.reshape(-1, ATTN_HEAD_DIM, PAGE_SIZE)
    cv = cv.reshape(-1, ATTN_HEAD_DIM, PAGE_SIZE)
    ok = (sel < n_past).astype(jnp.int32)
    blk = jnp.minimum(sel, n_past - 1)
    logical = blk[..., None] * PAGES_PER_BLOCK + jnp.arange(PAGES_PER_BLOCK)
    pages = jnp.take_along_axis(page_table, logical.reshape(nb, -1), axis=1)
    slabs = pages.reshape(nb, N_ATTN_HEADS, N_SEL_PAGES) * N_ATTN_HEADS + jnp.arange(N_ATTN_HEADS)[None, :, None]
    return _sample_attend(q, k_new, v_new, ck, cv, slabs.reshape(-1), ok.reshape(-1))


def _merge_ln_kernel(x_ref, g_ref, ys_ref, ya_ref, ga_ref, gb_ref, ws_ref, wa_ref, wo_ref, lng_ref, lnb_ref, o_ref):
    merged = (jax.nn.sigmoid(ga_ref[...]) * _dot(ys_ref[...], ws_ref[...])
              + jax.nn.sigmoid(gb_ref[...]) * _dot(ya_ref[...], wa_ref[...]))
    mix = _dot(merged.astype(BF16), wo_ref[...])
    y = DEEPNORM_ALPHA * x_ref[...] + g_ref[...] * mix
    o_ref[...] = _layer_norm(y, lng_ref[...], lnb_ref[...])


def _merge_ln(x, mods, k, y_ssm, y_attn, ga, gb, ws, wa, wo, lng, lnb, tm, rows_per_batch):
    n, d = x.shape
    rows = lambda w: pl.BlockSpec((tm, w), lambda i: (i, 0))
    return pl.pallas_call(
        _merge_ln_kernel,
        out_shape=jax.ShapeDtypeStruct((n, d), F32),
        grid=(n // tm,),
        in_specs=[rows(d)] + _mod_specs(mods, (k,), tm, rows_per_batch)
        + [rows(D_INNER), rows(ATTN_DIM), rows(d), rows(d),
           _resident(ws.shape), _resident(wa.shape), _resident(wo.shape), _resident((1, d)), _resident((1, d))],
        out_specs=rows(d),
        compiler_params=_params(1),
        name="merge_ln",
    )(x, mods, y_ssm, y_attn, ga, gb, ws, wa, wo, lng.reshape(1, d), lnb.reshape(1, d))


PROMPT_TM = 512
IN_PROJ_TM = 256


def _split_w_in(w_in):
    edges = (0, D_INNER, D_INNER + CONV_DIM, D_INNER + CONV_DIM + N_SSM_HEADS)
    edges = edges + tuple(edges[-1] + i * ATTN_DIM for i in range(1, 6))
    parts = [w_in[:, a:b] for a, b in zip(edges[:-1], edges[1:])]
    parts[2] = jnp.pad(parts[2], ((0, 0), (0, LANES - N_SSM_HEADS)))
    return [p.astype(BF16) for p in parts]


def kernel(x_prompt, x_sample, cache_k, cache_v, state_conv, state_ssm, page_table, c_prompt, c_sample, w_ada, b_ada, ln_g, ln_b, w_ffn1_gu, w_ffn1_down, w_ffn2_gu, w_ffn2_down, w_in, conv_w, conv_b, dt_bias, a_log, d_skip, ssm_norm_g, w_branch_ssm, w_branch_attn, w_out):
    assert w_in.shape[0] == DEPTH == 1
    nb, seq, d = x_prompt.shape
    nbs, dec_seq, _ = x_sample.shape
    assert dec_seq == 1
    l = 0
    w1gu, w1d = w_ffn1_gu[l].astype(BF16), w_ffn1_down[l].astype(BF16)
    w2gu, w2d = w_ffn2_gu[l].astype(BF16), w_ffn2_down[l].astype(BF16)
    w_parts = _split_w_in(w_in[l])
    ws, wa, wo = w_branch_ssm[l].astype(BF16), w_branch_attn[l].astype(BF16), w_out[l].astype(BF16)
    ssd_w = (conv_w[l], conv_b[l], dt_bias[l], a_log[l], d_skip[l], ssm_norm_g[l])

    mods = _ada_mods(jnp.concatenate([c_prompt, c_sample], axis=0), w_ada[l], b_ada[l])
    mods_p = mods[:nb].reshape(nb * 9, 1, d)
    mods_s = mods[nb:]

    xp = x_prompt.reshape(nb * seq, d)
    xp = _ffn_ln(xp, mods_p, (0, 1, 2), w1gu, w1d, ln_g[l, 0], ln_b[l, 0], PROMPT_TM, seq)
    kv_t = (False, False, False, False, True, True, False, False)
    w_parts_p = [w.T if t else w for w, t in zip(w_parts, kv_t)]
    z, xc, dt, q, k_t, v_t, ga, gb, conv_p = _in_proj(xp, mods_p, (3, 4), w_parts_p, kv_t, IN_PROJ_TM, seq,
                                                      conv=(1, conv_w[l], conv_b[l]))
    y_ssm, ssm_p = _ssd_prompt(xc, z, dt, nb, seq, *ssd_w[2:])
    ck, cv = _channel_major_pool(cache_k), _channel_major_pool(cache_v)
    ck_pages = ck.reshape(-1, ATTN_DIM, PAGE_SIZE)
    n_pages = page_table.shape[1]
    moba_steps = nb * (ATTN_DIM // MOBA_STEP_CHANNELS) * (seq // MOBA_BLOCK)
    if n_pages % PAGES_PER_STEP == 0 and nbs * (n_pages // PAGES_PER_STEP) == moba_steps:
        y_attn, bsum = _moba_prompt(q, k_t, v_t, nb, seq, paged=(ck_pages, page_table))
    else:
        y_attn = _moba_prompt(q, k_t, v_t, nb, seq)
        bsum = _page_sums(ck_pages, page_table)
    xp = _merge_ln(xp, mods_p, 5, y_ssm, y_attn, ga, gb, ws, wa, wo, ln_g[l, 1], ln_b[l, 1], PROMPT_TM, seq)
    xp = _ffn_ln(xp, mods_p, (6, 7, 8), w2gu, w2d, ln_g[l, 2], ln_b[l, 2], PROMPT_TM, seq)

    xs = x_sample.reshape(nbs, d)
    xs = _ffn_ln(xs, mods_s, (0, 1, 2), w1gu, w1d, ln_g[l, 0], ln_b[l, 0], nbs, 1)
    zs, xbcs, dts, qs, ks, vs, gas, gbs = _in_proj(xs, mods_s, (3, 4), w_parts, (False,) * len(w_parts), nbs, 1)
    y_ssm_s, conv_s, ssm_s = _ssd_sample(xbcs, zs, dts, state_conv[l], state_ssm[l], *ssd_w)
    y_attn_s = _moba_sample(qs, ks, vs, ck, cv, page_table, bsum)
    xs = _merge_ln(xs, mods_s, 5, y_ssm_s.reshape(nbs, D_INNER), y_attn_s, gas, gbs, ws, wa, wo,
                   ln_g[l, 1], ln_b[l, 1], nbs, 1)
    xs = _ffn_ln(xs, mods_s, (6, 7, 8), w2gu, w2d, ln_g[l, 2], ln_b[l, 2], nbs, 1)

    heads = (N_ATTN_HEADS, ATTN_HEAD_DIM)
    state = (N_SSM_HEADS, SSM_HEAD_DIM, SSM_STATE)
    to_rows = lambda a_t: jnp.transpose(a_t.reshape((1, nb) + heads + (seq,)), (0, 1, 4, 2, 3))
    return (xp.reshape(nb, seq, d), xs.reshape(nbs, 1, d), to_rows(k_t), to_rows(v_t),
            conv_p[None], ssm_p.reshape((1, nb) + state),
            ks.reshape((1, nbs, 1) + heads), vs.reshape((1, nbs, 1) + heads),
            conv_s[None], ssm_s.reshape((1, nbs) + state))
```

```python
import functools

import jax
import jax.numpy as jnp
from jax import lax
from jax.experimental import pallas as pl
from jax.experimental.pallas import tpu as pltpu

F32 = jnp.float32
BF16 = jnp.bfloat16

D_MODEL = 1024
D_INNER = 2048
SSM_HEAD_DIM = 64
N_SSM_HEADS = 32
N_SSM_GROUPS = 4
SSM_STATE = 128
CONV_WIDTH = 4
CONV_DIM = D_INNER + 2 * N_SSM_GROUPS * SSM_STATE
SSD_CHUNK = 128
ATTN_HEAD_DIM = 64
N_ATTN_HEADS = 16
ATTN_DIM = 1024
MOBA_BLOCK = 256
MOBA_TOPK = 3
PAGE_SIZE = 128
D_FF = 2816
DEPTH = 1
DEEPNORM_ALPHA = (2.0 * DEPTH) ** 0.25
LN_EPS = 1e-5
RMS_EPS = 1e-5
NEG_INF = -1e30
ATTN_SCALE = ATTN_HEAD_DIM ** -0.5

LANES = 128
SUBLANES = 8
VMEM_LIMIT = 56 * 1024 * 1024


def _dot(a, b):
    return jnp.dot(a, b, preferred_element_type=F32)


def _dot_nt(a, b):
    return lax.dot_general(a, b, (((1,), (1,)), ((), ())), preferred_element_type=F32)


def _split2(a):
    hi = a.astype(BF16)
    lo = (a - hi.astype(F32)).astype(BF16)
    return hi, lo


def _split3(a):
    hi = a.astype(BF16)
    r = a - hi.astype(F32)
    mid = r.astype(BF16)
    lo = (r - mid.astype(F32)).astype(BF16)
    return hi, mid, lo


def _dot_sel(a, sel_bf16):
    hi, mid, lo = _split3(a)
    return _dot(hi, sel_bf16) + _dot(mid, sel_bf16) + _dot(lo, sel_bf16)


def _dot_sel2(a, sel_bf16):
    hi, lo = _split2(a)
    return _dot(hi, sel_bf16) + _dot(lo, sel_bf16)


def _dot_hp(a, b):
    ah, al = _split2(a)
    bh, bl = _split2(b)
    return _dot(ah, bh) + _dot(al, bh) + _dot(ah, bl)


def _dot_nt_hp(a, b):
    ah, al = _split2(a)
    bh, bl = _split2(b)
    return _dot_nt(ah, bh) + _dot_nt(al, bh) + _dot_nt(ah, bl)


def _silu(x):
    return x * jax.nn.sigmoid(x)


def _softplus(x):
    return jnp.maximum(x, 0.0) + jnp.log1p(jnp.exp(-jnp.abs(x)))


def _layer_norm(y, g, b):
    mu = jnp.mean(y, axis=-1, keepdims=True)
    yc = y - mu
    var = jnp.mean(yc * yc, axis=-1, keepdims=True)
    return yc * lax.rsqrt(var + LN_EPS) * g + b


def _resident(shape):
    nd = len(shape)
    return pl.BlockSpec(shape, lambda *_: (0,) * nd, pipeline_mode=pl.Buffered(1))


def _params(n_axes):
    return pltpu.CompilerParams(dimension_semantics=("arbitrary",) * n_axes, vmem_limit_bytes=VMEM_LIMIT)


def _ada_kernel(c_ref, w_ref, b_ref, o_ref):
    s = _silu(c_ref[...])
    o_ref[...] = _dot_hp(s, w_ref[...]) + b_ref[...]


def _ada_mods(c, w_ada, b_ada):
    m, d = c.shape
    n = w_ada.shape[1]
    tn = 1024
    return pl.pallas_call(
        _ada_kernel,
        out_shape=jax.ShapeDtypeStruct((m, n), F32),
        grid=(n // tn,),
        in_specs=[pl.BlockSpec((m, d), lambda j: (0, 0)),
                  pl.BlockSpec((d, tn), lambda j: (0, j)),
                  pl.BlockSpec((1, tn), lambda j: (0, j))],
        out_specs=pl.BlockSpec((m, tn), lambda j: (0, j)),
        compiler_params=_params(1),
        name="ada_mods",
    )(c, w_ada, b_ada.reshape(1, n))


def _mod_specs(mods, ks, tm, rows_per_batch):
    if mods.ndim == 3:
        tiles_per_batch = rows_per_batch // tm
        return [pl.BlockSpec((None, 1, D_MODEL), functools.partial(
            lambda i, k: ((i // tiles_per_batch) * 9 + k, 0, 0), k=k)) for k in ks]
    return [pl.BlockSpec((tm, D_MODEL), functools.partial(lambda i, k: (i, k), k=k)) for k in ks]


MXU_DIM = 256
FF_CHUNK = 6 * MXU_DIM
FF_EDGES = tuple(range(0, D_FF, FF_CHUNK)) + (D_FF,)


def _ffn_ln_kernel(x_ref, sh_ref, sc_ref, g_ref, wgu_ref, wd_ref, lng_ref, lnb_ref, o_ref):
    x = x_ref[...]
    h = (x * (1.0 + sc_ref[...]) + sh_ref[...]).astype(BF16)
    acc = jnp.zeros(x.shape, F32)
    for lo, hi in zip(FF_EDGES[:-1], FF_EDGES[1:]):
        gate = _dot(h, wgu_ref[:, lo:hi])
        up = _dot(h, wgu_ref[:, D_FF + lo:D_FF + hi])
        act = (_silu(gate) * up).astype(BF16)
        acc = acc + _dot(act, wd_ref[lo:hi, :])
    y = DEEPNORM_ALPHA * x + 0.5 * g_ref[...] * acc
    o_ref[...] = _layer_norm(y, lng_ref[...], lnb_ref[...])


def _ffn_ln(x, mods, ks, wgu, wd, lng, lnb, tm, rows_per_batch):
    n, d = x.shape
    row = pl.BlockSpec((tm, d), lambda i: (i, 0))
    return pl.pallas_call(
        _ffn_ln_kernel,
        out_shape=jax.ShapeDtypeStruct((n, d), F32),
        grid=(n // tm,),
        in_specs=[row] + _mod_specs(mods, ks, tm, rows_per_batch)
        + [_resident(wgu.shape), _resident(wd.shape), _resident((1, d)), _resident((1, d))],
        out_specs=row,
        compiler_params=_params(1),
        name="ffn_ln",
    )(x, mods, mods, mods, wgu, wd, lng.reshape(1, d), lnb.reshape(1, d))


CARRY = SUBLANES
CONV_COLS = 256
CONV_ROWS = 256
PROJ_PIECE = 512
CONV_CHUNKS_PER_PIECE = 1


def _in_proj_kernel(x_ref, sh_ref, sc_ref, *refs, transposed, conv_index, tiles_per_batch):
    n_out = len(transposed)
    w_refs, o_refs = refs[:n_out], refs[n_out + 2 * (conv_index is not None):]
    h = (x_ref[...] * (1.0 + sc_ref[...]) + sh_ref[...]).astype(BF16)

    def project(idx, lo=None, hi=None):
        if transposed[idx]:
            return _dot_nt(w_refs[idx][lo:hi, :], h)
        return _dot(h, w_refs[idx][:, lo:hi])

    conv_chunks = []
    if conv_index is not None:
        cw_ref, cb_ref = refs[n_out:n_out + 2]
        state_ref, buf_s = refs[-2:]
        tm = x_ref.shape[0]
        tile = pl.program_id(0) % tiles_per_batch

        @pl.when(tile == 0)
        def _():
            buf_s[0:CARRY, :] = jnp.zeros((CARRY, CONV_DIM), F32)

        buf_s[CARRY:CARRY + tm, :] = project(conv_index)

        def conv_chunk(chunk):
            lo, r0 = chunk
            cols = slice(lo, lo + CONV_COLS)
            acc = cb_ref[:, cols]
            for j in range(CONV_WIDTH):
                off = CARRY - (CONV_WIDTH - 1) + j + r0
                acc = acc + cw_ref[j:j + 1, cols] * buf_s[off:off + CONV_ROWS, cols]
            o_refs[conv_index][r0:r0 + CONV_ROWS, cols] = _silu(acc)

        conv_chunks = [(lo, r0) for lo in range(0, CONV_DIM, CONV_COLS) for r0 in range(0, tm, CONV_ROWS)]

    for idx in range(n_out):
        if idx == conv_index:
            continue
        width = w_refs[idx].shape[0] if transposed[idx] else w_refs[idx].shape[1]
        for lo in range(0, width, PROJ_PIECE):
            hi = min(lo + PROJ_PIECE, width)
            piece = project(idx, lo, hi).astype(o_refs[idx].dtype)
            if transposed[idx]:
                o_refs[idx][lo:hi, :] = piece
            else:
                o_refs[idx][:, lo:hi] = piece
            for chunk in conv_chunks[:CONV_CHUNKS_PER_PIECE]:
                conv_chunk(chunk)
            del conv_chunks[:CONV_CHUNKS_PER_PIECE]
    for chunk in conv_chunks:
        conv_chunk(chunk)

    if conv_index is not None:
        @pl.when(tile == tiles_per_batch - 1)
        def _():
            state_ref[...] = buf_s[CARRY + tm - (CONV_WIDTH - 1):CARRY + tm, :]

        buf_s[0:CARRY, :] = buf_s[tm:tm + CARRY, :]


def _in_proj(x, mods, ks, weights, transposed, tm, rows_per_batch, conv=None):
    n, d = x.shape
    nb = n // rows_per_batch
    tiles_per_batch = max(rows_per_batch // tm, 1)
    row = pl.BlockSpec((tm, d), lambda i: (i, 0))
    out_shape, out_specs = [], []
    for w, t in zip(weights, transposed):
        if t:
            out_shape.append(jax.ShapeDtypeStruct((nb, w.shape[0], rows_per_batch), F32))
            out_specs.append(pl.BlockSpec((None, w.shape[0], tm),
                                          lambda i: (i // tiles_per_batch, 0, i % tiles_per_batch)))
        else:
            out_shape.append(jax.ShapeDtypeStruct((n, w.shape[1]), F32))
            out_specs.append(pl.BlockSpec((tm, w.shape[1]), lambda i: (i, 0)))
    in_specs = [row] + _mod_specs(mods, ks, tm, rows_per_batch) + [_resident(w.shape) for w in weights]
    args = [x, mods, mods, *weights]
    scratch = []
    if conv is not None:
        conv_index, conv_w, conv_b = conv
        in_specs += [_resident((CONV_WIDTH, CONV_DIM)), _resident((1, CONV_DIM))]
        args += [conv_w, conv_b.reshape(1, CONV_DIM)]
        out_shape.append(jax.ShapeDtypeStruct((nb, CONV_WIDTH - 1, CONV_DIM), F32))
        out_specs.append(pl.BlockSpec((None, CONV_WIDTH - 1, CONV_DIM), lambda i: (i // tiles_per_batch, 0, 0)))
        scratch.append(pltpu.VMEM((CARRY + tm, CONV_DIM), F32))
    return pl.pallas_call(
        functools.partial(_in_proj_kernel, transposed=tuple(transposed),
                          conv_index=None if conv is None else conv[0], tiles_per_batch=tiles_per_batch),
        out_shape=out_shape,
        grid=(n // tm,),
        in_specs=in_specs,
        out_specs=out_specs,
        scratch_shapes=scratch,
        compiler_params=_params(1),
        name="in_proj",
    )(*args)


GROUP_W = D_INNER // N_SSM_GROUPS


def _gated_rmsnorm(y, z, norm_g):
    yz = y * _silu(z)
    outs = []
    for g in range(N_SSM_GROUPS):
        blk = yz[:, g * GROUP_W:(g + 1) * GROUP_W]
        ms = jnp.mean(blk * blk, axis=-1, keepdims=True)
        outs.append(blk * lax.rsqrt(ms + RMS_EPS))
    return jnp.concatenate(outs, axis=-1) * norm_g


def _ssd_prompt_kernel(xc_s, z_ref, dt_ref, dtb_ref, alog_ref, dskip_ref, ng_ref, exp_ref, y_ref, ssm_ref, st_s):
    c = pl.program_id(1)
    q = SSD_CHUNK

    @pl.when(c == 0)
    def _():
        st_s[...] = jnp.zeros(st_s.shape, F32)

    expand = exp_ref[...]
    dt = _softplus(dt_ref[...] + dtb_ref[...])
    da = dt * (-jnp.exp(alog_ref[...]))
    row_i = lax.broadcasted_iota(jnp.int32, (q, q), 0)
    col_i = lax.broadcasted_iota(jnp.int32, (q, q), 1)
    causal = row_i >= col_i
    tril = jnp.where(causal, 1.0, 0.0).astype(BF16)
    a_cum = _dot_tril(tril, da)
    a_src = a_cum - jnp.log(dt)
    a_src_t = a_src.T
    a_last = a_cum[q - 1:q, :]
    dec_end_x = _dot_sel2(jnp.exp(a_last - a_src), expand)
    exp_a_x = _dot_sel2(jnp.exp(a_cum), expand)
    chunk_decay_x = _dot_sel(jnp.broadcast_to(jnp.exp(a_last), (SUBLANES, LANES)), expand)[0:1, :]

    xs = xc_s[:, 0:D_INNER]
    xs_b = xs.astype(BF16)
    xdec_b = (xs * dec_end_x).astype(BF16)
    lane = lax.broadcasted_iota(jnp.int32, (1, LANES), 1)
    zero_b = jnp.zeros((), BF16)

    y_parts = []
    for g in range(N_SSM_GROUPS):
        b_g = xc_s[:, D_INNER + g * SSM_STATE:D_INNER + (g + 1) * SSM_STATE]
        c_g = xc_s[:, D_INNER + (N_SSM_GROUPS + g) * SSM_STATE:D_INNER + (N_SSM_GROUPS + g + 1) * SSM_STATE]
        b_gb = b_g.astype(BF16)
        c_gb = c_g.astype(BF16)
        cb = _dot_nt(c_gb, b_gb)
        gcols = slice(g * GROUP_W, (g + 1) * GROUP_W)
        st_old = st_s[:, gcols]
        y_inter = _dot(c_gb, st_old.astype(BF16)) * exp_a_x[:, gcols]
        st_s[:, gcols] = chunk_decay_x[:, gcols] * st_old + _dot(b_g.T.astype(BF16), xdec_b[:, gcols])
        pair_out = []
        heads_per_group = N_SSM_HEADS // N_SSM_GROUPS
        for pr in range(heads_per_group // 2):
            pcols = slice(g * GROUP_W + pr * LANES, g * GROUP_W + (pr + 1) * LANES)
            x_pair = xs_b[:, pcols]
            acc = None
            for hh in range(2):
                h = g * heads_per_group + pr * 2 + hh
                seg = a_cum[:, h:h + 1] - a_src_t[h:h + 1, :]
                w = (cb * jnp.exp(jnp.where(causal, seg, -jnp.inf))).astype(BF16)
                x_h = jnp.where((lane // SSM_HEAD_DIM) == hh, x_pair, zero_b)
                t = _dot(w, x_h)
                acc = t if acc is None else acc + t
            pair_out.append(acc)
        y_parts.append(jnp.concatenate(pair_out, axis=-1) + y_inter)
    y = jnp.concatenate(y_parts, axis=-1) + dskip_ref[...] * xs
    y_ref[...] = _gated_rmsnorm(y, z_ref[...], ng_ref[...]).astype(y_ref.dtype)

    @pl.when(c == pl.num_programs(1) - 1)
    def _():
        for r in range(D_INNER // LANES):
            ssm_ref[r * LANES:(r + 1) * LANES, :] = st_s[:, r * LANES:(r + 1) * LANES].T


def _dot_tril(tril_bf16, a):
    hi, mid, lo = _split3(a)
    return _dot(tril_bf16, hi) + _dot(tril_bf16, mid) + _dot(tril_bf16, lo)


def _pad_lanes(v):
    return jnp.zeros((1, LANES), F32).at[0, :v.shape[0]].set(v)


def _expand_matrix():
    h = jnp.arange(LANES)[:, None]
    ch = jnp.arange(D_INNER)[None, :] // SSM_HEAD_DIM
    return (h == ch).astype(BF16)


def _ssd_prompt(xc, z, dt, nb, seq, dt_bias, a_log, d_skip, norm_g):
    nc = seq // SSD_CHUNK
    rows = lambda w: pl.BlockSpec((SSD_CHUNK, w), lambda b, c: (b * nc + c, 0))
    return pl.pallas_call(
        _ssd_prompt_kernel,
        out_shape=[jax.ShapeDtypeStruct((nb * seq, D_INNER), BF16),
                   jax.ShapeDtypeStruct((nb, D_INNER, SSM_STATE), F32)],
        grid=(nb, nc),
        in_specs=[rows(CONV_DIM), rows(D_INNER), rows(LANES), _resident((1, LANES)),
                  _resident((1, LANES)), _resident((1, D_INNER)), _resident((1, D_INNER)),
                  _resident((LANES, D_INNER))],
        out_specs=[rows(D_INNER), pl.BlockSpec((None, D_INNER, SSM_STATE), lambda b, c: (b, 0, 0))],
        scratch_shapes=[pltpu.VMEM((SSM_STATE, D_INNER), F32)],
        compiler_params=_params(2),
        name="ssd_prompt",
    )(xc, z, dt, _pad_lanes(dt_bias), _pad_lanes(a_log),
      jnp.repeat(d_skip, SSM_HEAD_DIM).reshape(1, D_INNER), norm_g.reshape(1, D_INNER), _expand_matrix())


def _column_block(row_vec, r):
    return jnp.broadcast_to(row_vec[:, r * LANES:(r + 1) * LANES], (LANES, LANES)).T


def _ssd_sample_kernel(xbc_ref, z_ref, dt_ref, buf_ref, h0_ref, cw_ref, cb_ref, dtb_ref, alog_ref, dskip_ref,
                       ng_ref, exp_ref, y_ref, conv_ref, ssm_ref):
    xr = xbc_ref[...]
    acc = cb_ref[...] + cw_ref[CONV_WIDTH - 1:CONV_WIDTH, :] * xr
    for j in range(CONV_WIDTH - 1):
        acc = acc + cw_ref[j:j + 1, :] * buf_ref[j:j + 1, :]
    xc = _silu(acc)
    for j in range(CONV_WIDTH - 2):
        conv_ref[j:j + 1, :] = buf_ref[j + 1:j + 2, :]
    conv_ref[CONV_WIDTH - 2:CONV_WIDTH - 1, :] = xr

    expand = exp_ref[...]
    dt = _softplus(dt_ref[...] + dtb_ref[...])
    dec = jnp.exp(dt * (-jnp.exp(alog_ref[...])))
    both = jnp.concatenate([jnp.broadcast_to(dt, (SUBLANES // 2, LANES)),
                            jnp.broadcast_to(dec, (SUBLANES // 2, LANES))], axis=0)
    both_x = _dot_sel(both, expand)
    dt_x = both_x[0:1, :]
    dec_x = both_x[SUBLANES // 2:SUBLANES // 2 + 1, :]
    xs = xc[:, 0:D_INNER]
    xdt = xs * dt_x

    y_parts = []
    blocks_per_group = GROUP_W // LANES
    for g in range(N_SSM_GROUPS):
        b_g = xc[:, D_INNER + g * SSM_STATE:D_INNER + (g + 1) * SSM_STATE]
        c_g = xc[:, D_INNER + (N_SSM_GROUPS + g) * SSM_STATE:D_INNER + (N_SSM_GROUPS + g + 1) * SSM_STATE]
        c8 = jnp.broadcast_to(c_g, (SUBLANES, SSM_STATE)).astype(BF16)
        for rb in range(blocks_per_group):
            r = g * blocks_per_group + rb
            rows = slice(r * LANES, (r + 1) * LANES)
            h_new = _column_block(dec_x, r) * h0_ref[rows, :] + _column_block(xdt, r) * b_g
            ssm_ref[rows, :] = h_new
            y_parts.append(_dot_nt(c8, h_new.astype(BF16))[0:1, :])
    y = jnp.concatenate(y_parts, axis=-1) + dskip_ref[...] * xs
    y_ref[...] = _gated_rmsnorm(y, z_ref[...], ng_ref[...]).astype(y_ref.dtype)


def _ssd_sample(xbc, z, dt, state_conv, state_ssm, conv_w, conv_b, dt_bias, a_log, d_skip, norm_g):
    nb = xbc.shape[0]
    per_b = lambda *s: pl.BlockSpec((None,) + s, lambda b: (b,) + (0,) * len(s))
    return pl.pallas_call(
        _ssd_sample_kernel,
        out_shape=[jax.ShapeDtypeStruct((nb, 1, D_INNER), BF16),
                   jax.ShapeDtypeStruct((nb, CONV_WIDTH - 1, CONV_DIM), F32),
                   jax.ShapeDtypeStruct((nb, D_INNER, SSM_STATE), F32)],
        grid=(nb,),
        in_specs=[per_b(1, CONV_DIM), per_b(1, D_INNER), per_b(1, LANES),
                  per_b(CONV_WIDTH - 1, CONV_DIM), per_b(D_INNER, SSM_STATE),
                  _resident((CONV_WIDTH, CONV_DIM)), _resident((1, CONV_DIM)), _resident((1, LANES)),
                  _resident((1, LANES)), _resident((1, D_INNER)), _resident((1, D_INNER)),
                  _resident((LANES, D_INNER))],
        out_specs=[per_b(1, D_INNER), per_b(CONV_WIDTH - 1, CONV_DIM), per_b(D_INNER, SSM_STATE)],
        compiler_params=_params(1),
        name="ssd_sample",
    )(xbc.reshape(nb, 1, CONV_DIM), z.reshape(nb, 1, D_INNER), dt.reshape(nb, 1, LANES), state_conv,
      state_ssm.reshape(nb, D_INNER, SSM_STATE), conv_w, conv_b.reshape(1, CONV_DIM), _pad_lanes(dt_bias),
      _pad_lanes(a_log), jnp.repeat(d_skip, SSM_HEAD_DIM).reshape(1, D_INNER), norm_g.reshape(1, D_INNER),
      _expand_matrix())


HEADS_PER_VREG = LANES // ATTN_HEAD_DIM
MOBA_PAIRS_PER_STEP = 2
MOBA_HEADS_PER_STEP = MOBA_PAIRS_PER_STEP * HEADS_PER_VREG
MOBA_STEP_CHANNELS = MOBA_PAIRS_PER_STEP * LANES


def _moba_prompt_kernel(*refs, page_steps):
    page_tasks = []
    if page_steps is not None:
        page_refs, bsum_ref = refs[4:4 + PAGES_PER_STEP], refs[5 + PAGES_PER_STEP]
        q_ref, k_ref, v_ref, o_ref = refs[1:4] + (refs[4 + PAGES_PER_STEP],)
        kb_s, vt_s, means_s, sel_s, qb_s, m_s, l_s, acc_s = refs[6 + PAGES_PER_STEP:]
        step = ((pl.program_id(0) * pl.num_programs(1) + pl.program_id(1)) * pl.num_programs(2)
                + pl.program_id(2))
        page_step = step % page_steps

        @pl.when(page_step == 0)
        def _():
            bsum_ref[...] = jnp.zeros(bsum_ref.shape, F32)

        page_tasks = list(range(0, ATTN_DIM, LANES))
    else:
        q_ref, k_ref, v_ref, o_ref, kb_s, vt_s, means_s, sel_s, qb_s, m_s, l_s, acc_s = refs

    def run_page_tasks(n):
        for lo in page_tasks[:n]:
            _page_sum_rows(page_refs, bsum_ref, page_step, lo)
        del page_tasks[:n]

    tq = MOBA_BLOCK
    nblk = kb_s.shape[0]
    own = pl.program_id(2)
    pair_cols = [slice(pr * LANES, (pr + 1) * LANES) for pr in range(MOBA_PAIRS_PER_STEP)]
    head_rows = [slice(h * ATTN_HEAD_DIM, (h + 1) * ATTN_HEAD_DIM) for h in range(MOBA_HEADS_PER_STEP)]

    @pl.when(own == 0)
    def _():
        for j in range(nblk):
            for pr in range(MOBA_PAIRS_PER_STEP):
                ksum = jnp.zeros((1, LANES), F32)
                for t in range(MOBA_BLOCK // LANES):
                    cols = slice(j * MOBA_BLOCK + t * LANES, j * MOBA_BLOCK + (t + 1) * LANES)
                    kblk = k_ref[pair_cols[pr], cols].T
                    kb_s[j, t * LANES:(t + 1) * LANES, pair_cols[pr]] = kblk.astype(BF16)
                    ksum = ksum + jnp.sum(kblk, axis=0, keepdims=True)
                means_s[j:j + 1, pair_cols[pr]] = ksum * (1.0 / MOBA_BLOCK)
            vt_s[j] = v_ref[:, j * MOBA_BLOCK:(j + 1) * MOBA_BLOCK].astype(BF16)

    lane = lax.broadcasted_iota(jnp.int32, (1, LANES), 1)
    blk_i = lax.broadcasted_iota(jnp.int32, (nblk, tq), 0)
    causal = (lax.broadcasted_iota(jnp.int32, (MOBA_BLOCK, tq), 0)
              <= lax.broadcasted_iota(jnp.int32, (MOBA_BLOCK, tq), 1))
    heads = range(MOBA_HEADS_PER_STEP)
    cols_of = [pair_cols[h // HEADS_PER_VREG] for h in heads]
    qms = [jnp.where((lane // ATTN_HEAD_DIM) == h % HEADS_PER_VREG, q_ref[:, cols_of[h]], 0.0) for h in heads]
    qbs = [(qm * ATTN_SCALE).astype(BF16) for qm in qms]
    for h in heads:
        qb_s[h] = qbs[h]
    raw_own = [_dot_nt(kb_s[own, :, cols_of[h]], qbs[h]) for h in heads]
    run_page_tasks(2)
    scores = [_dot_nt_hp(means_s[:, cols_of[h]], qms[h]) for h in heads]
    run_page_tasks(2)
    for h in heads:
        s_t = jnp.where(blk_i < own, scores[h], NEG_INF)
        for j in range(nblk):
            sj = s_t[j:j + 1, :]
            beats = jnp.where(s_t > sj, 1.0, 0.0) + jnp.where((s_t == sj) & (blk_i < j), 1.0, 0.0)
            rank = jnp.sum(beats, axis=0, keepdims=True)
            sel_s[h * nblk + j] = jnp.where(rank < float(MOBA_TOPK), 1.0, 0.0) * jnp.where(j < own, 1.0, 0.0)
    run_page_tasks(2)
    ps = []
    for h in heads:
        lg = jnp.where(causal, raw_own[h], NEG_INF)
        m = jnp.max(lg, axis=0, keepdims=True)
        p = jnp.exp(lg - m)
        m_s[h] = m
        l_s[h] = jnp.sum(p, axis=0, keepdims=True)
        ps.append(p.astype(BF16))
    run_page_tasks(len(page_tasks))
    for h in heads:
        acc_s[h] = _dot(vt_s[own, head_rows[h], :], ps[h])

    def past_blocks(js):
        raw = [[_dot_nt(kb_s[j, :, cols_of[h]], qb_s[h]) for h in heads] for j in js]
        ps, alphas = [], []
        for h in heads:
            lgs = [jnp.where(sel_s[h * nblk + j] > 0.5, raw[i][h], NEG_INF) for i, j in enumerate(js)]
            m_old = m_s[h]
            m_new = m_old
            for lg in lgs:
                m_new = jnp.maximum(m_new, jnp.max(lg, axis=0, keepdims=True))
            alpha = jnp.exp(m_old - m_new)
            pp = [jnp.exp(lg - m_new) for lg in lgs]
            l = alpha * l_s[h]
            for p in pp:
                l = l + jnp.sum(p, axis=0, keepdims=True)
            m_s[h] = m_new
            l_s[h] = l
            ps.append([p.astype(BF16) for p in pp])
            alphas.append(alpha)
        pvs = [[_dot(vt_s[j, head_rows[h], :], ps[h][i]) for i, j in enumerate(js)] for h in heads]
        for h in heads:
            acc = alphas[h] * acc_s[h]
            for pv in pvs[h]:
                acc = acc + pv
            acc_s[h] = acc

    def pair_body(jj, carry):
        past_blocks([2 * jj, 2 * jj + 1])
        return carry

    lax.fori_loop(0, own // 2, pair_body, 0)

    @pl.when(own % 2 == 1)
    def _():
        past_blocks([own - 1])
    for pr in range(MOBA_PAIRS_PER_STEP):
        heads = range(pr * HEADS_PER_VREG, (pr + 1) * HEADS_PER_VREG)
        o_t = jnp.concatenate([acc_s[h] / l_s[h] for h in heads], axis=0)
        o_ref[:, pair_cols[pr]] = o_t.T.astype(o_ref.dtype)


def _moba_prompt(q, k_t, v_t, nb, seq, paged=None):
    assert seq % MOBA_BLOCK == 0
    nblk = seq // MOBA_BLOCK
    gw = MOBA_STEP_CHANNELS
    n_groups = ATTN_DIM // gw
    q3, k3, v3 = q.reshape(nb, seq, ATTN_DIM), k_t, v_t
    kv_spec = pl.BlockSpec((None, gw, seq), lambda b, g, t, *_: (b, g, 0))
    q_spec = pl.BlockSpec((None, MOBA_BLOCK, gw), lambda b, g, t, *_: (b, t, g))
    scratch = [pltpu.VMEM((nblk, MOBA_BLOCK, gw), BF16),
               pltpu.VMEM((nblk, gw, MOBA_BLOCK), BF16),
               pltpu.VMEM((nblk, gw), F32),
               pltpu.VMEM((MOBA_HEADS_PER_STEP * nblk, 1, MOBA_BLOCK), F32),
               pltpu.VMEM((MOBA_HEADS_PER_STEP, MOBA_BLOCK, LANES), BF16),
               pltpu.VMEM((MOBA_HEADS_PER_STEP, 1, MOBA_BLOCK), F32),
               pltpu.VMEM((MOBA_HEADS_PER_STEP, 1, MOBA_BLOCK), F32),
               pltpu.VMEM((MOBA_HEADS_PER_STEP, ATTN_HEAD_DIM, MOBA_BLOCK), F32)]
    grid = (nb, n_groups, nblk)
    out_attn = jax.ShapeDtypeStruct((nb, seq, ATTN_DIM), BF16)
    if paged is None:
        out = pl.pallas_call(
            functools.partial(_moba_prompt_kernel, page_steps=None),
            out_shape=out_attn, grid=grid, in_specs=[q_spec, kv_spec, kv_spec], out_specs=q_spec,
            scratch_shapes=scratch, compiler_params=_params(3), name="moba_prompt",
        )(q3, k3, v3)
        return out.reshape(nb * seq, ATTN_DIM)
    cache_t, page_table = paged
    nbs, n_pages = page_table.shape
    page_steps = n_pages // PAGES_PER_STEP
    assert nbs * page_steps == nb * n_groups * nblk and n_pages // PAGES_PER_BLOCK <= LANES

    def step_of(b, g, t):
        return (b * n_groups + g) * nblk + t

    page_specs = _page_specs(lambda b, g, t, pt, r: pt[step_of(b, g, t) // page_steps,
                                                      (step_of(b, g, t) % page_steps) * PAGES_PER_STEP + r])
    out, bsum = pl.pallas_call(
        functools.partial(_moba_prompt_kernel, page_steps=page_steps),
        out_shape=[out_attn, jax.ShapeDtypeStruct((nbs, ATTN_DIM, LANES), F32)],
        grid_spec=pltpu.PrefetchScalarGridSpec(
            num_scalar_prefetch=1, grid=grid, in_specs=[q_spec, kv_spec, kv_spec] + page_specs,
            out_specs=[q_spec, pl.BlockSpec((None, ATTN_DIM, LANES),
                                            lambda b, g, t, pt: (step_of(b, g, t) // page_steps, 0, 0))],
            scratch_shapes=scratch),
        compiler_params=_params(3),
        name="moba_prompt_pages",
    )(page_table, q3, k3, v3, *([cache_t] * PAGES_PER_STEP))
    return out.reshape(nb * seq, ATTN_DIM), bsum


PAGES_PER_STEP = 16
PAGES_PER_BLOCK = MOBA_BLOCK // PAGE_SIZE


def _page_sum_rows(page_refs, o_ref, step, lo):
    blocks_per_step = PAGES_PER_STEP // PAGES_PER_BLOCK
    lane = lax.broadcasted_iota(jnp.int32, (1, LANES), 1)
    rows = slice(lo, lo + LANES)
    acc = o_ref[rows, :]
    for t in range(blocks_per_step):
        pages = page_refs[t * PAGES_PER_BLOCK][rows, :]
        for u in range(1, PAGES_PER_BLOCK):
            pages = pages + page_refs[t * PAGES_PER_BLOCK + u][rows, :]
        acc = jnp.where(lane == step * blocks_per_step + t, jnp.sum(pages, axis=-1, keepdims=True), acc)
    o_ref[rows, :] = acc


def _page_sums_kernel(pt_ref, *refs):
    o_ref = refs[-1]
    i = pl.program_id(1)

    @pl.when(i == 0)
    def _():
        o_ref[...] = jnp.zeros(o_ref.shape, F32)

    for lo in range(0, ATTN_DIM, LANES):
        _page_sum_rows(refs[:-1], o_ref, i, lo)


def _page_specs(index_of):
    return [pl.BlockSpec((None, ATTN_DIM, PAGE_SIZE), functools.partial(
        lambda *a, r: (index_of(*a, r), 0, 0), r=r)) for r in range(PAGES_PER_STEP)]


def _page_sums(cache_t, page_table):
    nb, n_pages = page_table.shape
    assert n_pages % PAGES_PER_STEP == 0 and n_pages // PAGES_PER_BLOCK <= LANES
    specs = _page_specs(lambda b, i, pt, r: pt[b, i * PAGES_PER_STEP + r])
    return pl.pallas_call(
        _page_sums_kernel,
        out_shape=jax.ShapeDtypeStruct((nb, ATTN_DIM, LANES), F32),
        grid_spec=pltpu.PrefetchScalarGridSpec(
            num_scalar_prefetch=1, grid=(nb, n_pages // PAGES_PER_STEP), in_specs=specs,
            out_specs=pl.BlockSpec((None, ATTN_DIM, LANES), lambda b, i, pt: (b, 0, 0))),
        compiler_params=_params(2),
        name="page_sums",
    )(page_table, *([cache_t] * PAGES_PER_STEP))


def _sample_select_kernel(bsum_ref, q_ref, knew_ref, sel_ref, *, n_past):
    seqs = range(bsum_ref.shape[0])
    head = lax.broadcasted_iota(jnp.int32, (N_ATTN_HEADS, ATTN_DIM), 0)
    chan_head = lax.broadcasted_iota(jnp.int32, (N_ATTN_HEADS, ATTN_DIM), 1) // ATTN_HEAD_DIM
    q_bds = [jnp.where(head == chan_head, q_ref[i], 0.0) for i in seqs]
    scores = [_dot_hp(q_bds[i], bsum_ref[i] * (1.0 / MOBA_BLOCK)) for i in seqs]
    lane = lax.broadcasted_iota(jnp.int32, (N_ATTN_HEADS, LANES), 1)
    lane_f = lane.astype(F32)
    own = n_past
    removed = -jnp.inf
    for i in seqs:
        s_new = jnp.sum(q_bds[i] * (knew_ref[i] * (1.0 / MOBA_BLOCK)), axis=-1, keepdims=True)
        s = jnp.where(lane == n_past, s_new, scores[i])
        s = jnp.where(lane < own, s, NEG_INF)
        s = jnp.where(lane <= n_past, s, removed)
        out = jnp.zeros(s.shape, jnp.int32)
        for r in range(MOBA_TOPK):
            mx = jnp.max(s, axis=-1, keepdims=True)
            idx = jnp.min(jnp.where(s == mx, lane_f, float(LANES)), axis=-1, keepdims=True)
            out = jnp.where(lane == r, idx.astype(jnp.int32), out)
            s = jnp.where(lane_f == idx, removed, s)
        sel_ref[i] = out


SELECT_SEQS_PER_STEP = 4


def _sample_select(bsum, q, k_new, n_past):
    nb = bsum.shape[0]
    assert n_past < LANES
    ns = SELECT_SEQS_PER_STEP if nb % SELECT_SEQS_PER_STEP == 0 else 1
    per_b = lambda *s: pl.BlockSpec((ns,) + s, lambda b: (b,) + (0,) * len(s))
    return pl.pallas_call(
        functools.partial(_sample_select_kernel, n_past=n_past),
        out_shape=jax.ShapeDtypeStruct((nb, N_ATTN_HEADS, LANES), jnp.int32),
        grid=(nb // ns,),
        in_specs=[per_b(ATTN_DIM, LANES), per_b(1, ATTN_DIM), per_b(1, ATTN_DIM)],
        out_specs=per_b(N_ATTN_HEADS, LANES),
        compiler_params=_params(1),
        name="moba_sample_select",
    )(bsum, q.reshape(nb, 1, ATTN_DIM), k_new.reshape(nb, 1, ATTN_DIM))


N_SEL_PAGES = MOBA_TOPK * PAGES_PER_BLOCK


def _sample_attend_kernel(pg_ref, ok_ref, q_ref, knew_ref, vnew_ref, *refs):
    o_ref = refs[-1]
    k_refs = refs[:HEADS_PER_VREG * N_SEL_PAGES]
    v_refs = refs[HEADS_PER_VREG * N_SEL_PAGES:2 * HEADS_PER_VREG * N_SEL_PAGES]
    step = pl.program_id(0) * pl.num_programs(1) + pl.program_id(1)
    heads = range(HEADS_PER_VREG)
    chans = [slice(hh * ATTN_HEAD_DIM, (hh + 1) * ATTN_HEAD_DIM) for hh in heads]
    qhs = [q_ref[:, chans[hh]] * ATTN_SCALE for hh in heads]
    q8s = [jnp.broadcast_to(qh, (SUBLANES, ATTN_HEAD_DIM)).astype(BF16) for qh in qhs]
    raw = [[_dot(q8s[hh], k_refs[hh * N_SEL_PAGES + r][...].astype(BF16))[0:1, :]
            for r in range(N_SEL_PAGES)] for hh in heads]
    ps, ls, p_news = [], [], []
    for hh in heads:
        oks = [ok_ref[(step * HEADS_PER_VREG + hh) * MOBA_TOPK + t] for t in range(MOBA_TOPK)]
        lgs = [jnp.where(oks[r // PAGES_PER_BLOCK] > 0, raw[hh][r], NEG_INF) for r in range(N_SEL_PAGES)]
        lg_new = jnp.sum(qhs[hh] * knew_ref[:, chans[hh]], axis=-1, keepdims=True)
        m = lg_new
        for lg in lgs:
            m = jnp.maximum(m, jnp.max(lg, axis=-1, keepdims=True))
        p_new = jnp.exp(lg_new - m)
        pp = [jnp.exp(lg - m) for lg in lgs]
        l = p_new
        for p in pp:
            l = l + jnp.sum(p, axis=-1, keepdims=True)
        ps.append([jnp.broadcast_to(p, (SUBLANES, PAGE_SIZE)).astype(BF16) for p in pp])
        ls.append(l)
        p_news.append(p_new)
    pvs = [[_dot_nt(ps[hh][r], v_refs[hh * N_SEL_PAGES + r][...].astype(BF16))[0:1, :]
            for r in range(N_SEL_PAGES)] for hh in heads]
    outs = []
    for hh in heads:
        acc = p_news[hh] * vnew_ref[:, chans[hh]]
        for pv in pvs[hh]:
            acc = acc + pv
        outs.append(acc / ls[hh])
    o_ref[...] = jnp.concatenate(outs, axis=-1).astype(o_ref.dtype)


def _sample_attend(q, k_new, v_new, cache_k, cache_v, pages, ok):
    nb = q.shape[0]
    n_pairs = ATTN_DIM // LANES
    row = pl.BlockSpec((None, 1, LANES), lambda b, hp, pg, ok: (b, 0, hp))
    per_step = HEADS_PER_VREG * N_SEL_PAGES

    def page_spec(i):
        return pl.BlockSpec((None, ATTN_HEAD_DIM, PAGE_SIZE),
                            lambda b, hp, pg, ok: (pg[(b * n_pairs + hp) * per_step + i], 0, 0))

    page_specs = [page_spec(i) for i in range(per_step)]
    out = pl.pallas_call(
        _sample_attend_kernel,
        out_shape=jax.ShapeDtypeStruct((nb, 1, ATTN_DIM), BF16),
        grid_spec=pltpu.PrefetchScalarGridSpec(
            num_scalar_prefetch=2, grid=(nb, n_pairs),
            in_specs=[row, row, row] + page_specs + page_specs, out_specs=row),
        compiler_params=_params(2),
        name="moba_sample_attend",
    )(pages, ok, q.reshape(nb, 1, ATTN_DIM), k_new.reshape(nb, 1, ATTN_DIM), v_new.reshape(nb, 1, ATTN_DIM),
      *([cache_k] * len(page_specs)), *([cache_v] * len(page_specs)))
    return out.reshape(nb, ATTN_DIM)


def _channel_major_pool(cache):
    return jnp.transpose(cache, (0, 1, 3, 4, 2)).reshape(-1, N_ATTN_HEADS, ATTN_HEAD_DIM, PAGE_SIZE)


def _moba_sample(q, k_new, v_new, ck, cv, page_table, bsum):
    nb, n_pages = page_table.shape
    n_past = n_pages * PAGE_SIZE // MOBA_BLOCK
    assert n_pages * PAGE_SIZE == n_past * MOBA_BLOCK
    sel = _sample_select(bsum, q, k_new, n_past)[:, :, :MOBA_TOPK]
    ck = ck.reshape(-1, ATTN_HEAD_DIM, PAGE_SIZE)
    cv = cv.reshape(-1, ATTN_HEAD_DIM, PAGE_SIZE)
    ok = (sel < n_past).astype(jnp.int32)
    blk = jnp.minimum(sel, n_past - 1)
    logical = blk[..., None] * PAGES_PER_BLOCK + jnp.arange(PAGES_PER_BLOCK)
    pages = jnp.take_along_axis(page_table, logical.reshape(nb, -1), axis=1)
    slabs = pages.reshape(nb, N_ATTN_HEADS, N_SEL_PAGES) * N_ATTN_HEADS + jnp.arange(N_ATTN_HEADS)[None, :, None]
    return _sample_attend(q, k_new, v_new, ck, cv, slabs.reshape(-1), ok.reshape(-1))


def _merge_ln_kernel(x_ref, g_ref, ys_ref, ya_ref, ga_ref, gb_ref, ws_ref, wa_ref, wo_ref, lng_ref, lnb_ref, o_ref):
    merged = (jax.nn.sigmoid(ga_ref[...]) * _dot(ys_ref[...], ws_ref[...])
              + jax.nn.sigmoid(gb_ref[...]) * _dot(ya_ref[...], wa_ref[...]))
    mix = _dot(merged.astype(BF16), wo_ref[...])
    y = DEEPNORM_ALPHA * x_ref[...] + g_ref[...] * mix
    o_ref[...] = _layer_norm(y, lng_ref[...], lnb_ref[...])


def _merge_ln(x, mods, k, y_ssm, y_attn, ga, gb, ws, wa, wo, lng, lnb, tm, rows_per_batch):
    n, d = x.shape
    rows = lambda w: pl.BlockSpec((tm, w), lambda i: (i, 0))
    return pl.pallas_call(
        _merge_ln_kernel,
        out_shape=jax.ShapeDtypeStruct((n, d), F32),
        grid=(n // tm,),
        in_specs=[rows(d)] + _mod_specs(mods, (k,), tm, rows_per_batch)
        + [rows(D_INNER), rows(ATTN_DIM), rows(d), rows(d),
           _resident(ws.shape), _resident(wa.shape), _resident(wo.shape), _resident((1, d)), _resident((1, d))],
        out_specs=rows(d),
        compiler_params=_params(1),
        name="merge_ln",
    )(x, mods, y_ssm, y_attn, ga, gb, ws, wa, wo, lng.reshape(1, d), lnb.reshape(1, d))


PROMPT_TM = 512
IN_PROJ_TM = 256


def _split_w_in(w_in):
    edges = (0, D_INNER, D_INNER + CONV_DIM, D_INNER + CONV_DIM + N_SSM_HEADS)
    edges = edges + tuple(edges[-1] + i * ATTN_DIM for i in range(1, 6))
    parts = [w_in[:, a:b] for a, b in zip(edges[:-1], edges[1:])]
    parts[2] = jnp.pad(parts[2], ((0, 0), (0, LANES - N_SSM_HEADS)))
    return [p.astype(BF16) for p in parts]


def kernel(x_prompt, x_sample, cache_k, cache_v, state_conv, state_ssm, page_table, c_prompt, c_sample, w_ada, b_ada, ln_g, ln_b, w_ffn1_gu, w_ffn1_down, w_ffn2_gu, w_ffn2_down, w_in, conv_w, conv_b, dt_bias, a_log, d_skip, ssm_norm_g, w_branch_ssm, w_branch_attn, w_out):
    assert w_in.shape[0] == DEPTH == 1
    nb, seq, d = x_prompt.shape
    nbs, dec_seq, _ = x_sample.shape
    assert dec_seq == 1
    l = 0
    w1gu, w1d = w_ffn1_gu[l].astype(BF16), w_ffn1_down[l].astype(BF16)
    w2gu, w2d = w_ffn2_gu[l].astype(BF16), w_ffn2_down[l].astype(BF16)
    w_parts = _split_w_in(w_in[l])
    ws, wa, wo = w_branch_ssm[l].astype(BF16), w_branch_attn[l].astype(BF16), w_out[l].astype(BF16)
    ssd_w = (conv_w[l], conv_b[l], dt_bias[l], a_log[l], d_skip[l], ssm_norm_g[l])

    mods = _ada_mods(jnp.concatenate([c_prompt, c_sample], axis=0), w_ada[l], b_ada[l])
    mods_p = mods[:nb].reshape(nb * 9, 1, d)
    mods_s = mods[nb:]

    xp = x_prompt.reshape(nb * seq, d)
    xp = _ffn_ln(xp, mods_p, (0, 1, 2), w1gu, w1d, ln_g[l, 0], ln_b[l, 0], PROMPT_TM, seq)
    kv_t = (False, False, False, False, True, True, False, False)
    w_parts_p = [w.T if t else w for w, t in zip(w_parts, kv_t)]
    z, xc, dt, q, k_t, v_t, ga, gb, conv_p = _in_proj(xp, mods_p, (3, 4), w_parts_p, kv_t, IN_PROJ_TM, seq,
                                                      conv=(1, conv_w[l], conv_b[l]))
    y_ssm, ssm_p = _ssd_prompt(xc, z, dt, nb, seq, *ssd_w[2:])
    ck, cv = _channel_major_pool(cache_k), _channel_major_pool(cache_v)
    ck_pages = ck.reshape(-1, ATTN_DIM, PAGE_SIZE)
    n_pages = page_table.shape[1]
    moba_steps = nb * (ATTN_DIM // MOBA_STEP_CHANNELS) * (seq // MOBA_BLOCK)
    if n_pages % PAGES_PER_STEP == 0 and nbs * (n_pages // PAGES_PER_STEP) == moba_steps:
        y_attn, bsum = _moba_prompt(q, k_t, v_t, nb, seq, paged=(ck_pages, page_table))
    else:
        y_attn = _moba_prompt(q, k_t, v_t, nb, seq)
        bsum = _page_sums(ck_pages, page_table)
    xp = _merge_ln(xp, mods_p, 5, y_ssm, y_attn, ga, gb, ws, wa, wo, ln_g[l, 1], ln_b[l, 1], PROMPT_TM, seq)
    xp = _ffn_ln(xp, mods_p, (6, 7, 8), w2gu, w2d, ln_g[l, 2], ln_b[l, 2], PROMPT_TM, seq)

    xs = x_sample.reshape(nbs, d)
    xs = _ffn_ln(xs, mods_s, (0, 1, 2), w1gu, w1d, ln_g[l, 0], ln_b[l, 0], nbs, 1)
    zs, xbcs, dts, qs, ks, vs, gas, gbs = _in_proj(xs, mods_s, (3, 4), w_parts, (False,) * len(w_parts), nbs, 1)
    y_ssm_s, conv_s, ssm_s = _ssd_sample(xbcs, zs, dts, state_conv[l], state_ssm[l], *ssd_w)
    y_attn_s = _moba_sample(qs, ks, vs, ck, cv, page_table, bsum)
    xs = _merge_ln(xs, mods_s, 5, y_ssm_s.reshape(nbs, D_INNER), y_attn_s, gas, gbs, ws, wa, wo,
                   ln_g[l, 1], ln_b[l, 1], nbs, 1)
    xs = _ffn_ln(xs, mods_s, (6, 7, 8), w2gu, w2d, ln_g[l, 2], ln_b[l, 2], nbs, 1)

    heads = (N_ATTN_HEADS, ATTN_HEAD_DIM)
    state = (N_SSM_HEADS, SSM_HEAD_DIM, SSM_STATE)
    to_rows = lambda a_t: jnp.transpose(a_t.reshape((1, nb) + heads + (seq,)), (0, 1, 4, 2, 3))
    return (xp.reshape(nb, seq, d), xs.reshape(nbs, 1, d), to_rows(k_t), to_rows(v_t),
            conv_p[None], ssm_p.reshape((1, nb) + state),
            ks.reshape((1, nbs, 1) + heads), vs.reshape((1, nbs, 1) + heads),
            conv_s[None], ssm_s.reshape((1, nbs) + state))
```

```python
import functools

import jax
import jax.numpy as jnp
from jax import lax
from jax.experimental import pallas as pl
from jax.experimental.pallas import tpu as pltpu

F32 = jnp.float32
BF16 = jnp.bfloat16

D_MODEL = 1024
D_INNER = 2048
SSM_HEAD_DIM = 64
N_SSM_HEADS = 32
N_SSM_GROUPS = 4
SSM_STATE = 128
CONV_WIDTH = 4
CONV_DIM = D_INNER + 2 * N_SSM_GROUPS * SSM_STATE
SSD_CHUNK = 128
ATTN_HEAD_DIM = 64
N_ATTN_HEADS = 16
ATTN_DIM = 1024
MOBA_BLOCK = 256
MOBA_TOPK = 3
PAGE_SIZE = 128
D_FF = 2816
DEPTH = 1
DEEPNORM_ALPHA = (2.0 * DEPTH) ** 0.25
LN_EPS = 1e-5
RMS_EPS = 1e-5
NEG_INF = -1e30
ATTN_SCALE = ATTN_HEAD_DIM ** -0.5

LANES = 128
SUBLANES = 8
VMEM_LIMIT = 56 * 1024 * 1024


def _dot(a, b):
    return jnp.dot(a, b, preferred_element_type=F32)


def _dot_nt(a, b):
    return lax.dot_general(a, b, (((1,), (1,)), ((), ())), preferred_element_type=F32)


def _split2(a):
    hi = a.astype(BF16)
    lo = (a - hi.astype(F32)).astype(BF16)
    return hi, lo


def _split3(a):
    hi = a.astype(BF16)
    r = a - hi.astype(F32)
    mid = r.astype(BF16)
    lo = (r - mid.astype(F32)).astype(BF16)
    return hi, mid, lo


def _dot_sel(a, sel_bf16):
    hi, mid, lo = _split3(a)
    return _dot(hi, sel_bf16) + _dot(mid, sel_bf16) + _dot(lo, sel_bf16)


def _dot_sel2(a, sel_bf16):
    hi, lo = _split2(a)
    return _dot(hi, sel_bf16) + _dot(lo, sel_bf16)


def _dot_hp(a, b):
    ah, al = _split2(a)
    bh, bl = _split2(b)
    return _dot(ah, bh) + _dot(al, bh) + _dot(ah, bl)


def _dot_nt_hp(a, b):
    ah, al = _split2(a)
    bh, bl = _split2(b)
    return _dot_nt(ah, bh) + _dot_nt(al, bh) + _dot_nt(ah, bl)


def _silu(x):
    half = 0.5 * x
    return half + half * jnp.tanh(half)


def _softplus(x):
    return jnp.maximum(x, 0.0) + jnp.log1p(jnp.exp(-jnp.abs(x)))


def _layer_norm(y, g, b):
    mu = jnp.mean(y, axis=-1, keepdims=True)
    yc = y - mu
    var = jnp.mean(yc * yc, axis=-1, keepdims=True)
    return yc * lax.rsqrt(var + LN_EPS) * g + b


def _resident(shape):
    nd = len(shape)
    return pl.BlockSpec(shape, lambda *_: (0,) * nd, pipeline_mode=pl.Buffered(1))


def _params(n_axes):
    return pltpu.CompilerParams(dimension_semantics=("arbitrary",) * n_axes, vmem_limit_bytes=VMEM_LIMIT)


def _ada_kernel(c_ref, w_ref, b_ref, o_ref):
    s = _silu(c_ref[...])
    o_ref[...] = _dot_hp(s, w_ref[...]) + b_ref[...]


def _ada_mods(c, w_ada, b_ada):
    m, d = c.shape
    n = w_ada.shape[1]
    tn = 1024
    return pl.pallas_call(
        _ada_kernel,
        out_shape=jax.ShapeDtypeStruct((m, n), F32),
        grid=(n // tn,),
        in_specs=[pl.BlockSpec((m, d), lambda j: (0, 0)),
                  pl.BlockSpec((d, tn), lambda j: (0, j)),
                  pl.BlockSpec((1, tn), lambda j: (0, j))],
        out_specs=pl.BlockSpec((m, tn), lambda j: (0, j)),
        compiler_params=_params(1),
        name="ada_mods",
    )(c, w_ada, b_ada.reshape(1, n))


def _mod_specs(mods, ks, tm, rows_per_batch):
    if mods.ndim == 3:
        tiles_per_batch = rows_per_batch // tm
        return [pl.BlockSpec((None, 1, D_MODEL), functools.partial(
            lambda i, k: ((i // tiles_per_batch) * 9 + k, 0, 0), k=k)) for k in ks]
    return [pl.BlockSpec((tm, D_MODEL), functools.partial(lambda i, k: (i, k), k=k)) for k in ks]


MXU_DIM = 256
FF_CHUNK = 6 * MXU_DIM
FF_EDGES = tuple(range(0, D_FF, FF_CHUNK)) + (D_FF,)


def _ffn_ln_kernel(x_ref, sh_ref, sc_ref, g_ref, wgu_ref, wd_ref, lng_ref, lnb_ref, o_ref):
    x = x_ref[...]
    h = (x * (1.0 + sc_ref[...]) + sh_ref[...]).astype(BF16)
    acc = jnp.zeros(x.shape, F32)
    for lo, hi in zip(FF_EDGES[:-1], FF_EDGES[1:]):
        gate = _dot(h, wgu_ref[:, lo:hi])
        up = _dot(h, wgu_ref[:, D_FF + lo:D_FF + hi])
        act = (_silu(gate) * up).astype(BF16)
        acc = acc + _dot(act, wd_ref[lo:hi, :])
    y = DEEPNORM_ALPHA * x + 0.5 * g_ref[...] * acc
    o_ref[...] = _layer_norm(y, lng_ref[...], lnb_ref[...])


def _ffn_ln(x, mods, ks, wgu, wd, lng, lnb, tm, rows_per_batch):
    n, d = x.shape
    row = pl.BlockSpec((tm, d), lambda i: (i, 0))
    return pl.pallas_call(
        _ffn_ln_kernel,
        out_shape=jax.ShapeDtypeStruct((n, d), F32),
        grid=(n // tm,),
        in_specs=[row] + _mod_specs(mods, ks, tm, rows_per_batch)
        + [_resident(wgu.shape), _resident(wd.shape), _resident((1, d)), _resident((1, d))],
        out_specs=row,
        compiler_params=_params(1),
        name="ffn_ln",
    )(x, mods, mods, mods, wgu, wd, lng.reshape(1, d), lnb.reshape(1, d))


CARRY = SUBLANES
PROJ_PIECE = 512
CONV_CHUNKS_PER_PIECE = 2


def _in_proj_kernel(x_ref, sh_ref, sc_ref, *refs, transposed, conv_index, tiles_per_batch):
    n_out = len(transposed)
    w_refs, o_refs = refs[:n_out], refs[n_out + 2 * (conv_index is not None):]
    h = (x_ref[...] * (1.0 + sc_ref[...]) + sh_ref[...]).astype(BF16)

    def project(idx, lo=None, hi=None):
        if transposed[idx]:
            return _dot_nt(w_refs[idx][lo:hi, :], h)
        return _dot(h, w_refs[idx][:, lo:hi])

    conv_chunks = []
    if conv_index is not None:
        cw_ref, cb_ref = refs[n_out:n_out + 2]
        state_ref, buf_s = refs[-2:]
        tm = x_ref.shape[0]
        tile = pl.program_id(0) % tiles_per_batch

        n_lane_tiles = CONV_DIM // LANES

        @pl.when(tile == 0)
        def _():
            buf_s[:, 0:CARRY, :] = jnp.zeros((n_lane_tiles, CARRY, LANES), F32)

        raw = project(conv_index)
        for c in range(n_lane_tiles):
            buf_s[c, CARRY:CARRY + tm, :] = raw[:, c * LANES:(c + 1) * LANES]

        def conv_chunk(c):
            cols = slice(c * LANES, (c + 1) * LANES)
            acc = cb_ref[:, cols]
            for j in range(CONV_WIDTH):
                off = CARRY - (CONV_WIDTH - 1) + j
                acc = acc + cw_ref[j:j + 1, cols] * buf_s[c, off:off + tm, :]
            o_refs[conv_index][:, cols] = _silu(acc)

        conv_chunks = list(range(n_lane_tiles))

    for idx in range(n_out):
        if idx == conv_index:
            continue
        width = w_refs[idx].shape[0] if transposed[idx] else w_refs[idx].shape[1]
        for lo in range(0, width, PROJ_PIECE):
            hi = min(lo + PROJ_PIECE, width)
            piece = project(idx, lo, hi).astype(o_refs[idx].dtype)
            if transposed[idx]:
                o_refs[idx][lo:hi, :] = piece
            else:
                o_refs[idx][:, lo:hi] = piece
            for chunk in conv_chunks[:CONV_CHUNKS_PER_PIECE]:
                conv_chunk(chunk)
            del conv_chunks[:CONV_CHUNKS_PER_PIECE]
    for chunk in conv_chunks:
        conv_chunk(chunk)

    if conv_index is not None:
        @pl.when(tile == tiles_per_batch - 1)
        def _():
            for c in range(n_lane_tiles):
                state_ref[:, c * LANES:(c + 1) * LANES] = buf_s[c, CARRY + tm - (CONV_WIDTH - 1):CARRY + tm, :]

        buf_s[:, 0:CARRY, :] = buf_s[:, tm:tm + CARRY, :]


def _in_proj(x, mods, ks, weights, transposed, tm, rows_per_batch, conv=None):
    n, d = x.shape
    nb = n // rows_per_batch
    tiles_per_batch = max(rows_per_batch // tm, 1)
    row = pl.BlockSpec((tm, d), lambda i: (i, 0))
    out_shape, out_specs = [], []
    for w, t in zip(weights, transposed):
        if t:
            out_shape.append(jax.ShapeDtypeStruct((nb, w.shape[0], rows_per_batch), F32))
            out_specs.append(pl.BlockSpec((None, w.shape[0], tm),
                                          lambda i: (i // tiles_per_batch, 0, i % tiles_per_batch)))
        else:
            out_shape.append(jax.ShapeDtypeStruct((n, w.shape[1]), F32))
            out_specs.append(pl.BlockSpec((tm, w.shape[1]), lambda i: (i, 0)))
    in_specs = [row] + _mod_specs(mods, ks, tm, rows_per_batch) + [_resident(w.shape) for w in weights]
    args = [x, mods, mods, *weights]
    scratch = []
    if conv is not None:
        conv_index, conv_w, conv_b = conv
        in_specs += [_resident((CONV_WIDTH, CONV_DIM)), _resident((1, CONV_DIM))]
        args += [conv_w, conv_b.reshape(1, CONV_DIM)]
        out_shape.append(jax.ShapeDtypeStruct((nb, CONV_WIDTH - 1, CONV_DIM), F32))
        out_specs.append(pl.BlockSpec((None, CONV_WIDTH - 1, CONV_DIM), lambda i: (i // tiles_per_batch, 0, 0)))
        scratch.append(pltpu.VMEM((CONV_DIM // LANES, CARRY + tm, LANES), F32))
    return pl.pallas_call(
        functools.partial(_in_proj_kernel, transposed=tuple(transposed),
                          conv_index=None if conv is None else conv[0], tiles_per_batch=tiles_per_batch),
        out_shape=out_shape,
        grid=(n // tm,),
        in_specs=in_specs,
        out_specs=out_specs,
        scratch_shapes=scratch,
        compiler_params=_params(1),
        name="in_proj",
    )(*args)


GROUP_W = D_INNER // N_SSM_GROUPS


def _gated_rmsnorm(y, z, norm_g):
    yz = y * _silu(z)
    outs = []
    for g in range(N_SSM_GROUPS):
        blk = yz[:, g * GROUP_W:(g + 1) * GROUP_W]
        ms = jnp.mean(blk * blk, axis=-1, keepdims=True)
        outs.append(blk * lax.rsqrt(ms + RMS_EPS))
    return jnp.concatenate(outs, axis=-1) * norm_g


def _ssd_prompt_kernel(xc_s, z_ref, dt_ref, dtb_ref, alog_ref, dskip_ref, ng_ref, exp_ref, y_ref, ssm_ref, st_s):
    c = pl.program_id(1)
    q = SSD_CHUNK

    @pl.when(c == 0)
    def _():
        st_s[...] = jnp.zeros(st_s.shape, F32)

    expand = exp_ref[...]
    dt = _softplus(dt_ref[...] + dtb_ref[...])
    da = dt * (-jnp.exp(alog_ref[...]))
    row_i = lax.broadcasted_iota(jnp.int32, (q, q), 0)
    col_i = lax.broadcasted_iota(jnp.int32, (q, q), 1)
    causal = row_i >= col_i
    tril = jnp.where(causal, 1.0, 0.0).astype(BF16)
    a_cum = _dot_tril(tril, da)
    a_src = a_cum - jnp.log(dt)
    a_src_t = a_src.T
    a_last = a_cum[q - 1:q, :]
    dec_end_x = _dot_sel2(jnp.exp(a_last - a_src), expand)
    exp_a_x = _dot_sel2(jnp.exp(a_cum), expand)
    chunk_decay_x = _dot_sel(jnp.broadcast_to(jnp.exp(a_last), (SUBLANES, LANES)), expand)[0:1, :]

    xs = xc_s[:, 0:D_INNER]
    xs_b = xs.astype(BF16)
    xdec_b = (xs * dec_end_x).astype(BF16)
    lane = lax.broadcasted_iota(jnp.int32, (1, LANES), 1)
    zero_b = jnp.zeros((), BF16)

    y_parts = []
    for g in range(N_SSM_GROUPS):
        b_g = xc_s[:, D_INNER + g * SSM_STATE:D_INNER + (g + 1) * SSM_STATE]
        c_g = xc_s[:, D_INNER + (N_SSM_GROUPS + g) * SSM_STATE:D_INNER + (N_SSM_GROUPS + g + 1) * SSM_STATE]
        b_gb = b_g.astype(BF16)
        c_gb = c_g.astype(BF16)
        cb = _dot_nt(c_gb, b_gb)
        gcols = slice(g * GROUP_W, (g + 1) * GROUP_W)
        st_old = st_s[:, gcols]
        y_inter = _dot(c_gb, st_old.astype(BF16)) * exp_a_x[:, gcols]
        st_s[:, gcols] = chunk_decay_x[:, gcols] * st_old + _dot(b_g.T.astype(BF16), xdec_b[:, gcols])
        pair_out = []
        heads_per_group = N_SSM_HEADS // N_SSM_GROUPS
        for pr in range(heads_per_group // 2):
            pcols = slice(g * GROUP_W + pr * LANES, g * GROUP_W + (pr + 1) * LANES)
            x_pair = xs_b[:, pcols]
            acc = None
            for hh in range(2):
                h = g * heads_per_group + pr * 2 + hh
                seg = a_cum[:, h:h + 1] - a_src_t[h:h + 1, :]
                w = (cb * jnp.exp(jnp.where(causal, seg, -jnp.inf))).astype(BF16)
                x_h = jnp.where((lane // SSM_HEAD_DIM) == hh, x_pair, zero_b)
                t = _dot(w, x_h)
                acc = t if acc is None else acc + t
            pair_out.append(acc)
        y_parts.append(jnp.concatenate(pair_out, axis=-1) + y_inter)
    y = jnp.concatenate(y_parts, axis=-1) + dskip_ref[...] * xs
    y_ref[...] = _gated_rmsnorm(y, z_ref[...], ng_ref[...]).astype(y_ref.dtype)

    @pl.when(c == pl.num_programs(1) - 1)
    def _():
        for r in range(D_INNER // LANES):
            ssm_ref[r * LANES:(r + 1) * LANES, :] = st_s[:, r * LANES:(r + 1) * LANES].T


def _dot_tril(tril_bf16, a):
    hi, mid, lo = _split3(a)
    return _dot(tril_bf16, hi) + _dot(tril_bf16, mid) + _dot(tril_bf16, lo)


def _pad_lanes(v):
    return jnp.zeros((1, LANES), F32).at[0, :v.shape[0]].set(v)


def _expand_matrix():
    h = jnp.arange(LANES)[:, None]
    ch = jnp.arange(D_INNER)[None, :] // SSM_HEAD_DIM
    return (h == ch).astype(BF16)


def _ssd_prompt(xc, z, dt, nb, seq, dt_bias, a_log, d_skip, norm_g):
    nc = seq // SSD_CHUNK
    rows = lambda w: pl.BlockSpec((SSD_CHUNK, w), lambda b, c: (b * nc + c, 0))
    return pl.pallas_call(
        _ssd_prompt_kernel,
        out_shape=[jax.ShapeDtypeStruct((nb * seq, D_INNER), BF16),
                   jax.ShapeDtypeStruct((nb, D_INNER, SSM_STATE), F32)],
        grid=(nb, nc),
        in_specs=[rows(CONV_DIM), rows(D_INNER), rows(LANES), _resident((1, LANES)),
                  _resident((1, LANES)), _resident((1, D_INNER)), _resident((1, D_INNER)),
                  _resident((LANES, D_INNER))],
        out_specs=[rows(D_INNER), pl.BlockSpec((None, D_INNER, SSM_STATE), lambda b, c: (b, 0, 0))],
        scratch_shapes=[pltpu.VMEM((SSM_STATE, D_INNER), F32)],
        compiler_params=_params(2),
        name="ssd_prompt",
    )(xc, z, dt, _pad_lanes(dt_bias), _pad_lanes(a_log),
      jnp.repeat(d_skip, SSM_HEAD_DIM).reshape(1, D_INNER), norm_g.reshape(1, D_INNER), _expand_matrix())


def _column_block(row_vec, r):
    return jnp.broadcast_to(row_vec[:, r * LANES:(r + 1) * LANES], (LANES, LANES)).T


def _ssd_sample_kernel(xbc_ref, z_ref, dt_ref, buf_ref, h0_ref, cw_ref, cb_ref, dtb_ref, alog_ref, dskip_ref,
                       ng_ref, exp_ref, y_ref, conv_ref, ssm_ref):
    xr = xbc_ref[...]
    acc = cb_ref[...] + cw_ref[CONV_WIDTH - 1:CONV_WIDTH, :] * xr
    for j in range(CONV_WIDTH - 1):
        acc = acc + cw_ref[j:j + 1, :] * buf_ref[j:j + 1, :]
    xc = _silu(acc)
    for j in range(CONV_WIDTH - 2):
        conv_ref[j:j + 1, :] = buf_ref[j + 1:j + 2, :]
    conv_ref[CONV_WIDTH - 2:CONV_WIDTH - 1, :] = xr

    expand = exp_ref[...]
    dt = _softplus(dt_ref[...] + dtb_ref[...])
    dec = jnp.exp(dt * (-jnp.exp(alog_ref[...])))
    both = jnp.concatenate([jnp.broadcast_to(dt, (SUBLANES // 2, LANES)),
                            jnp.broadcast_to(dec, (SUBLANES // 2, LANES))], axis=0)
    both_x = _dot_sel(both, expand)
    dt_x = both_x[0:1, :]
    dec_x = both_x[SUBLANES // 2:SUBLANES // 2 + 1, :]
    xs = xc[:, 0:D_INNER]
    xdt = xs * dt_x

    y_parts = []
    blocks_per_group = GROUP_W // LANES
    for g in range(N_SSM_GROUPS):
        b_g = xc[:, D_INNER + g * SSM_STATE:D_INNER + (g + 1) * SSM_STATE]
        c_g = xc[:, D_INNER + (N_SSM_GROUPS + g) * SSM_STATE:D_INNER + (N_SSM_GROUPS + g + 1) * SSM_STATE]
        c8 = jnp.broadcast_to(c_g, (SUBLANES, SSM_STATE)).astype(BF16)
        for rb in range(blocks_per_group):
            r = g * blocks_per_group + rb
            rows = slice(r * LANES, (r + 1) * LANES)
            h_new = _column_block(dec_x, r) * h0_ref[rows, :] + _column_block(xdt, r) * b_g
            ssm_ref[rows, :] = h_new
            y_parts.append(_dot_nt(c8, h_new.astype(BF16))[0:1, :])
    y = jnp.concatenate(y_parts, axis=-1) + dskip_ref[...] * xs
    y_ref[...] = _gated_rmsnorm(y, z_ref[...], ng_ref[...]).astype(y_ref.dtype)


def _ssd_sample(xbc, z, dt, state_conv, state_ssm, conv_w, conv_b, dt_bias, a_log, d_skip, norm_g):
    nb = xbc.shape[0]
    per_b = lambda *s: pl.BlockSpec((None,) + s, lambda b: (b,) + (0,) * len(s))
    return pl.pallas_call(
        _ssd_sample_kernel,
        out_shape=[jax.ShapeDtypeStruct((nb, 1, D_INNER), BF16),
                   jax.ShapeDtypeStruct((nb, CONV_WIDTH - 1, CONV_DIM), F32),
                   jax.ShapeDtypeStruct((nb, D_INNER, SSM_STATE), F32)],
        grid=(nb,),
        in_specs=[per_b(1, CONV_DIM), per_b(1, D_INNER), per_b(1, LANES),
                  per_b(CONV_WIDTH - 1, CONV_DIM), per_b(D_INNER, SSM_STATE),
                  _resident((CONV_WIDTH, CONV_DIM)), _resident((1, CONV_DIM)), _resident((1, LANES)),
                  _resident((1, LANES)), _resident((1, D_INNER)), _resident((1, D_INNER)),
                  _resident((LANES, D_INNER))],
        out_specs=[per_b(1, D_INNER), per_b(CONV_WIDTH - 1, CONV_DIM), per_b(D_INNER, SSM_STATE)],
        compiler_params=_params(1),
        name="ssd_sample",
    )(xbc.reshape(nb, 1, CONV_DIM), z.reshape(nb, 1, D_INNER), dt.reshape(nb, 1, LANES), state_conv,
      state_ssm.reshape(nb, D_INNER, SSM_STATE), conv_w, conv_b.reshape(1, CONV_DIM), _pad_lanes(dt_bias),
      _pad_lanes(a_log), jnp.repeat(d_skip, SSM_HEAD_DIM).reshape(1, D_INNER), norm_g.reshape(1, D_INNER),
      _expand_matrix())


HEADS_PER_VREG = LANES // ATTN_HEAD_DIM
MOBA_PAIRS_PER_STEP = 2
MOBA_HEADS_PER_STEP = MOBA_PAIRS_PER_STEP * HEADS_PER_VREG
MOBA_STEP_CHANNELS = MOBA_PAIRS_PER_STEP * LANES


def _moba_prompt_kernel(*refs, page_steps):
    page_tasks = []
    if page_steps is not None:
        page_refs, bsum_ref = refs[4:4 + PAGES_PER_STEP], refs[5 + PAGES_PER_STEP]
        q_ref, k_ref, v_ref, o_ref = refs[1:4] + (refs[4 + PAGES_PER_STEP],)
        kb_s, vt_s, means_s, sel_s, qb_s, m_s, l_s, acc_s = refs[6 + PAGES_PER_STEP:]
        step = ((pl.program_id(0) * pl.num_programs(1) + pl.program_id(1)) * pl.num_programs(2)
                + pl.program_id(2))
        page_step = step % page_steps

        @pl.when(page_step == 0)
        def _():
            bsum_ref[...] = jnp.zeros(bsum_ref.shape, F32)

        page_tasks = list(range(0, ATTN_DIM, LANES))
    else:
        q_ref, k_ref, v_ref, o_ref, kb_s, vt_s, means_s, sel_s, qb_s, m_s, l_s, acc_s = refs

    def run_page_tasks(n):
        for lo in page_tasks[:n]:
            _page_sum_rows(page_refs, bsum_ref, page_step, lo)
        del page_tasks[:n]

    tq = MOBA_BLOCK
    nblk = kb_s.shape[0]
    own = pl.program_id(2)
    pair_cols = [slice(pr * LANES, (pr + 1) * LANES) for pr in range(MOBA_PAIRS_PER_STEP)]
    head_rows = [slice(h * ATTN_HEAD_DIM, (h + 1) * ATTN_HEAD_DIM) for h in range(MOBA_HEADS_PER_STEP)]

    @pl.when(own == 0)
    def _():
        for j in range(nblk):
            for pr in range(MOBA_PAIRS_PER_STEP):
                ksum = jnp.zeros((1, LANES), F32)
                for t in range(MOBA_BLOCK // LANES):
                    cols = slice(j * MOBA_BLOCK + t * LANES, j * MOBA_BLOCK + (t + 1) * LANES)
                    kblk = k_ref[pair_cols[pr], cols].T
                    kb_s[j, t * LANES:(t + 1) * LANES, pair_cols[pr]] = kblk.astype(BF16)
                    ksum = ksum + jnp.sum(kblk, axis=0, keepdims=True)
                means_s[j:j + 1, pair_cols[pr]] = ksum * (1.0 / MOBA_BLOCK)
            vt_s[j] = v_ref[:, j * MOBA_BLOCK:(j + 1) * MOBA_BLOCK].astype(BF16)

    lane = lax.broadcasted_iota(jnp.int32, (1, LANES), 1)
    blk_i = lax.broadcasted_iota(jnp.int32, (nblk, tq), 0)
    causal = (lax.broadcasted_iota(jnp.int32, (MOBA_BLOCK, tq), 0)
              <= lax.broadcasted_iota(jnp.int32, (MOBA_BLOCK, tq), 1))
    heads = range(MOBA_HEADS_PER_STEP)
    cols_of = [pair_cols[h // HEADS_PER_VREG] for h in heads]
    qms = [jnp.where((lane // ATTN_HEAD_DIM) == h % HEADS_PER_VREG, q_ref[:, cols_of[h]], 0.0) for h in heads]
    qbs = [(qm * ATTN_SCALE).astype(BF16) for qm in qms]
    for h in heads:
        qb_s[h] = qbs[h]
    raw_own = [_dot_nt(kb_s[own, :, cols_of[h]], qbs[h]) for h in heads]
    run_page_tasks(2)
    scores = [_dot_nt_hp(means_s[:, cols_of[h]], qms[h]) for h in heads]
    run_page_tasks(2)
    for h in heads:
        s_t = jnp.where(blk_i < own, scores[h], NEG_INF)
        for j in range(nblk):
            sj = s_t[j:j + 1, :]
            beats = jnp.where(s_t > sj, 1.0, 0.0) + jnp.where((s_t == sj) & (blk_i < j), 1.0, 0.0)
            rank = jnp.sum(beats, axis=0, keepdims=True)
            sel_s[h * nblk + j] = jnp.where(rank < float(MOBA_TOPK), 1.0, 0.0) * jnp.where(j < own, 1.0, 0.0)
    run_page_tasks(2)
    ps = []
    for h in heads:
        lg = jnp.where(causal, raw_own[h], NEG_INF)
        m = jnp.max(lg, axis=0, keepdims=True)
        p = jnp.exp(lg - m)
        m_s[h] = m
        l_s[h] = jnp.sum(p, axis=0, keepdims=True)
        ps.append(p.astype(BF16))
    run_page_tasks(len(page_tasks))
    for h in heads:
        acc_s[h] = _dot(vt_s[own, head_rows[h], :], ps[h])

    def past_blocks(js):
        raw = [[_dot_nt(kb_s[j, :, cols_of[h]], qb_s[h]) for h in heads] for j in js]
        ps, alphas = [], []
        for h in heads:
            lgs = [jnp.where(sel_s[h * nblk + j] > 0.5, raw[i][h], NEG_INF) for i, j in enumerate(js)]
            m_old = m_s[h]
            m_new = m_old
            for lg in lgs:
                m_new = jnp.maximum(m_new, jnp.max(lg, axis=0, keepdims=True))
            alpha = jnp.exp(m_old - m_new)
            pp = [jnp.exp(lg - m_new) for lg in lgs]
            l = alpha * l_s[h]
            for p in pp:
                l = l + jnp.sum(p, axis=0, keepdims=True)
            m_s[h] = m_new
            l_s[h] = l
            ps.append([p.astype(BF16) for p in pp])
            alphas.append(alpha)
        pvs = [[_dot(vt_s[j, head_rows[h], :], ps[h][i]) for i, j in enumerate(js)] for h in heads]
        for h in heads:
            acc = alphas[h] * acc_s[h]
            for pv in pvs[h]:
                acc = acc + pv
            acc_s[h] = acc

    def pair_body(jj, carry):
        past_blocks([2 * jj, 2 * jj + 1])
        return carry

    lax.fori_loop(0, own // 2, pair_body, 0)

    @pl.when(own % 2 == 1)
    def _():
        past_blocks([own - 1])
    for pr in range(MOBA_PAIRS_PER_STEP):
        heads = range(pr * HEADS_PER_VREG, (pr + 1) * HEADS_PER_VREG)
        o_t = jnp.concatenate([acc_s[h] / l_s[h] for h in heads], axis=0)
        o_ref[:, pair_cols[pr]] = o_t.T.astype(o_ref.dtype)


def _moba_prompt(q, k_t, v_t, nb, seq, paged=None):
    assert seq % MOBA_BLOCK == 0
    nblk = seq // MOBA_BLOCK
    gw = MOBA_STEP_CHANNELS
    n_groups = ATTN_DIM // gw
    q3, k3, v3 = q.reshape(nb, seq, ATTN_DIM), k_t, v_t
    kv_spec = pl.BlockSpec((None, gw, seq), lambda b, g, t, *_: (b, g, 0))
    q_spec = pl.BlockSpec((None, MOBA_BLOCK, gw), lambda b, g, t, *_: (b, t, g))
    scratch = [pltpu.VMEM((nblk, MOBA_BLOCK, gw), BF16),
               pltpu.VMEM((nblk, gw, MOBA_BLOCK), BF16),
               pltpu.VMEM((nblk, gw), F32),
               pltpu.VMEM((MOBA_HEADS_PER_STEP * nblk, 1, MOBA_BLOCK), F32),
               pltpu.VMEM((MOBA_HEADS_PER_STEP, MOBA_BLOCK, LANES), BF16),
               pltpu.VMEM((MOBA_HEADS_PER_STEP, 1, MOBA_BLOCK), F32),
               pltpu.VMEM((MOBA_HEADS_PER_STEP, 1, MOBA_BLOCK), F32),
               pltpu.VMEM((MOBA_HEADS_PER_STEP, ATTN_HEAD_DIM, MOBA_BLOCK), F32)]
    grid = (nb, n_groups, nblk)
    out_attn = jax.ShapeDtypeStruct((nb, seq, ATTN_DIM), BF16)
    if paged is None:
        out = pl.pallas_call(
            functools.partial(_moba_prompt_kernel, page_steps=None),
            out_shape=out_attn, grid=grid, in_specs=[q_spec, kv_spec, kv_spec], out_specs=q_spec,
            scratch_shapes=scratch, compiler_params=_params(3), name="moba_prompt",
        )(q3, k3, v3)
        return out.reshape(nb * seq, ATTN_DIM)
    cache_t, page_table = paged
    nbs, n_pages = page_table.shape
    page_steps = n_pages // PAGES_PER_STEP
    assert nbs * page_steps == nb * n_groups * nblk and n_pages // PAGES_PER_BLOCK <= LANES

    def step_of(b, g, t):
        return (b * n_groups + g) * nblk + t

    page_specs = _page_specs(lambda b, g, t, pt, r: pt[step_of(b, g, t) // page_steps,
                                                      (step_of(b, g, t) % page_steps) * PAGES_PER_STEP + r])
    out, bsum = pl.pallas_call(
        functools.partial(_moba_prompt_kernel, page_steps=page_steps),
        out_shape=[out_attn, jax.ShapeDtypeStruct((nbs, ATTN_DIM, LANES), F32)],
        grid_spec=pltpu.PrefetchScalarGridSpec(
            num_scalar_prefetch=1, grid=grid, in_specs=[q_spec, kv_spec, kv_spec] + page_specs,
            out_specs=[q_spec, pl.BlockSpec((None, ATTN_DIM, LANES),
                                            lambda b, g, t, pt: (step_of(b, g, t) // page_steps, 0, 0))],
            scratch_shapes=scratch),
        compiler_params=_params(3),
        name="moba_prompt_pages",
    )(page_table, q3, k3, v3, *([cache_t] * PAGES_PER_STEP))
    return out.reshape(nb * seq, ATTN_DIM), bsum


PAGES_PER_STEP = 16
PAGES_PER_BLOCK = MOBA_BLOCK // PAGE_SIZE


def _page_sum_rows(page_refs, o_ref, step, lo):
    blocks_per_step = PAGES_PER_STEP // PAGES_PER_BLOCK
    lane = lax.broadcasted_iota(jnp.int32, (1, LANES), 1)
    rows = slice(lo, lo + LANES)
    acc = o_ref[rows, :]
    for t in range(blocks_per_step):
        pages = page_refs[t * PAGES_PER_BLOCK][rows, :]
        for u in range(1, PAGES_PER_BLOCK):
            pages = pages + page_refs[t * PAGES_PER_BLOCK + u][rows, :]
        acc = jnp.where(lane == step * blocks_per_step + t, jnp.sum(pages, axis=-1, keepdims=True), acc)
    o_ref[rows, :] = acc


def _page_sums_kernel(pt_ref, *refs):
    o_ref = refs[-1]
    i = pl.program_id(1)

    @pl.when(i == 0)
    def _():
        o_ref[...] = jnp.zeros(o_ref.shape, F32)

    for lo in range(0, ATTN_DIM, LANES):
        _page_sum_rows(refs[:-1], o_ref, i, lo)


def _page_specs(index_of):
    return [pl.BlockSpec((None, ATTN_DIM, PAGE_SIZE), functools.partial(
        lambda *a, r: (index_of(*a, r), 0, 0), r=r)) for r in range(PAGES_PER_STEP)]


def _page_sums(cache_t, page_table):
    nb, n_pages = page_table.shape
    assert n_pages % PAGES_PER_STEP == 0 and n_pages // PAGES_PER_BLOCK <= LANES
    specs = _page_specs(lambda b, i, pt, r: pt[b, i * PAGES_PER_STEP + r])
    return pl.pallas_call(
        _page_sums_kernel,
        out_shape=jax.ShapeDtypeStruct((nb, ATTN_DIM, LANES), F32),
        grid_spec=pltpu.PrefetchScalarGridSpec(
            num_scalar_prefetch=1, grid=(nb, n_pages // PAGES_PER_STEP), in_specs=specs,
            out_specs=pl.BlockSpec((None, ATTN_DIM, LANES), lambda b, i, pt: (b, 0, 0))),
        compiler_params=_params(2),
        name="page_sums",
    )(page_table, *([cache_t] * PAGES_PER_STEP))


def _sample_select_kernel(bsum_ref, q_ref, knew_ref, sel_ref, *, n_past):
    seqs = range(bsum_ref.shape[0])
    head = lax.broadcasted_iota(jnp.int32, (N_ATTN_HEADS, ATTN_DIM), 0)
    chan_head = lax.broadcasted_iota(jnp.int32, (N_ATTN_HEADS, ATTN_DIM), 1) // ATTN_HEAD_DIM
    q_bds = [jnp.where(head == chan_head, q_ref[i], 0.0) for i in seqs]
    scores = [_dot_hp(q_bds[i], bsum_ref[i] * (1.0 / MOBA_BLOCK)) for i in seqs]
    lane = lax.broadcasted_iota(jnp.int32, (N_ATTN_HEADS, LANES), 1)
    lane_f = lane.astype(F32)
    own = n_past
    removed = -jnp.inf
    for i in seqs:
        s_new = jnp.sum(q_bds[i] * (knew_ref[i] * (1.0 / MOBA_BLOCK)), axis=-1, keepdims=True)
        s = jnp.where(lane == n_past, s_new, scores[i])
        s = jnp.where(lane < own, s, NEG_INF)
        s = jnp.where(lane <= n_past, s, removed)
        out = jnp.zeros(s.shape, jnp.int32)
        for r in range(MOBA_TOPK):
            mx = jnp.max(s, axis=-1, keepdims=True)
            idx = jnp.min(jnp.where(s == mx, lane_f, float(LANES)), axis=-1, keepdims=True)
            out = jnp.where(lane == r, idx.astype(jnp.int32), out)
            s = jnp.where(lane_f == idx, removed, s)
        sel_ref[i] = out


SELECT_SEQS_PER_STEP = 4


def _sample_select(bsum, q, k_new, n_past):
    nb = bsum.shape[0]
    assert n_past < LANES
    ns = SELECT_SEQS_PER_STEP if nb % SELECT_SEQS_PER_STEP == 0 else 1
    per_b = lambda *s: pl.BlockSpec((ns,) + s, lambda b: (b,) + (0,) * len(s))
    return pl.pallas_call(
        functools.partial(_sample_select_kernel, n_past=n_past),
        out_shape=jax.ShapeDtypeStruct((nb, N_ATTN_HEADS, LANES), jnp.int32),
        grid=(nb // ns,),
        in_specs=[per_b(ATTN_DIM, LANES), per_b(1, ATTN_DIM), per_b(1, ATTN_DIM)],
        out_specs=per_b(N_ATTN_HEADS, LANES),
        compiler_params=_params(1),
        name="moba_sample_select",
    )(bsum, q.reshape(nb, 1, ATTN_DIM), k_new.reshape(nb, 1, ATTN_DIM))


N_SEL_PAGES = MOBA_TOPK * PAGES_PER_BLOCK


def _sample_attend_kernel(pg_ref, ok_ref, q_ref, knew_ref, vnew_ref, *refs):
    o_ref = refs[-1]
    k_refs = refs[:HEADS_PER_VREG * N_SEL_PAGES]
    v_refs = refs[HEADS_PER_VREG * N_SEL_PAGES:2 * HEADS_PER_VREG * N_SEL_PAGES]
    step = pl.program_id(0) * pl.num_programs(1) + pl.program_id(1)
    heads = range(HEADS_PER_VREG)
    chans = [slice(hh * ATTN_HEAD_DIM, (hh + 1) * ATTN_HEAD_DIM) for hh in heads]
    qhs = [q_ref[:, chans[hh]] * ATTN_SCALE for hh in heads]
    q8s = [jnp.broadcast_to(qh, (SUBLANES, ATTN_HEAD_DIM)).astype(BF16) for qh in qhs]
    raw = [[_dot(q8s[hh], k_refs[hh * N_SEL_PAGES + r][...].astype(BF16))[0:1, :]
            for r in range(N_SEL_PAGES)] for hh in heads]
    ps, ls, p_news = [], [], []
    for hh in heads:
        oks = [ok_ref[(step * HEADS_PER_VREG + hh) * MOBA_TOPK + t] for t in range(MOBA_TOPK)]
        lgs = [jnp.where(oks[r // PAGES_PER_BLOCK] > 0, raw[hh][r], NEG_INF) for r in range(N_SEL_PAGES)]
        lg_new = jnp.sum(qhs[hh] * knew_ref[:, chans[hh]], axis=-1, keepdims=True)
        m = lg_new
        for lg in lgs:
            m = jnp.maximum(m, jnp.max(lg, axis=-1, keepdims=True))
        p_new = jnp.exp(lg_new - m)
        pp = [jnp.exp(lg - m) for lg in lgs]
        l = p_new
        for p in pp:
            l = l + jnp.sum(p, axis=-1, keepdims=True)
        ps.append([jnp.broadcast_to(p, (SUBLANES, PAGE_SIZE)).astype(BF16) for p in pp])
        ls.append(l)
        p_news.append(p_new)
    pvs = [[_dot_nt(ps[hh][r], v_refs[hh * N_SEL_PAGES + r][...].astype(BF16))[0:1, :]
            for r in range(N_SEL_PAGES)] for hh in heads]
    outs = []
    for hh in heads:
        acc = p_news[hh] * vnew_ref[:, chans[hh]]
        for pv in pvs[hh]:
            acc = acc + pv
        outs.append(acc / ls[hh])
    o_ref[...] = jnp.concatenate(outs, axis=-1).astype(o_ref.dtype)


def _sample_attend(q, k_new, v_new, cache_k, cache_v, pages, ok):
    nb = q.shape[0]
    n_pairs = ATTN_DIM // LANES
    row = pl.BlockSpec((None, 1, LANES), lambda b, hp, pg, ok: (b, 0, hp))
    per_step = HEADS_PER_VREG * N_SEL_PAGES

    def page_spec(i):
        return pl.BlockSpec((None, ATTN_HEAD_DIM, PAGE_SIZE),
                            lambda b, hp, pg, ok: (pg[(b * n_pairs + hp) * per_step + i], 0, 0))

    page_specs = [page_spec(i) for i in range(per_step)]
    out = pl.pallas_call(
        _sample_attend_kernel,
        out_shape=jax.ShapeDtypeStruct((nb, 1, ATTN_DIM), BF16),
        grid_spec=pltpu.PrefetchScalarGridSpec(
            num_scalar_prefetch=2, grid=(nb, n_pairs),
            in_specs=[row, row, row] + page_specs + page_specs, out_specs=row),
        compiler_params=_params(2),
        name="moba_sample_attend",
    )(pages, ok, q.reshape(nb, 1, ATTN_DIM), k_new.reshape(nb, 1, ATTN_DIM), v_new.reshape(nb, 1, ATTN_DIM),
      *([cache_k] * len(page_specs)), *([cache_v] * len(page_specs)))
    return out.reshape(nb, ATTN_DIM)


def _channel_major_pool(cache):
    return jnp.transpose(cache, (0, 1, 3, 4, 2)).reshape(-1, N_ATTN_HEADS, ATTN_HEAD_DIM, PAGE_SIZE)


def _moba_sample(q, k_new, v_new, ck, cv, page_table, bsum):
    nb, n_pages = page_table.shape
    n_past = n_pages * PAGE_SIZE // MOBA_BLOCK
    assert n_pages * PAGE_SIZE == n_past * MOBA_BLOCK
    sel = _sample_select(bsum, q, k_new, n_past)[:, :, :MOBA_TOPK]
    ck = ck.reshape(-1, ATTN_HEAD_DIM, PAGE_SIZE)
    cv = cv.reshape(-1, ATTN_HEAD_DIM, PAGE_SIZE)
    ok = (sel < n_past).astype(jnp.int32)
    blk = jnp.minimum(sel, n_past - 1)
    logical = blk[..., None] * PAGES_PER_BLOCK + jnp.arange(PAGES_PER_BLOCK)
    pages = jnp.take_along_axis(page_table, logical.reshape(nb, -1), axis=1)
    slabs = pages.reshape(nb, N_ATTN_HEADS, N_SEL_PAGES) * N_ATTN_HEADS + jnp.arange(N_ATTN_HEADS)[None, :, None]
    return _sample_attend(q, k_new, v_new, ck, cv, slabs.reshape(-1), ok.reshape(-1))


def _merge_ln_kernel(x_ref, g_ref, ys_ref, ya_ref, ga_ref, gb_ref, ws_ref, wa_ref, wo_ref, lng_ref, lnb_ref, o_ref):
    merged = (jax.nn.sigmoid(ga_ref[...]) * _dot(ys_ref[...], ws_ref[...])
              + jax.nn.sigmoid(gb_ref[...]) * _dot(ya_ref[...], wa_ref[...]))
    mix = _dot(merged.astype(BF16), wo_ref[...])
    y = DEEPNORM_ALPHA * x_ref[...] + g_ref[...] * mix
    o_ref[...] = _layer_norm(y, lng_ref[...], lnb_ref[...])


def _merge_ln(x, mods, k, y_ssm, y_attn, ga, gb, ws, wa, wo, lng, lnb, tm, rows_per_batch):
    n, d = x.shape
    rows = lambda w: pl.BlockSpec((tm, w), lambda i: (i, 0))
    return pl.pallas_call(
        _merge_ln_kernel,
        out_shape=jax.ShapeDtypeStruct((n, d), F32),
        grid=(n // tm,),
        in_specs=[rows(d)] + _mod_specs(mods, (k,), tm, rows_per_batch)
        + [rows(D_INNER), rows(ATTN_DIM), rows(d), rows(d),
           _resident(ws.shape), _resident(wa.shape), _resident(wo.shape), _resident((1, d)), _resident((1, d))],
        out_specs=rows(d),
        compiler_params=_params(1),
        name="merge_ln",
    )(x, mods, y_ssm, y_attn, ga, gb, ws, wa, wo, lng.reshape(1, d), lnb.reshape(1, d))


PROMPT_TM = 512
IN_PROJ_TM = 256


def _split_w_in(w_in):
    edges = (0, D_INNER, D_INNER + CONV_DIM, D_INNER + CONV_DIM + N_SSM_HEADS)
    edges = edges + tuple(edges[-1] + i * ATTN_DIM for i in range(1, 6))
    parts = [w_in[:, a:b] for a, b in zip(edges[:-1], edges[1:])]
    parts[2] = jnp.pad(parts[2], ((0, 0), (0, LANES - N_SSM_HEADS)))
    return [p.astype(BF16) for p in parts]


def kernel(x_prompt, x_sample, cache_k, cache_v, state_conv, state_ssm, page_table, c_prompt, c_sample, w_ada, b_ada, ln_g, ln_b, w_ffn1_gu, w_ffn1_down, w_ffn2_gu, w_ffn2_down, w_in, conv_w, conv_b, dt_bias, a_log, d_skip, ssm_norm_g, w_branch_ssm, w_branch_attn, w_out):
    assert w_in.shape[0] == DEPTH == 1
    nb, seq, d = x_prompt.shape
    nbs, dec_seq, _ = x_sample.shape
    assert dec_seq == 1
    l = 0
    w1gu, w1d = w_ffn1_gu[l].astype(BF16), w_ffn1_down[l].astype(BF16)
    w2gu, w2d = w_ffn2_gu[l].astype(BF16), w_ffn2_down[l].astype(BF16)
    w_parts = _split_w_in(w_in[l])
    ws, wa, wo = w_branch_ssm[l].astype(BF16), w_branch_attn[l].astype(BF16), w_out[l].astype(BF16)
    ssd_w = (conv_w[l], conv_b[l], dt_bias[l], a_log[l], d_skip[l], ssm_norm_g[l])

    mods = _ada_mods(jnp.concatenate([c_prompt, c_sample], axis=0), w_ada[l], b_ada[l])
    mods_p = mods[:nb].reshape(nb * 9, 1, d)
    mods_s = mods[nb:]

    xp = x_prompt.reshape(nb * seq, d)
    xp = _ffn_ln(xp, mods_p, (0, 1, 2), w1gu, w1d, ln_g[l, 0], ln_b[l, 0], PROMPT_TM, seq)
    kv_t = (False, False, False, False, True, True, False, False)
    w_parts_p = [w.T if t else w for w, t in zip(w_parts, kv_t)]
    z, xc, dt, q, k_t, v_t, ga, gb, conv_p = _in_proj(xp, mods_p, (3, 4), w_parts_p, kv_t, IN_PROJ_TM, seq,
                                                      conv=(1, conv_w[l], conv_b[l]))
    y_ssm, ssm_p = _ssd_prompt(xc, z, dt, nb, seq, *ssd_w[2:])
    ck, cv = _channel_major_pool(cache_k), _channel_major_pool(cache_v)
    ck_pages = ck.reshape(-1, ATTN_DIM, PAGE_SIZE)
    n_pages = page_table.shape[1]
    moba_steps = nb * (ATTN_DIM // MOBA_STEP_CHANNELS) * (seq // MOBA_BLOCK)
    if n_pages % PAGES_PER_STEP == 0 and nbs * (n_pages // PAGES_PER_STEP) == moba_steps:
        y_attn, bsum = _moba_prompt(q, k_t, v_t, nb, seq, paged=(ck_pages, page_table))
    else:
        y_attn = _moba_prompt(q, k_t, v_t, nb, seq)
        bsum = _page_sums(ck_pages, page_table)
    xp = _merge_ln(xp, mods_p, 5, y_ssm, y_attn, ga, gb, ws, wa, wo, ln_g[l, 1], ln_b[l, 1], PROMPT_TM, seq)
    xp = _ffn_ln(xp, mods_p, (6, 7, 8), w2gu, w2d, ln_g[l, 2], ln_b[l, 2], PROMPT_TM, seq)

    xs = x_sample.reshape(nbs, d)
    xs = _ffn_ln(xs, mods_s, (0, 1, 2), w1gu, w1d, ln_g[l, 0], ln_b[l, 0], nbs, 1)
    zs, xbcs, dts, qs, ks, vs, gas, gbs = _in_proj(xs, mods_s, (3, 4), w_parts, (False,) * len(w_parts), nbs, 1)
    y_ssm_s, conv_s, ssm_s = _ssd_sample(xbcs, zs, dts, state_conv[l], state_ssm[l], *ssd_w)
    y_attn_s = _moba_sample(qs, ks, vs, ck, cv, page_table, bsum)
    xs = _merge_ln(xs, mods_s, 5, y_ssm_s.reshape(nbs, D_INNER), y_attn_s, gas, gbs, ws, wa, wo,
                   ln_g[l, 1], ln_b[l, 1], nbs, 1)
    xs = _ffn_ln(xs, mods_s, (6, 7, 8), w2gu, w2d, ln_g[l, 2], ln_b[l, 2], nbs, 1)

    heads = (N_ATTN_HEADS, ATTN_HEAD_DIM)
    state = (N_SSM_HEADS, SSM_HEAD_DIM, SSM_STATE)
    to_rows = lambda a_t: jnp.transpose(a_t.reshape((1, nb) + heads + (seq,)), (0, 1, 4, 2, 3))
    return (xp.reshape(nb, seq, d), xs.reshape(nbs, 1, d), to_rows(k_t), to_rows(v_t),
            conv_p[None], ssm_p.reshape((1, nb) + state),
            ks.reshape((1, nbs, 1) + heads), vs.reshape((1, nbs, 1) + heads),
            conv_s[None], ssm_s.reshape((1, nbs) + state))
```

```python
import functools

import jax
import jax.numpy as jnp
from jax import lax
from jax.experimental import pallas as pl
from jax.experimental.pallas import tpu as pltpu

F32 = jnp.float32
BF16 = jnp.bfloat16

D_MODEL = 1024
D_INNER = 2048
SSM_HEAD_DIM = 64
N_SSM_HEADS = 32
N_SSM_GROUPS = 4
SSM_STATE = 128
CONV_WIDTH = 4
CONV_DIM = D_INNER + 2 * N_SSM_GROUPS * SSM_STATE
SSD_CHUNK = 128
ATTN_HEAD_DIM = 64
N_ATTN_HEADS = 16
ATTN_DIM = 1024
MOBA_BLOCK = 256
MOBA_TOPK = 3
PAGE_SIZE = 128
D_FF = 2816
DEPTH = 1
DEEPNORM_ALPHA = (2.0 * DEPTH) ** 0.25
LN_EPS = 1e-5
RMS_EPS = 1e-5
NEG_INF = -1e30
ATTN_SCALE = ATTN_HEAD_DIM ** -0.5

LANES = 128
SUBLANES = 8
VMEM_LIMIT = 56 * 1024 * 1024


def _dot(a, b):
    return jnp.dot(a, b, preferred_element_type=F32)


def _dot_nt(a, b):
    return lax.dot_general(a, b, (((1,), (1,)), ((), ())), preferred_element_type=F32)


def _split2(a):
    hi = a.astype(BF16)
    lo = (a - hi.astype(F32)).astype(BF16)
    return hi, lo


def _split3(a):
    hi = a.astype(BF16)
    r = a - hi.astype(F32)
    mid = r.astype(BF16)
    lo = (r - mid.astype(F32)).astype(BF16)
    return hi, mid, lo


def _dot_sel(a, sel_bf16):
    hi, mid, lo = _split3(a)
    return _dot(hi, sel_bf16) + _dot(mid, sel_bf16) + _dot(lo, sel_bf16)


def _dot_sel2(a, sel_bf16):
    hi, lo = _split2(a)
    return _dot(hi, sel_bf16) + _dot(lo, sel_bf16)


def _dot_hp(a, b):
    ah, al = _split2(a)
    bh, bl = _split2(b)
    return _dot(ah, bh) + _dot(al, bh) + _dot(ah, bl)


def _dot_nt_hp(a, b):
    ah, al = _split2(a)
    bh, bl = _split2(b)
    return _dot_nt(ah, bh) + _dot_nt(al, bh) + _dot_nt(ah, bl)


def _silu(x):
    half = 0.5 * x
    return half + half * jnp.tanh(half)


def _softplus(x):
    return jnp.maximum(x, 0.0) + jnp.log1p(jnp.exp(-jnp.abs(x)))


def _layer_norm(y, g, b):
    mu = jnp.mean(y, axis=-1, keepdims=True)
    yc = y - mu
    var = jnp.mean(yc * yc, axis=-1, keepdims=True)
    return yc * lax.rsqrt(var + LN_EPS) * g + b


def _resident(shape):
    nd = len(shape)
    return pl.BlockSpec(shape, lambda *_: (0,) * nd, pipeline_mode=pl.Buffered(1))


def _params(n_axes):
    return pltpu.CompilerParams(dimension_semantics=("arbitrary",) * n_axes, vmem_limit_bytes=VMEM_LIMIT)


def _ada_kernel(c_ref, w_ref, b_ref, o_ref):
    s = _silu(c_ref[...])
    o_ref[...] = _dot_hp(s, w_ref[...]) + b_ref[...]


def _ada_mods(c, w_ada, b_ada):
    m, d = c.shape
    n = w_ada.shape[1]
    tn = 1024
    return pl.pallas_call(
        _ada_kernel,
        out_shape=jax.ShapeDtypeStruct((m, n), F32),
        grid=(n // tn,),
        in_specs=[pl.BlockSpec((m, d), lambda j: (0, 0)),
                  pl.BlockSpec((d, tn), lambda j: (0, j)),
                  pl.BlockSpec((1, tn), lambda j: (0, j))],
        out_specs=pl.BlockSpec((m, tn), lambda j: (0, j)),
        compiler_params=_params(1),
        name="ada_mods",
    )(c, w_ada, b_ada.reshape(1, n))


def _mod_specs(mods, ks, tm, rows_per_batch):
    if mods.ndim == 3:
        tiles_per_batch = rows_per_batch // tm
        return [pl.BlockSpec((None, 1, D_MODEL), functools.partial(
            lambda i, k: ((i // tiles_per_batch) * 9 + k, 0, 0), k=k)) for k in ks]
    return [pl.BlockSpec((tm, D_MODEL), functools.partial(lambda i, k: (i, k), k=k)) for k in ks]


MXU_DIM = 256
FF_CHUNK = 6 * MXU_DIM
FF_EDGES = tuple(range(0, D_FF, FF_CHUNK)) + (D_FF,)


def _ffn_ln_kernel(x_ref, sh_ref, sc_ref, g_ref, wgu_ref, wd_ref, lng_ref, lnb_ref, o_ref):
    x = x_ref[...]
    h = (x * (1.0 + sc_ref[...]) + sh_ref[...]).astype(BF16)
    acc = jnp.zeros(x.shape, F32)
    for lo, hi in zip(FF_EDGES[:-1], FF_EDGES[1:]):
        gate = _dot(h, wgu_ref[:, lo:hi])
        up = _dot(h, wgu_ref[:, D_FF + lo:D_FF + hi])
        act = (_silu(gate) * up).astype(BF16)
        acc = acc + _dot(act, wd_ref[lo:hi, :])
    y = DEEPNORM_ALPHA * x + 0.5 * g_ref[...] * acc
    o_ref[...] = _layer_norm(y, lng_ref[...], lnb_ref[...])


def _ffn_ln(x, mods, ks, wgu, wd, lng, lnb, tm, rows_per_batch):
    n, d = x.shape
    row = pl.BlockSpec((tm, d), lambda i: (i, 0))
    return pl.pallas_call(
        _ffn_ln_kernel,
        out_shape=jax.ShapeDtypeStruct((n, d), F32),
        grid=(n // tm,),
        in_specs=[row] + _mod_specs(mods, ks, tm, rows_per_batch)
        + [_resident(wgu.shape), _resident(wd.shape), _resident((1, d)), _resident((1, d))],
        out_specs=row,
        compiler_params=_params(1),
        name="ffn_ln",
    )(x, mods, mods, mods, wgu, wd, lng.reshape(1, d), lnb.reshape(1, d))


CARRY = SUBLANES
PROJ_PIECE = 512
CONV_CHUNKS_PER_PIECE = 2


def _in_proj_kernel(x_ref, sh_ref, sc_ref, *refs, transposed, conv_index, tiles_per_batch):
    n_out = len(transposed)
    w_refs, o_refs = refs[:n_out], refs[n_out + 2 * (conv_index is not None):]
    h = (x_ref[...] * (1.0 + sc_ref[...]) + sh_ref[...]).astype(BF16)

    def project(idx, lo=None, hi=None):
        if transposed[idx]:
            return _dot_nt(w_refs[idx][lo:hi, :], h)
        return _dot(h, w_refs[idx][:, lo:hi])

    conv_chunks = []
    if conv_index is not None:
        cw_ref, cb_ref = refs[n_out:n_out + 2]
        state_ref, buf_s = refs[-2:]
        tm = x_ref.shape[0]
        tile = pl.program_id(0) % tiles_per_batch

        n_lane_tiles = CONV_DIM // LANES

        @pl.when(tile == 0)
        def _():
            buf_s[:, 0:CARRY, :] = jnp.zeros((n_lane_tiles, CARRY, LANES), F32)

        raw = project(conv_index)
        for c in range(n_lane_tiles):
            buf_s[c, CARRY:CARRY + tm, :] = raw[:, c * LANES:(c + 1) * LANES]

        def conv_chunk(c):
            cols = slice(c * LANES, (c + 1) * LANES)
            acc = cb_ref[:, cols]
            for j in range(CONV_WIDTH):
                off = CARRY - (CONV_WIDTH - 1) + j
                acc = acc + cw_ref[j:j + 1, cols] * buf_s[c, off:off + tm, :]
            o_refs[conv_index][:, cols] = _silu(acc)

        conv_chunks = list(range(n_lane_tiles))

    for idx in range(n_out):
        if idx == conv_index:
            continue
        width = w_refs[idx].shape[0] if transposed[idx] else w_refs[idx].shape[1]
        for lo in range(0, width, PROJ_PIECE):
            hi = min(lo + PROJ_PIECE, width)
            piece = project(idx, lo, hi).astype(o_refs[idx].dtype)
            if transposed[idx]:
                o_refs[idx][lo:hi, :] = piece
            else:
                o_refs[idx][:, lo:hi] = piece
            for chunk in conv_chunks[:CONV_CHUNKS_PER_PIECE]:
                conv_chunk(chunk)
            del conv_chunks[:CONV_CHUNKS_PER_PIECE]
    for chunk in conv_chunks:
        conv_chunk(chunk)

    if conv_index is not None:
        @pl.when(tile == tiles_per_batch - 1)
        def _():
            for c in range(n_lane_tiles):
                state_ref[:, c * LANES:(c + 1) * LANES] = buf_s[c, CARRY + tm - (CONV_WIDTH - 1):CARRY + tm, :]

        buf_s[:, 0:CARRY, :] = buf_s[:, tm:tm + CARRY, :]


def _in_proj(x, mods, ks, weights, transposed, tm, rows_per_batch, conv=None):
    n, d = x.shape
    nb = n // rows_per_batch
    tiles_per_batch = max(rows_per_batch // tm, 1)
    row = pl.BlockSpec((tm, d), lambda i: (i, 0))
    out_shape, out_specs = [], []
    for w, t in zip(weights, transposed):
        if t:
            out_shape.append(jax.ShapeDtypeStruct((nb, w.shape[0], rows_per_batch), F32))
            out_specs.append(pl.BlockSpec((None, w.shape[0], tm),
                                          lambda i: (i // tiles_per_batch, 0, i % tiles_per_batch)))
        else:
            out_shape.append(jax.ShapeDtypeStruct((n, w.shape[1]), F32))
            out_specs.append(pl.BlockSpec((tm, w.shape[1]), lambda i: (i, 0)))
    in_specs = [row] + _mod_specs(mods, ks, tm, rows_per_batch) + [_resident(w.shape) for w in weights]
    args = [x, mods, mods, *weights]
    scratch = []
    if conv is not None:
        conv_index, conv_w, conv_b = conv
        in_specs += [_resident((CONV_WIDTH, CONV_DIM)), _resident((1, CONV_DIM))]
        args += [conv_w, conv_b.reshape(1, CONV_DIM)]
        out_shape.append(jax.ShapeDtypeStruct((nb, CONV_WIDTH - 1, CONV_DIM), F32))
        out_specs.append(pl.BlockSpec((None, CONV_WIDTH - 1, CONV_DIM), lambda i: (i // tiles_per_batch, 0, 0)))
        scratch.append(pltpu.VMEM((CONV_DIM // LANES, CARRY + tm, LANES), F32))
    return pl.pallas_call(
        functools.partial(_in_proj_kernel, transposed=tuple(transposed),
                          conv_index=None if conv is None else conv[0], tiles_per_batch=tiles_per_batch),
        out_shape=out_shape,
        grid=(n // tm,),
        in_specs=in_specs,
        out_specs=out_specs,
        scratch_shapes=scratch,
        compiler_params=_params(1),
        name="in_proj",
    )(*args)


GROUP_W = D_INNER // N_SSM_GROUPS


def _gated_rmsnorm(y, z, norm_g):
    yz = y * _silu(z)
    outs = []
    for g in range(N_SSM_GROUPS):
        blk = yz[:, g * GROUP_W:(g + 1) * GROUP_W]
        ms = jnp.mean(blk * blk, axis=-1, keepdims=True)
        outs.append(blk * lax.rsqrt(ms + RMS_EPS))
    return jnp.concatenate(outs, axis=-1) * norm_g


def _ssd_prompt_kernel(xc_s, z_ref, dt_ref, dtb_ref, alog_ref, dskip_ref, ng_ref, exp_ref, y_ref, ssm_ref, st_s):
    c = pl.program_id(1)
    q = SSD_CHUNK

    @pl.when(c == 0)
    def _():
        st_s[...] = jnp.zeros(st_s.shape, F32)

    expand = exp_ref[...]
    dt = _softplus(dt_ref[...] + dtb_ref[...])
    da = dt * (-jnp.exp(alog_ref[...]))
    row_i = lax.broadcasted_iota(jnp.int32, (q, q), 0)
    col_i = lax.broadcasted_iota(jnp.int32, (q, q), 1)
    causal = row_i >= col_i
    tril = jnp.where(causal, 1.0, 0.0).astype(BF16)
    a_cum = _dot_tril(tril, da)
    a_src = a_cum - jnp.log(dt)
    a_src_t = a_src.T
    a_last = a_cum[q - 1:q, :]
    dec_end_x = _dot_sel2(jnp.exp(a_last - a_src), expand)
    exp_a_x = _dot_sel2(jnp.exp(a_cum), expand)
    chunk_decay_x = _dot_sel(jnp.broadcast_to(jnp.exp(a_last), (SUBLANES, LANES)), expand)[0:1, :]

    xs = xc_s[:, 0:D_INNER]
    xs_b = xs.astype(BF16)
    xdec_b = (xs * dec_end_x).astype(BF16)
    lane = lax.broadcasted_iota(jnp.int32, (1, LANES), 1)
    zero_b = jnp.zeros((), BF16)

    y_parts = []
    for g in range(N_SSM_GROUPS):
        b_g = xc_s[:, D_INNER + g * SSM_STATE:D_INNER + (g + 1) * SSM_STATE]
        c_g = xc_s[:, D_INNER + (N_SSM_GROUPS + g) * SSM_STATE:D_INNER + (N_SSM_GROUPS + g + 1) * SSM_STATE]
        b_gb = b_g.astype(BF16)
        c_gb = c_g.astype(BF16)
        cb = _dot_nt(c_gb, b_gb)
        gcols = slice(g * GROUP_W, (g + 1) * GROUP_W)
        st_old = st_s[:, gcols]
        y_inter = _dot(c_gb, st_old.astype(BF16)) * exp_a_x[:, gcols]
        st_s[:, gcols] = chunk_decay_x[:, gcols] * st_old + _dot(b_g.T.astype(BF16), xdec_b[:, gcols])
        pair_out = []
        heads_per_group = N_SSM_HEADS // N_SSM_GROUPS
        for pr in range(heads_per_group // 2):
            pcols = slice(g * GROUP_W + pr * LANES, g * GROUP_W + (pr + 1) * LANES)
            x_pair = xs_b[:, pcols]
            acc = None
            for hh in range(2):
                h = g * heads_per_group + pr * 2 + hh
                seg = a_cum[:, h:h + 1] - a_src_t[h:h + 1, :]
                w = (cb * jnp.exp(jnp.where(causal, seg, -jnp.inf))).astype(BF16)
                x_h = jnp.where((lane // SSM_HEAD_DIM) == hh, x_pair, zero_b)
                t = _dot(w, x_h)
                acc = t if acc is None else acc + t
            pair_out.append(acc)
        y_parts.append(jnp.concatenate(pair_out, axis=-1) + y_inter)
    y = jnp.concatenate(y_parts, axis=-1) + dskip_ref[...] * xs
    y_ref[...] = _gated_rmsnorm(y, z_ref[...], ng_ref[...]).astype(y_ref.dtype)

    @pl.when(c == pl.num_programs(1) - 1)
    def _():
        for r in range(D_INNER // LANES):
            ssm_ref[r * LANES:(r + 1) * LANES, :] = st_s[:, r * LANES:(r + 1) * LANES].T


def _dot_tril(tril_bf16, a):
    hi, mid, lo = _split3(a)
    return _dot(tril_bf16, hi) + _dot(tril_bf16, mid) + _dot(tril_bf16, lo)


def _pad_lanes(v):
    return jnp.zeros((1, LANES), F32).at[0, :v.shape[0]].set(v)


def _expand_matrix():
    h = jnp.arange(LANES)[:, None]
    ch = jnp.arange(D_INNER)[None, :] // SSM_HEAD_DIM
    return (h == ch).astype(BF16)


def _ssd_prompt(xc, z, dt, nb, seq, dt_bias, a_log, d_skip, norm_g):
    nc = seq // SSD_CHUNK
    rows = lambda w: pl.BlockSpec((SSD_CHUNK, w), lambda b, c: (b * nc + c, 0))
    return pl.pallas_call(
        _ssd_prompt_kernel,
        out_shape=[jax.ShapeDtypeStruct((nb * seq, D_INNER), BF16),
                   jax.ShapeDtypeStruct((nb, D_INNER, SSM_STATE), F32)],
        grid=(nb, nc),
        in_specs=[rows(CONV_DIM), rows(D_INNER), rows(LANES), _resident((1, LANES)),
                  _resident((1, LANES)), _resident((1, D_INNER)), _resident((1, D_INNER)),
                  _resident((LANES, D_INNER))],
        out_specs=[rows(D_INNER), pl.BlockSpec((None, D_INNER, SSM_STATE), lambda b, c: (b, 0, 0))],
        scratch_shapes=[pltpu.VMEM((SSM_STATE, D_INNER), F32)],
        compiler_params=_params(2),
        name="ssd_prompt",
    )(xc, z, dt, _pad_lanes(dt_bias), _pad_lanes(a_log),
      jnp.repeat(d_skip, SSM_HEAD_DIM).reshape(1, D_INNER), norm_g.reshape(1, D_INNER), _expand_matrix())


def _column_block(row_vec, r):
    return jnp.broadcast_to(row_vec[:, r * LANES:(r + 1) * LANES], (LANES, LANES)).T


def _ssd_sample_kernel(xbc_ref, z_ref, dt_ref, buf_ref, h0_ref, cw_ref, cb_ref, dtb_ref, alog_ref, dskip_ref,
                       ng_ref, exp_ref, y_ref, conv_ref, ssm_ref):
    xr = xbc_ref[...]
    acc = cb_ref[...] + cw_ref[CONV_WIDTH - 1:CONV_WIDTH, :] * xr
    for j in range(CONV_WIDTH - 1):
        acc = acc + cw_ref[j:j + 1, :] * buf_ref[j:j + 1, :]
    xc = _silu(acc)
    for j in range(CONV_WIDTH - 2):
        conv_ref[j:j + 1, :] = buf_ref[j + 1:j + 2, :]
    conv_ref[CONV_WIDTH - 2:CONV_WIDTH - 1, :] = xr

    expand = exp_ref[...]
    dt = _softplus(dt_ref[...] + dtb_ref[...])
    dec = jnp.exp(dt * (-jnp.exp(alog_ref[...])))
    both = jnp.concatenate([jnp.broadcast_to(dt, (SUBLANES // 2, LANES)),
                            jnp.broadcast_to(dec, (SUBLANES // 2, LANES))], axis=0)
    both_x = _dot_sel(both, expand)
    dt_x = both_x[0:1, :]
    dec_x = both_x[SUBLANES // 2:SUBLANES // 2 + 1, :]
    xs = xc[:, 0:D_INNER]
    xdt = xs * dt_x

    y_parts = []
    blocks_per_group = GROUP_W // LANES
    for g in range(N_SSM_GROUPS):
        b_g = xc[:, D_INNER + g * SSM_STATE:D_INNER + (g + 1) * SSM_STATE]
        c_g = xc[:, D_INNER + (N_SSM_GROUPS + g) * SSM_STATE:D_INNER + (N_SSM_GROUPS + g + 1) * SSM_STATE]
        c8 = jnp.broadcast_to(c_g, (SUBLANES, SSM_STATE)).astype(BF16)
        for rb in range(blocks_per_group):
            r = g * blocks_per_group + rb
            rows = slice(r * LANES, (r + 1) * LANES)
            h_new = _column_block(dec_x, r) * h0_ref[rows, :] + _column_block(xdt, r) * b_g
            ssm_ref[rows, :] = h_new
            y_parts.append(_dot_nt(c8, h_new.astype(BF16))[0:1, :])
    y = jnp.concatenate(y_parts, axis=-1) + dskip_ref[...] * xs
    y_ref[...] = _gated_rmsnorm(y, z_ref[...], ng_ref[...]).astype(y_ref.dtype)


def _ssd_sample(xbc, z, dt, state_conv, state_ssm, conv_w, conv_b, dt_bias, a_log, d_skip, norm_g):
    nb = xbc.shape[0]
    per_b = lambda *s: pl.BlockSpec((None,) + s, lambda b: (b,) + (0,) * len(s))
    return pl.pallas_call(
        _ssd_sample_kernel,
        out_shape=[jax.ShapeDtypeStruct((nb, 1, D_INNER), BF16),
                   jax.ShapeDtypeStruct((nb, CONV_WIDTH - 1, CONV_DIM), F32),
                   jax.ShapeDtypeStruct((nb, D_INNER, SSM_STATE), F32)],
        grid=(nb,),
        in_specs=[per_b(1, CONV_DIM), per_b(1, D_INNER), per_b(1, LANES),
                  per_b(CONV_WIDTH - 1, CONV_DIM), per_b(D_INNER, SSM_STATE),
                  _resident((CONV_WIDTH, CONV_DIM)), _resident((1, CONV_DIM)), _resident((1, LANES)),
                  _resident((1, LANES)), _resident((1, D_INNER)), _resident((1, D_INNER)),
                  _resident((LANES, D_INNER))],
        out_specs=[per_b(1, D_INNER), per_b(CONV_WIDTH - 1, CONV_DIM), per_b(D_INNER, SSM_STATE)],
        compiler_params=_params(1),
        name="ssd_sample",
    )(xbc.reshape(nb, 1, CONV_DIM), z.reshape(nb, 1, D_INNER), dt.reshape(nb, 1, LANES), state_conv,
      state_ssm.reshape(nb, D_INNER, SSM_STATE), conv_w, conv_b.reshape(1, CONV_DIM), _pad_lanes(dt_bias),
      _pad_lanes(a_log), jnp.repeat(d_skip, SSM_HEAD_DIM).reshape(1, D_INNER), norm_g.reshape(1, D_INNER),
      _expand_matrix())


HEADS_PER_VREG = LANES // ATTN_HEAD_DIM
MOBA_PAIRS_PER_STEP = 2
MOBA_HEADS_PER_STEP = MOBA_PAIRS_PER_STEP * HEADS_PER_VREG
MOBA_STEP_CHANNELS = MOBA_PAIRS_PER_STEP * LANES


def _moba_prompt_kernel(*refs, page_steps, step_pages):
    page_tasks = []
    if page_steps is not None:
        page_refs, bsum_ref = refs[4:4 + step_pages], refs[5 + step_pages]
        q_ref, k_ref, v_ref, o_ref = refs[1:4] + (refs[4 + step_pages],)
        kb_s, vt_s, means_s, sel_s, qb_s, m_s, l_s, acc_s = refs[6 + step_pages:]
        step = ((pl.program_id(0) * pl.num_programs(1) + pl.program_id(1)) * pl.num_programs(2)
                + pl.program_id(2))
        page_step = step % page_steps
        first_block = page_step * (step_pages // PAGES_PER_BLOCK)

        @pl.when(page_step == 0)
        def _():
            bsum_ref[...] = jnp.zeros(bsum_ref.shape, F32)

        page_tasks = list(range(0, ATTN_DIM, LANES))
    else:
        q_ref, k_ref, v_ref, o_ref, kb_s, vt_s, means_s, sel_s, qb_s, m_s, l_s, acc_s = refs

    def run_page_tasks(n):
        for lo in page_tasks[:n]:
            _page_sum_rows(page_refs, bsum_ref, first_block, lo)
        del page_tasks[:n]

    tq = MOBA_BLOCK
    nblk = kb_s.shape[0]
    own = pl.program_id(2)
    pair_cols = [slice(pr * LANES, (pr + 1) * LANES) for pr in range(MOBA_PAIRS_PER_STEP)]
    head_rows = [slice(h * ATTN_HEAD_DIM, (h + 1) * ATTN_HEAD_DIM) for h in range(MOBA_HEADS_PER_STEP)]

    @pl.when(own == 0)
    def _():
        for j in range(nblk):
            for pr in range(MOBA_PAIRS_PER_STEP):
                ksum = jnp.zeros((1, LANES), F32)
                for t in range(MOBA_BLOCK // LANES):
                    cols = slice(j * MOBA_BLOCK + t * LANES, j * MOBA_BLOCK + (t + 1) * LANES)
                    kblk = k_ref[pair_cols[pr], cols].T
                    kb_s[j, t * LANES:(t + 1) * LANES, pair_cols[pr]] = kblk.astype(BF16)
                    ksum = ksum + jnp.sum(kblk, axis=0, keepdims=True)
                means_s[j:j + 1, pair_cols[pr]] = ksum * (1.0 / MOBA_BLOCK)
            vt_s[j] = v_ref[:, j * MOBA_BLOCK:(j + 1) * MOBA_BLOCK].astype(BF16)

    lane = lax.broadcasted_iota(jnp.int32, (1, LANES), 1)
    blk_i = lax.broadcasted_iota(jnp.int32, (nblk, tq), 0)
    causal = (lax.broadcasted_iota(jnp.int32, (MOBA_BLOCK, tq), 0)
              <= lax.broadcasted_iota(jnp.int32, (MOBA_BLOCK, tq), 1))
    heads = range(MOBA_HEADS_PER_STEP)
    cols_of = [pair_cols[h // HEADS_PER_VREG] for h in heads]
    qms = [jnp.where((lane // ATTN_HEAD_DIM) == h % HEADS_PER_VREG, q_ref[:, cols_of[h]], 0.0) for h in heads]
    qbs = [(qm * ATTN_SCALE).astype(BF16) for qm in qms]
    for h in heads:
        qb_s[h] = qbs[h]
    raw_own = [_dot_nt(kb_s[own, :, cols_of[h]], qbs[h]) for h in heads]
    run_page_tasks(2)
    scores = [_dot_nt_hp(means_s[:, cols_of[h]], qms[h]) for h in heads]
    run_page_tasks(2)
    for h in heads:
        s_t = jnp.where(blk_i < own, scores[h], NEG_INF)
        for j in range(nblk):
            sj = s_t[j:j + 1, :]
            beats = jnp.where(s_t > sj, 1.0, 0.0) + jnp.where((s_t == sj) & (blk_i < j), 1.0, 0.0)
            rank = jnp.sum(beats, axis=0, keepdims=True)
            sel_s[h * nblk + j] = jnp.where(rank < float(MOBA_TOPK), 1.0, 0.0) * jnp.where(j < own, 1.0, 0.0)
    run_page_tasks(2)
    ps = []
    for h in heads:
        lg = jnp.where(causal, raw_own[h], NEG_INF)
        m = jnp.max(lg, axis=0, keepdims=True)
        p = jnp.exp(lg - m)
        m_s[h] = m
        l_s[h] = jnp.sum(p, axis=0, keepdims=True)
        ps.append(p.astype(BF16))
    run_page_tasks(len(page_tasks))
    for h in heads:
        acc_s[h] = _dot(vt_s[own, head_rows[h], :], ps[h])

    def past_blocks(js):
        raw = [[_dot_nt(kb_s[j, :, cols_of[h]], qb_s[h]) for h in heads] for j in js]
        ps, alphas = [], []
        for h in heads:
            lgs = [jnp.where(sel_s[h * nblk + j] > 0.5, raw[i][h], NEG_INF) for i, j in enumerate(js)]
            m_old = m_s[h]
            m_new = m_old
            for lg in lgs:
                m_new = jnp.maximum(m_new, jnp.max(lg, axis=0, keepdims=True))
            alpha = jnp.exp(m_old - m_new)
            pp = [jnp.exp(lg - m_new) for lg in lgs]
            l = alpha * l_s[h]
            for p in pp:
                l = l + jnp.sum(p, axis=0, keepdims=True)
            m_s[h] = m_new
            l_s[h] = l
            ps.append([p.astype(BF16) for p in pp])
            alphas.append(alpha)
        pvs = [[_dot(vt_s[j, head_rows[h], :], ps[h][i]) for i, j in enumerate(js)] for h in heads]
        for h in heads:
            acc = alphas[h] * acc_s[h]
            for pv in pvs[h]:
                acc = acc + pv
            acc_s[h] = acc

    def pair_body(jj, carry):
        past_blocks([2 * jj, 2 * jj + 1])
        return carry

    lax.fori_loop(0, own // 2, pair_body, 0)

    @pl.when(own % 2 == 1)
    def _():
        past_blocks([own - 1])
    for pr in range(MOBA_PAIRS_PER_STEP):
        heads = range(pr * HEADS_PER_VREG, (pr + 1) * HEADS_PER_VREG)
        o_t = jnp.concatenate([acc_s[h] / l_s[h] for h in heads], axis=0)
        o_ref[:, pair_cols[pr]] = o_t.T.astype(o_ref.dtype)


def _moba_prompt(q, k_t, v_t, nb, seq, paged=None):
    assert seq % MOBA_BLOCK == 0
    nblk = seq // MOBA_BLOCK
    gw = MOBA_STEP_CHANNELS
    n_groups = ATTN_DIM // gw
    q3, k3, v3 = q.reshape(nb, seq, ATTN_DIM), k_t, v_t
    kv_spec = pl.BlockSpec((None, gw, seq), lambda b, g, t, *_: (b, g, 0))
    q_spec = pl.BlockSpec((None, MOBA_BLOCK, gw), lambda b, g, t, *_: (b, t, g))
    scratch = [pltpu.VMEM((nblk, MOBA_BLOCK, gw), BF16),
               pltpu.VMEM((nblk, gw, MOBA_BLOCK), BF16),
               pltpu.VMEM((nblk, gw), F32),
               pltpu.VMEM((MOBA_HEADS_PER_STEP * nblk, 1, MOBA_BLOCK), F32),
               pltpu.VMEM((MOBA_HEADS_PER_STEP, MOBA_BLOCK, LANES), BF16),
               pltpu.VMEM((MOBA_HEADS_PER_STEP, 1, MOBA_BLOCK), F32),
               pltpu.VMEM((MOBA_HEADS_PER_STEP, 1, MOBA_BLOCK), F32),
               pltpu.VMEM((MOBA_HEADS_PER_STEP, ATTN_HEAD_DIM, MOBA_BLOCK), F32)]
    grid = (nb, n_groups, nblk)
    out_attn = jax.ShapeDtypeStruct((nb, seq, ATTN_DIM), BF16)
    if paged is None:
        out = pl.pallas_call(
            functools.partial(_moba_prompt_kernel, page_steps=None, step_pages=0),
            out_shape=out_attn, grid=grid, in_specs=[q_spec, kv_spec, kv_spec], out_specs=q_spec,
            scratch_shapes=scratch, compiler_params=_params(3), name="moba_prompt",
        )(q3, k3, v3)
        return out.reshape(nb * seq, ATTN_DIM)
    cache_t, page_table, step_pages = paged
    nbs, n_pages = page_table.shape
    page_steps = nb * n_groups * nblk // nbs
    assert nbs * page_steps == nb * n_groups * nblk and page_steps * step_pages <= n_pages
    assert step_pages % PAGES_PER_BLOCK == 0 and n_pages // PAGES_PER_BLOCK <= LANES

    def step_of(b, g, t):
        return (b * n_groups + g) * nblk + t

    page_specs = _page_specs(lambda b, g, t, pt, r: pt[step_of(b, g, t) // page_steps,
                                                      (step_of(b, g, t) % page_steps) * step_pages + r],
                             step_pages)
    out, bsum = pl.pallas_call(
        functools.partial(_moba_prompt_kernel, page_steps=page_steps, step_pages=step_pages),
        out_shape=[out_attn, jax.ShapeDtypeStruct((nbs, ATTN_DIM, LANES), F32)],
        grid_spec=pltpu.PrefetchScalarGridSpec(
            num_scalar_prefetch=1, grid=grid, in_specs=[q_spec, kv_spec, kv_spec] + page_specs,
            out_specs=[q_spec, pl.BlockSpec((None, ATTN_DIM, LANES),
                                            lambda b, g, t, pt: (step_of(b, g, t) // page_steps, 0, 0))],
            scratch_shapes=scratch),
        compiler_params=_params(3),
        name="moba_prompt_pages",
    )(page_table, q3, k3, v3, *([cache_t] * step_pages))
    return out.reshape(nb * seq, ATTN_DIM), bsum


PAGES_PER_STEP = 16
PAGES_PER_BLOCK = MOBA_BLOCK // PAGE_SIZE


def _page_sum_rows(page_refs, o_ref, first_block, lo):
    lane = lax.broadcasted_iota(jnp.int32, (1, LANES), 1)
    rows = slice(lo, lo + LANES)
    acc = o_ref[rows, :]
    for t in range(len(page_refs) // PAGES_PER_BLOCK):
        pages = page_refs[t * PAGES_PER_BLOCK][rows, :]
        for u in range(1, PAGES_PER_BLOCK):
            pages = pages + page_refs[t * PAGES_PER_BLOCK + u][rows, :]
        acc = jnp.where(lane == first_block + t, jnp.sum(pages, axis=-1, keepdims=True), acc)
    o_ref[rows, :] = acc


def _page_sums_kernel(pt_ref, *refs):
    o_ref = refs[-1]
    i = pl.program_id(1)

    @pl.when(i == 0)
    def _():
        o_ref[...] = jnp.zeros(o_ref.shape, F32)

    for lo in range(0, ATTN_DIM, LANES):
        _page_sum_rows(refs[:-1], o_ref, i * (PAGES_PER_STEP // PAGES_PER_BLOCK), lo)


def _page_specs(index_of, n=PAGES_PER_STEP):
    return [pl.BlockSpec((None, ATTN_DIM, PAGE_SIZE), functools.partial(
        lambda *a, r: (index_of(*a, r), 0, 0), r=r)) for r in range(n)]


def _page_sums(cache_t, page_table):
    nb, n_pages = page_table.shape
    assert n_pages % PAGES_PER_STEP == 0 and n_pages // PAGES_PER_BLOCK <= LANES
    specs = _page_specs(lambda b, i, pt, r: pt[b, i * PAGES_PER_STEP + r])
    return pl.pallas_call(
        _page_sums_kernel,
        out_shape=jax.ShapeDtypeStruct((nb, ATTN_DIM, LANES), F32),
        grid_spec=pltpu.PrefetchScalarGridSpec(
            num_scalar_prefetch=1, grid=(nb, n_pages // PAGES_PER_STEP), in_specs=specs,
            out_specs=pl.BlockSpec((None, ATTN_DIM, LANES), lambda b, i, pt: (b, 0, 0))),
        compiler_params=_params(2),
        name="page_sums",
    )(page_table, *([cache_t] * PAGES_PER_STEP))


def _sample_select_kernel(bsum_ref, q_ref, knew_ref, sel_ref, *, n_past):
    seqs = range(bsum_ref.shape[0])
    head = lax.broadcasted_iota(jnp.int32, (N_ATTN_HEADS, ATTN_DIM), 0)
    chan_head = lax.broadcasted_iota(jnp.int32, (N_ATTN_HEADS, ATTN_DIM), 1) // ATTN_HEAD_DIM
    q_bds = [jnp.where(head == chan_head, q_ref[i], 0.0) for i in seqs]
    scores = [_dot_hp(q_bds[i], bsum_ref[i] * (1.0 / MOBA_BLOCK)) for i in seqs]
    lane = lax.broadcasted_iota(jnp.int32, (N_ATTN_HEADS, LANES), 1)
    lane_f = lane.astype(F32)
    own = n_past
    removed = -jnp.inf
    for i in seqs:
        s_new = jnp.sum(q_bds[i] * (knew_ref[i] * (1.0 / MOBA_BLOCK)), axis=-1, keepdims=True)
        s = jnp.where(lane == n_past, s_new, scores[i])
        s = jnp.where(lane < own, s, NEG_INF)
        s = jnp.where(lane <= n_past, s, removed)
        out = jnp.zeros(s.shape, jnp.int32)
        for r in range(MOBA_TOPK):
            mx = jnp.max(s, axis=-1, keepdims=True)
            idx = jnp.min(jnp.where(s == mx, lane_f, float(LANES)), axis=-1, keepdims=True)
            out = jnp.where(lane == r, idx.astype(jnp.int32), out)
            s = jnp.where(lane_f == idx, removed, s)
        sel_ref[i] = out


SELECT_SEQS_PER_STEP = 4


def _sample_select(bsum, q, k_new, n_past):
    nb = bsum.shape[0]
    assert n_past < LANES
    ns = SELECT_SEQS_PER_STEP if nb % SELECT_SEQS_PER_STEP == 0 else 1
    per_b = lambda *s: pl.BlockSpec((ns,) + s, lambda b: (b,) + (0,) * len(s))
    return pl.pallas_call(
        functools.partial(_sample_select_kernel, n_past=n_past),
        out_shape=jax.ShapeDtypeStruct((nb, N_ATTN_HEADS, LANES), jnp.int32),
        grid=(nb // ns,),
        in_specs=[per_b(ATTN_DIM, LANES), per_b(1, ATTN_DIM), per_b(1, ATTN_DIM)],
        out_specs=per_b(N_ATTN_HEADS, LANES),
        compiler_params=_params(1),
        name="moba_sample_select",
    )(bsum, q.reshape(nb, 1, ATTN_DIM), k_new.reshape(nb, 1, ATTN_DIM))


N_SEL_PAGES = MOBA_TOPK * PAGES_PER_BLOCK
ATTEND_HEADS = 8


def _sample_attend_kernel(pg_ref, ok_ref, q_ref, knew_ref, vnew_ref, *refs):
    o_ref = refs[-1]
    k_refs = refs[:ATTEND_HEADS * N_SEL_PAGES]
    v_refs = refs[ATTEND_HEADS * N_SEL_PAGES:2 * ATTEND_HEADS * N_SEL_PAGES]
    step = pl.program_id(0) * pl.num_programs(1) + pl.program_id(1)
    heads = range(ATTEND_HEADS)
    chans = [slice(hh * ATTN_HEAD_DIM, (hh + 1) * ATTN_HEAD_DIM) for hh in heads]
    qhs = [q_ref[:, chans[hh]] * ATTN_SCALE for hh in heads]
    q8s = [jnp.broadcast_to(qh, (SUBLANES, ATTN_HEAD_DIM)).astype(BF16) for qh in qhs]
    raw = [[_dot(q8s[hh], k_refs[hh * N_SEL_PAGES + r][...].astype(BF16))[0:1, :]
            for r in range(N_SEL_PAGES)] for hh in heads]
    ps, ls, p_news = [], [], []
    for hh in heads:
        oks = [ok_ref[(step * ATTEND_HEADS + hh) * MOBA_TOPK + t] for t in range(MOBA_TOPK)]
        lgs = [jnp.where(oks[r // PAGES_PER_BLOCK] > 0, raw[hh][r], NEG_INF) for r in range(N_SEL_PAGES)]
        lg_new = jnp.sum(qhs[hh] * knew_ref[:, chans[hh]], axis=-1, keepdims=True)
        m = lg_new
        for lg in lgs:
            m = jnp.maximum(m, jnp.max(lg, axis=-1, keepdims=True))
        p_new = jnp.exp(lg_new - m)
        pp = [jnp.exp(lg - m) for lg in lgs]
        l = p_new
        for p in pp:
            l = l + jnp.sum(p, axis=-1, keepdims=True)
        ps.append([jnp.broadcast_to(p, (SUBLANES, PAGE_SIZE)).astype(BF16) for p in pp])
        ls.append(l)
        p_news.append(p_new)
    pvs = [[_dot_nt(ps[hh][r], v_refs[hh * N_SEL_PAGES + r][...].astype(BF16))[0:1, :]
            for r in range(N_SEL_PAGES)] for hh in heads]
    outs = []
    for hh in heads:
        acc = p_news[hh] * vnew_ref[:, chans[hh]]
        for pv in pvs[hh]:
            acc = acc + pv
        outs.append(acc / ls[hh])
    o_ref[...] = jnp.concatenate(outs, axis=-1).astype(o_ref.dtype)


def _sample_attend(q, k_new, v_new, cache_k, cache_v, pages, ok):
    nb = q.shape[0]
    n_groups = N_ATTN_HEADS // ATTEND_HEADS
    row = pl.BlockSpec((None, 1, ATTEND_HEADS * ATTN_HEAD_DIM), lambda b, hg, pg, ok: (b, 0, hg))
    per_step = ATTEND_HEADS * N_SEL_PAGES

    def page_spec(i):
        return pl.BlockSpec((None, ATTN_HEAD_DIM, PAGE_SIZE),
                            lambda b, hg, pg, ok: (pg[(b * n_groups + hg) * per_step + i], 0, 0))

    page_specs = [page_spec(i) for i in range(per_step)]
    out = pl.pallas_call(
        _sample_attend_kernel,
        out_shape=jax.ShapeDtypeStruct((nb, 1, ATTN_DIM), BF16),
        grid_spec=pltpu.PrefetchScalarGridSpec(
            num_scalar_prefetch=2, grid=(nb, n_groups),
            in_specs=[row, row, row] + page_specs + page_specs, out_specs=row),
        compiler_params=_params(2),
        name="moba_sample_attend",
    )(pages, ok, q.reshape(nb, 1, ATTN_DIM), k_new.reshape(nb, 1, ATTN_DIM), v_new.reshape(nb, 1, ATTN_DIM),
      *([cache_k] * len(page_specs)), *([cache_v] * len(page_specs)))
    return out.reshape(nb, ATTN_DIM)


def _channel_major_pool(cache):
    return jnp.transpose(cache, (0, 1, 3, 4, 2)).reshape(-1, N_ATTN_HEADS, ATTN_HEAD_DIM, PAGE_SIZE)


def _moba_sample(q, k_new, v_new, ck, cv, page_table, bsum):
    nb, n_pages = page_table.shape
    n_past = n_pages * PAGE_SIZE // MOBA_BLOCK
    assert n_pages * PAGE_SIZE == n_past * MOBA_BLOCK
    sel = _sample_select(bsum, q, k_new, n_past)[:, :, :MOBA_TOPK]
    ck = ck.reshape(-1, ATTN_HEAD_DIM, PAGE_SIZE)
    cv = cv.reshape(-1, ATTN_HEAD_DIM, PAGE_SIZE)
    ok = (sel < n_past).astype(jnp.int32)
    blk = jnp.minimum(sel, n_past - 1)
    logical = blk[..., None] * PAGES_PER_BLOCK + jnp.arange(PAGES_PER_BLOCK)
    pages = jnp.take_along_axis(page_table, logical.reshape(nb, -1), axis=1)
    slabs = pages.reshape(nb, N_ATTN_HEADS, N_SEL_PAGES) * N_ATTN_HEADS + jnp.arange(N_ATTN_HEADS)[None, :, None]
    return _sample_attend(q, k_new, v_new, ck, cv, slabs.reshape(-1), ok.reshape(-1))


def _merge_ln_kernel(x_ref, g_ref, ys_ref, ya_ref, ga_ref, gb_ref, ws_ref, wa_ref, wo_ref, lng_ref, lnb_ref, o_ref):
    merged = (jax.nn.sigmoid(ga_ref[...]) * _dot(ys_ref[...], ws_ref[...])
              + jax.nn.sigmoid(gb_ref[...]) * _dot(ya_ref[...], wa_ref[...]))
    mix = _dot(merged.astype(BF16), wo_ref[...])
    y = DEEPNORM_ALPHA * x_ref[...] + g_ref[...] * mix
    o_ref[...] = _layer_norm(y, lng_ref[...], lnb_ref[...])


def _merge_ln(x, mods, k, y_ssm, y_attn, ga, gb, ws, wa, wo, lng, lnb, tm, rows_per_batch):
    n, d = x.shape
    rows = lambda w: pl.BlockSpec((tm, w), lambda i: (i, 0))
    return pl.pallas_call(
        _merge_ln_kernel,
        out_shape=jax.ShapeDtypeStruct((n, d), F32),
        grid=(n // tm,),
        in_specs=[rows(d)] + _mod_specs(mods, (k,), tm, rows_per_batch)
        + [rows(D_INNER), rows(ATTN_DIM), rows(d), rows(d),
           _resident(ws.shape), _resident(wa.shape), _resident(wo.shape), _resident((1, d)), _resident((1, d))],
        out_specs=rows(d),
        compiler_params=_params(1),
        name="merge_ln",
    )(x, mods, y_ssm, y_attn, ga, gb, ws, wa, wo, lng.reshape(1, d), lnb.reshape(1, d))


PROMPT_TM = 512
IN_PROJ_TM = 256


def _split_w_in(w_in):
    edges = (0, D_INNER, D_INNER + CONV_DIM, D_INNER + CONV_DIM + N_SSM_HEADS)
    edges = edges + tuple(edges[-1] + i * ATTN_DIM for i in range(1, 6))
    parts = [w_in[:, a:b] for a, b in zip(edges[:-1], edges[1:])]
    parts[2] = jnp.pad(parts[2], ((0, 0), (0, LANES - N_SSM_HEADS)))
    return [p.astype(BF16) for p in parts]


def kernel(x_prompt, x_sample, cache_k, cache_v, state_conv, state_ssm, page_table, c_prompt, c_sample, w_ada, b_ada, ln_g, ln_b, w_ffn1_gu, w_ffn1_down, w_ffn2_gu, w_ffn2_down, w_in, conv_w, conv_b, dt_bias, a_log, d_skip, ssm_norm_g, w_branch_ssm, w_branch_attn, w_out):
    assert w_in.shape[0] == DEPTH == 1
    nb, seq, d = x_prompt.shape
    nbs, dec_seq, _ = x_sample.shape
    assert dec_seq == 1
    l = 0
    w1gu, w1d = w_ffn1_gu[l].astype(BF16), w_ffn1_down[l].astype(BF16)
    w2gu, w2d = w_ffn2_gu[l].astype(BF16), w_ffn2_down[l].astype(BF16)
    w_parts = _split_w_in(w_in[l])
    ws, wa, wo = w_branch_ssm[l].astype(BF16), w_branch_attn[l].astype(BF16), w_out[l].astype(BF16)
    ssd_w = (conv_w[l], conv_b[l], dt_bias[l], a_log[l], d_skip[l], ssm_norm_g[l])

    mods = _ada_mods(jnp.concatenate([c_prompt, c_sample], axis=0), w_ada[l], b_ada[l])
    mods_p = mods[:nb].reshape(nb * 9, 1, d)
    mods_s = mods[nb:]

    xp = x_prompt.reshape(nb * seq, d)
    xp = _ffn_ln(xp, mods_p, (0, 1, 2), w1gu, w1d, ln_g[l, 0], ln_b[l, 0], PROMPT_TM, seq)
    kv_t = (False, False, False, False, True, True, False, False)
    w_parts_p = [w.T if t else w for w, t in zip(w_parts, kv_t)]
    z, xc, dt, q, k_t, v_t, ga, gb, conv_p = _in_proj(xp, mods_p, (3, 4), w_parts_p, kv_t, IN_PROJ_TM, seq,
                                                      conv=(1, conv_w[l], conv_b[l]))
    y_ssm, ssm_p = _ssd_prompt(xc, z, dt, nb, seq, *ssd_w[2:])
    ck, cv = _channel_major_pool(cache_k), _channel_major_pool(cache_v)
    ck_pages = ck.reshape(-1, ATTN_DIM, PAGE_SIZE)
    n_pages = page_table.shape[1]
    moba_steps = nb * (ATTN_DIM // MOBA_STEP_CHANNELS) * (seq // MOBA_BLOCK)
    if n_pages % PAGES_PER_STEP == 0 and nbs * (n_pages // PAGES_PER_STEP) == moba_steps:
        y_attn, bsum = _moba_prompt(q, k_t, v_t, nb, seq, paged=(ck_pages, page_table, PAGES_PER_STEP))
    else:
        y_attn = _moba_prompt(q, k_t, v_t, nb, seq)
        bsum = _page_sums(ck_pages, page_table)
    xp = _merge_ln(xp, mods_p, 5, y_ssm, y_attn, ga, gb, ws, wa, wo, ln_g[l, 1], ln_b[l, 1], PROMPT_TM, seq)
    xp = _ffn_ln(xp, mods_p, (6, 7, 8), w2gu, w2d, ln_g[l, 2], ln_b[l, 2], PROMPT_TM, seq)

    xs = x_sample.reshape(nbs, d)
    xs = _ffn_ln(xs, mods_s, (0, 1, 2), w1gu, w1d, ln_g[l, 0], ln_b[l, 0], nbs, 1)
    zs, xbcs, dts, qs, ks, vs, gas, gbs = _in_proj(xs, mods_s, (3, 4), w_parts, (False,) * len(w_parts), nbs, 1)
    y_ssm_s, conv_s, ssm_s = _ssd_sample(xbcs, zs, dts, state_conv[l], state_ssm[l], *ssd_w)
    y_attn_s = _moba_sample(qs, ks, vs, ck, cv, page_table, bsum)
    xs = _merge_ln(xs, mods_s, 5, y_ssm_s.reshape(nbs, D_INNER), y_attn_s, gas, gbs, ws, wa, wo,
                   ln_g[l, 1], ln_b[l, 1], nbs, 1)
    xs = _ffn_ln(xs, mods_s, (6, 7, 8), w2gu, w2d, ln_g[l, 2], ln_b[l, 2], nbs, 1)

    heads = (N_ATTN_HEADS, ATTN_HEAD_DIM)
    state = (N_SSM_HEADS, SSM_HEAD_DIM, SSM_STATE)
    to_rows = lambda a_t: jnp.transpose(a_t.reshape((1, nb) + heads + (seq,)), (0, 1, 4, 2, 3))
    return (xp.reshape(nb, seq, d), xs.reshape(nbs, 1, d), to_rows(k_t), to_rows(v_t),
            conv_p[None], ssm_p.reshape((1, nb) + state),
            ks.reshape((1, nbs, 1) + heads), vs.reshape((1, nbs, 1) + heads),
            conv_s[None], ssm_s.reshape((1, nbs) + state))
```

```python
import functools

import jax
import jax.numpy as jnp
from jax import lax
from jax.experimental import pallas as pl
from jax.experimental.pallas import tpu as pltpu

F32 = jnp.float32
BF16 = jnp.bfloat16

D_MODEL = 1024
D_INNER = 2048
SSM_HEAD_DIM = 64
N_SSM_HEADS = 32
N_SSM_GROUPS = 4
SSM_STATE = 128
CONV_WIDTH = 4
CONV_DIM = D_INNER + 2 * N_SSM_GROUPS * SSM_STATE
SSD_CHUNK = 128
ATTN_HEAD_DIM = 64
N_ATTN_HEADS = 16
ATTN_DIM = 1024
MOBA_BLOCK = 256
MOBA_TOPK = 3
PAGE_SIZE = 128
D_FF = 2816
DEPTH = 1
DEEPNORM_ALPHA = (2.0 * DEPTH) ** 0.25
LN_EPS = 1e-5
RMS_EPS = 1e-5
NEG_INF = -1e30
ATTN_SCALE = ATTN_HEAD_DIM ** -0.5

LANES = 128
SUBLANES = 8
VMEM_LIMIT = 56 * 1024 * 1024


def _dot(a, b):
    return jnp.dot(a, b, preferred_element_type=F32)


def _dot_nt(a, b):
    return lax.dot_general(a, b, (((1,), (1,)), ((), ())), preferred_element_type=F32)


def _split2(a):
    hi = a.astype(BF16)
    lo = (a - hi.astype(F32)).astype(BF16)
    return hi, lo


def _split3(a):
    hi = a.astype(BF16)
    r = a - hi.astype(F32)
    mid = r.astype(BF16)
    lo = (r - mid.astype(F32)).astype(BF16)
    return hi, mid, lo


def _dot_sel(a, sel_bf16):
    hi, mid, lo = _split3(a)
    return _dot(hi, sel_bf16) + _dot(mid, sel_bf16) + _dot(lo, sel_bf16)


def _dot_sel2(a, sel_bf16):
    hi, lo = _split2(a)
    return _dot(hi, sel_bf16) + _dot(lo, sel_bf16)


def _dot_hp(a, b):
    ah, al = _split2(a)
    bh, bl = _split2(b)
    return _dot(ah, bh) + _dot(al, bh) + _dot(ah, bl)


def _dot_nt_hp(a, b):
    ah, al = _split2(a)
    bh, bl = _split2(b)
    return _dot_nt(ah, bh) + _dot_nt(al, bh) + _dot_nt(ah, bl)


def _silu(x):
    half = 0.5 * x
    return half + half * jnp.tanh(half)


def _softplus(x):
    return jnp.maximum(x, 0.0) + jnp.log1p(jnp.exp(-jnp.abs(x)))


def _layer_norm(y, g, b):
    mu = jnp.mean(y, axis=-1, keepdims=True)
    yc = y - mu
    var = jnp.mean(yc * yc, axis=-1, keepdims=True)
    return yc * lax.rsqrt(var + LN_EPS) * g + b


def _resident(shape):
    nd = len(shape)
    return pl.BlockSpec(shape, lambda *_: (0,) * nd, pipeline_mode=pl.Buffered(1))


def _params(n_axes):
    return pltpu.CompilerParams(dimension_semantics=("arbitrary",) * n_axes, vmem_limit_bytes=VMEM_LIMIT)


def _ada_kernel(c_ref, w_ref, b_ref, o_ref):
    s = _silu(c_ref[...])
    o_ref[...] = _dot_hp(s, w_ref[...]) + b_ref[...]


def _ada_mods(c, w_ada, b_ada):
    m, d = c.shape
    n = w_ada.shape[1]
    tn = 1024
    return pl.pallas_call(
        _ada_kernel,
        out_shape=jax.ShapeDtypeStruct((m, n), F32),
        grid=(n // tn,),
        in_specs=[pl.BlockSpec((m, d), lambda j: (0, 0)),
                  pl.BlockSpec((d, tn), lambda j: (0, j)),
                  pl.BlockSpec((1, tn), lambda j: (0, j))],
        out_specs=pl.BlockSpec((m, tn), lambda j: (0, j)),
        compiler_params=_params(1),
        name="ada_mods",
    )(c, w_ada, b_ada.reshape(1, n))


def _mod_specs(mods, ks, tm, rows_per_batch):
    if mods.ndim == 3:
        tiles_per_batch = rows_per_batch // tm
        return [pl.BlockSpec((None, 1, D_MODEL), functools.partial(
            lambda i, k: ((i // tiles_per_batch) * 9 + k, 0, 0), k=k)) for k in ks]
    return [pl.BlockSpec((tm, D_MODEL), functools.partial(lambda i, k: (i, k), k=k)) for k in ks]


MXU_DIM = 256
FF_CHUNK = 6 * MXU_DIM
FF_EDGES = tuple(range(0, D_FF, FF_CHUNK)) + (D_FF,)


def _ffn_ln_kernel(x_ref, sh_ref, sc_ref, g_ref, wgu_ref, wd_ref, lng_ref, lnb_ref, o_ref):
    x = x_ref[...]
    h = (x * (1.0 + sc_ref[...]) + sh_ref[...]).astype(BF16)
    acc = jnp.zeros(x.shape, F32)
    for lo, hi in zip(FF_EDGES[:-1], FF_EDGES[1:]):
        gate = _dot(h, wgu_ref[:, lo:hi])
        up = _dot(h, wgu_ref[:, D_FF + lo:D_FF + hi])
        act = (_silu(gate) * up).astype(BF16)
        acc = acc + _dot(act, wd_ref[lo:hi, :])
    y = DEEPNORM_ALPHA * x + 0.5 * g_ref[...] * acc
    o_ref[...] = _layer_norm(y, lng_ref[...], lnb_ref[...])


def _ffn_ln(x, mods, ks, wgu, wd, lng, lnb, tm, rows_per_batch):
    n, d = x.shape
    row = pl.BlockSpec((tm, d), lambda i: (i, 0))
    return pl.pallas_call(
        _ffn_ln_kernel,
        out_shape=jax.ShapeDtypeStruct((n, d), F32),
        grid=(n // tm,),
        in_specs=[row] + _mod_specs(mods, ks, tm, rows_per_batch)
        + [_resident(wgu.shape), _resident(wd.shape), _resident((1, d)), _resident((1, d))],
        out_specs=row,
        compiler_params=_params(1),
        name="ffn_ln",
    )(x, mods, mods, mods, wgu, wd, lng.reshape(1, d), lnb.reshape(1, d))


CARRY = SUBLANES
PROJ_PIECE = 512
CONV_CHUNKS_PER_PIECE = 2


def _in_proj_kernel(x_ref, sh_ref, sc_ref, *refs, transposed, conv_index, tiles_per_batch):
    n_out = len(transposed)
    w_refs, o_refs = refs[:n_out], refs[n_out + 2 * (conv_index is not None):]
    h = (x_ref[...] * (1.0 + sc_ref[...]) + sh_ref[...]).astype(BF16)

    def project(idx, lo=None, hi=None):
        if transposed[idx]:
            return _dot_nt(w_refs[idx][lo:hi, :], h)
        return _dot(h, w_refs[idx][:, lo:hi])

    conv_chunks = []
    if conv_index is not None:
        cw_ref, cb_ref = refs[n_out:n_out + 2]
        state_ref, buf_s = refs[-2:]
        tm = x_ref.shape[0]
        tile = pl.program_id(0) % tiles_per_batch

        n_lane_tiles = CONV_DIM // LANES

        @pl.when(tile == 0)
        def _():
            buf_s[:, 0:CARRY, :] = jnp.zeros((n_lane_tiles, CARRY, LANES), F32)

        raw = project(conv_index)
        for c in range(n_lane_tiles):
            buf_s[c, CARRY:CARRY + tm, :] = raw[:, c * LANES:(c + 1) * LANES]

        def conv_chunk(c):
            cols = slice(c * LANES, (c + 1) * LANES)
            acc = cb_ref[:, cols]
            for j in range(CONV_WIDTH):
                off = CARRY - (CONV_WIDTH - 1) + j
                acc = acc + cw_ref[j:j + 1, cols] * buf_s[c, off:off + tm, :]
            o_refs[conv_index][:, cols] = _silu(acc)

        conv_chunks = list(range(n_lane_tiles))

    for idx in range(n_out):
        if idx == conv_index:
            continue
        width = w_refs[idx].shape[0] if transposed[idx] else w_refs[idx].shape[1]
        for lo in range(0, width, PROJ_PIECE):
            hi = min(lo + PROJ_PIECE, width)
            piece = project(idx, lo, hi).astype(o_refs[idx].dtype)
            if transposed[idx]:
                o_refs[idx][lo:hi, :] = piece
            else:
                o_refs[idx][:, lo:hi] = piece
            for chunk in conv_chunks[:CONV_CHUNKS_PER_PIECE]:
                conv_chunk(chunk)
            del conv_chunks[:CONV_CHUNKS_PER_PIECE]
    for chunk in conv_chunks:
        conv_chunk(chunk)

    if conv_index is not None:
        @pl.when(tile == tiles_per_batch - 1)
        def _():
            for c in range(n_lane_tiles):
                state_ref[:, c * LANES:(c + 1) * LANES] = buf_s[c, CARRY + tm - (CONV_WIDTH - 1):CARRY + tm, :]

        buf_s[:, 0:CARRY, :] = buf_s[:, tm:tm + CARRY, :]


def _in_proj(x, mods, ks, weights, transposed, tm, rows_per_batch, conv=None):
    n, d = x.shape
    nb = n // rows_per_batch
    tiles_per_batch = max(rows_per_batch // tm, 1)
    row = pl.BlockSpec((tm, d), lambda i: (i, 0))
    out_shape, out_specs = [], []
    for w, t in zip(weights, transposed):
        if t:
            out_shape.append(jax.ShapeDtypeStruct((nb, w.shape[0], rows_per_batch), F32))
            out_specs.append(pl.BlockSpec((None, w.shape[0], tm),
                                          lambda i: (i // tiles_per_batch, 0, i % tiles_per_batch)))
        else:
            out_shape.append(jax.ShapeDtypeStruct((n, w.shape[1]), F32))
            out_specs.append(pl.BlockSpec((tm, w.shape[1]), lambda i: (i, 0)))
    in_specs = [row] + _mod_specs(mods, ks, tm, rows_per_batch) + [_resident(w.shape) for w in weights]
    args = [x, mods, mods, *weights]
    scratch = []
    if conv is not None:
        conv_index, conv_w, conv_b = conv
        in_specs += [_resident((CONV_WIDTH, CONV_DIM)), _resident((1, CONV_DIM))]
        args += [conv_w, conv_b.reshape(1, CONV_DIM)]
        out_shape.append(jax.ShapeDtypeStruct((nb, CONV_WIDTH - 1, CONV_DIM), F32))
        out_specs.append(pl.BlockSpec((None, CONV_WIDTH - 1, CONV_DIM), lambda i: (i // tiles_per_batch, 0, 0)))
        scratch.append(pltpu.VMEM((CONV_DIM // LANES, CARRY + tm, LANES), F32))
    return pl.pallas_call(
        functools.partial(_in_proj_kernel, transposed=tuple(transposed),
                          conv_index=None if conv is None else conv[0], tiles_per_batch=tiles_per_batch),
        out_shape=out_shape,
        grid=(n // tm,),
        in_specs=in_specs,
        out_specs=out_specs,
        scratch_shapes=scratch,
        compiler_params=_params(1),
        name="in_proj",
    )(*args)


GROUP_W = D_INNER // N_SSM_GROUPS


def _gated_rmsnorm(y, z, norm_g):
    yz = y * _silu(z)
    outs = []
    for g in range(N_SSM_GROUPS):
        blk = yz[:, g * GROUP_W:(g + 1) * GROUP_W]
        ms = jnp.mean(blk * blk, axis=-1, keepdims=True)
        outs.append(blk * lax.rsqrt(ms + RMS_EPS))
    return jnp.concatenate(outs, axis=-1) * norm_g


def _ssd_prompt_kernel(xc_s, z_ref, dt_ref, dtb_ref, alog_ref, dskip_ref, ng_ref, exp_ref, y_ref, ssm_ref, st_s):
    c = pl.program_id(1)
    q = SSD_CHUNK

    @pl.when(c == 0)
    def _():
        st_s[...] = jnp.zeros(st_s.shape, F32)

    expand = exp_ref[...]
    dt = _softplus(dt_ref[...] + dtb_ref[...])
    da = dt * (-jnp.exp(alog_ref[...]))
    row_i = lax.broadcasted_iota(jnp.int32, (q, q), 0)
    col_i = lax.broadcasted_iota(jnp.int32, (q, q), 1)
    causal = row_i >= col_i
    tril = jnp.where(causal, 1.0, 0.0).astype(BF16)
    a_cum = _dot_tril(tril, da)
    a_src = a_cum - jnp.log(dt)
    a_src_t = a_src.T
    a_last = a_cum[q - 1:q, :]
    dec_end_x = _dot_sel2(jnp.exp(a_last - a_src), expand)
    exp_a_x = _dot_sel2(jnp.exp(a_cum), expand)
    chunk_decay_x = _dot_sel(jnp.broadcast_to(jnp.exp(a_last), (SUBLANES, LANES)), expand)[0:1, :]

    xs = xc_s[:, 0:D_INNER]
    xs_b = xs.astype(BF16)
    xdec_b = (xs * dec_end_x).astype(BF16)
    lane = lax.broadcasted_iota(jnp.int32, (1, LANES), 1)
    zero_b = jnp.zeros((), BF16)

    y_parts = []
    for g in range(N_SSM_GROUPS):
        b_g = xc_s[:, D_INNER + g * SSM_STATE:D_INNER + (g + 1) * SSM_STATE]
        c_g = xc_s[:, D_INNER + (N_SSM_GROUPS + g) * SSM_STATE:D_INNER + (N_SSM_GROUPS + g + 1) * SSM_STATE]
        b_gb = b_g.astype(BF16)
        c_gb = c_g.astype(BF16)
        cb = _dot_nt(c_gb, b_gb)
        gcols = slice(g * GROUP_W, (g + 1) * GROUP_W)
        st_old = st_s[:, gcols]
        y_inter = _dot(c_gb, st_old.astype(BF16)) * exp_a_x[:, gcols]
        st_s[:, gcols] = chunk_decay_x[:, gcols] * st_old + _dot(b_g.T.astype(BF16), xdec_b[:, gcols])
        pair_out = []
        heads_per_group = N_SSM_HEADS // N_SSM_GROUPS
        for pr in range(heads_per_group // 2):
            pcols = slice(g * GROUP_W + pr * LANES, g * GROUP_W + (pr + 1) * LANES)
            x_pair = xs_b[:, pcols]
            acc = None
            for hh in range(2):
                h = g * heads_per_group + pr * 2 + hh
                seg = a_cum[:, h:h + 1] - a_src_t[h:h + 1, :]
                w = (cb * jnp.exp(jnp.where(causal, seg, -jnp.inf))).astype(BF16)
                x_h = jnp.where((lane // SSM_HEAD_DIM) == hh, x_pair, zero_b)
                t = _dot(w, x_h)
                acc = t if acc is None else acc + t
            pair_out.append(acc)
        y_parts.append(jnp.concatenate(pair_out, axis=-1) + y_inter)
    y = jnp.concatenate(y_parts, axis=-1) + dskip_ref[...] * xs
    y_ref[...] = _gated_rmsnorm(y, z_ref[...], ng_ref[...]).astype(y_ref.dtype)

    @pl.when(c == pl.num_programs(1) - 1)
    def _():
        for r in range(D_INNER // LANES):
            ssm_ref[r * LANES:(r + 1) * LANES, :] = st_s[:, r * LANES:(r + 1) * LANES].T


def _dot_tril(tril_bf16, a):
    hi, mid, lo = _split3(a)
    return _dot(tril_bf16, hi) + _dot(tril_bf16, mid) + _dot(tril_bf16, lo)


def _pad_lanes(v):
    return jnp.zeros((1, LANES), F32).at[0, :v.shape[0]].set(v)


def _expand_matrix():
    h = jnp.arange(LANES)[:, None]
    ch = jnp.arange(D_INNER)[None, :] // SSM_HEAD_DIM
    return (h == ch).astype(BF16)


def _ssd_prompt(xc, z, dt, nb, seq, dt_bias, a_log, d_skip, norm_g):
    nc = seq // SSD_CHUNK
    rows = lambda w: pl.BlockSpec((SSD_CHUNK, w), lambda b, c: (b * nc + c, 0))
    return pl.pallas_call(
        _ssd_prompt_kernel,
        out_shape=[jax.ShapeDtypeStruct((nb * seq, D_INNER), BF16),
                   jax.ShapeDtypeStruct((nb, D_INNER, SSM_STATE), F32)],
        grid=(nb, nc),
        in_specs=[rows(CONV_DIM), rows(D_INNER), rows(LANES), _resident((1, LANES)),
                  _resident((1, LANES)), _resident((1, D_INNER)), _resident((1, D_INNER)),
                  _resident((LANES, D_INNER))],
        out_specs=[rows(D_INNER), pl.BlockSpec((None, D_INNER, SSM_STATE), lambda b, c: (b, 0, 0))],
        scratch_shapes=[pltpu.VMEM((SSM_STATE, D_INNER), F32)],
        compiler_params=_params(2),
        name="ssd_prompt",
    )(xc, z, dt, _pad_lanes(dt_bias), _pad_lanes(a_log),
      jnp.repeat(d_skip, SSM_HEAD_DIM).reshape(1, D_INNER), norm_g.reshape(1, D_INNER), _expand_matrix())


def _column_block(row_vec, r):
    return jnp.broadcast_to(row_vec[:, r * LANES:(r + 1) * LANES], (LANES, LANES)).T


def _ssd_sample_kernel(xbc_ref, z_ref, dt_ref, buf_ref, h0_ref, cw_ref, cb_ref, dtb_ref, alog_ref, dskip_ref,
                       ng_ref, exp_ref, y_ref, conv_ref, ssm_ref):
    xr = xbc_ref[...]
    acc = cb_ref[...] + cw_ref[CONV_WIDTH - 1:CONV_WIDTH, :] * xr
    for j in range(CONV_WIDTH - 1):
        acc = acc + cw_ref[j:j + 1, :] * buf_ref[j:j + 1, :]
    xc = _silu(acc)
    for j in range(CONV_WIDTH - 2):
        conv_ref[j:j + 1, :] = buf_ref[j + 1:j + 2, :]
    conv_ref[CONV_WIDTH - 2:CONV_WIDTH - 1, :] = xr

    expand = exp_ref[...]
    dt = _softplus(dt_ref[...] + dtb_ref[...])
    dec = jnp.exp(dt * (-jnp.exp(alog_ref[...])))
    both = jnp.concatenate([jnp.broadcast_to(dt, (SUBLANES // 2, LANES)),
                            jnp.broadcast_to(dec, (SUBLANES // 2, LANES))], axis=0)
    both_x = _dot_sel(both, expand)
    dt_x = both_x[0:1, :]
    dec_x = both_x[SUBLANES // 2:SUBLANES // 2 + 1, :]
    xs = xc[:, 0:D_INNER]
    xdt = xs * dt_x

    y_parts = []
    blocks_per_group = GROUP_W // LANES
    for g in range(N_SSM_GROUPS):
        b_g = xc[:, D_INNER + g * SSM_STATE:D_INNER + (g + 1) * SSM_STATE]
        c_g = xc[:, D_INNER + (N_SSM_GROUPS + g) * SSM_STATE:D_INNER + (N_SSM_GROUPS + g + 1) * SSM_STATE]
        c8 = jnp.broadcast_to(c_g, (SUBLANES, SSM_STATE)).astype(BF16)
        for rb in range(blocks_per_group):
            r = g * blocks_per_group + rb
            rows = slice(r * LANES, (r + 1) * LANES)
            h_new = _column_block(dec_x, r) * h0_ref[rows, :] + _column_block(xdt, r) * b_g
            ssm_ref[rows, :] = h_new
            y_parts.append(_dot_nt(c8, h_new.astype(BF16))[0:1, :])
    y = jnp.concatenate(y_parts, axis=-1) + dskip_ref[...] * xs
    y_ref[...] = _gated_rmsnorm(y, z_ref[...], ng_ref[...]).astype(y_ref.dtype)


def _ssd_sample(xbc, z, dt, state_conv, state_ssm, conv_w, conv_b, dt_bias, a_log, d_skip, norm_g):
    nb = xbc.shape[0]
    per_b = lambda *s: pl.BlockSpec((None,) + s, lambda b: (b,) + (0,) * len(s))
    return pl.pallas_call(
        _ssd_sample_kernel,
        out_shape=[jax.ShapeDtypeStruct((nb, 1, D_INNER), BF16),
                   jax.ShapeDtypeStruct((nb, CONV_WIDTH - 1, CONV_DIM), F32),
                   jax.ShapeDtypeStruct((nb, D_INNER, SSM_STATE), F32)],
        grid=(nb,),
        in_specs=[per_b(1, CONV_DIM), per_b(1, D_INNER), per_b(1, LANES),
                  per_b(CONV_WIDTH - 1, CONV_DIM), per_b(D_INNER, SSM_STATE),
                  _resident((CONV_WIDTH, CONV_DIM)), _resident((1, CONV_DIM)), _resident((1, LANES)),
                  _resident((1, LANES)), _resident((1, D_INNER)), _resident((1, D_INNER)),
                  _resident((LANES, D_INNER))],
        out_specs=[per_b(1, D_INNER), per_b(CONV_WIDTH - 1, CONV_DIM), per_b(D_INNER, SSM_STATE)],
        compiler_params=_params(1),
        name="ssd_sample",
    )(xbc.reshape(nb, 1, CONV_DIM), z.reshape(nb, 1, D_INNER), dt.reshape(nb, 1, LANES), state_conv,
      state_ssm.reshape(nb, D_INNER, SSM_STATE), conv_w, conv_b.reshape(1, CONV_DIM), _pad_lanes(dt_bias),
      _pad_lanes(a_log), jnp.repeat(d_skip, SSM_HEAD_DIM).reshape(1, D_INNER), norm_g.reshape(1, D_INNER),
      _expand_matrix())


HEADS_PER_VREG = LANES // ATTN_HEAD_DIM
MOBA_PAIRS_PER_STEP = 2
MOBA_HEADS_PER_STEP = MOBA_PAIRS_PER_STEP * HEADS_PER_VREG
MOBA_STEP_CHANNELS = MOBA_PAIRS_PER_STEP * LANES


def _moba_prompt_kernel(*refs, page_steps, step_pages):
    page_tasks = []
    if page_steps is not None:
        page_refs, bsum_ref = refs[4:4 + step_pages], refs[5 + step_pages]
        q_ref, k_ref, v_ref, o_ref = refs[1:4] + (refs[4 + step_pages],)
        kb_s, vt_s, means_s, sel_s, qb_s, m_s, l_s, acc_s = refs[6 + step_pages:]
        step = ((pl.program_id(0) * pl.num_programs(1) + pl.program_id(1)) * pl.num_programs(2)
                + pl.program_id(2))
        page_step = step % page_steps
        first_block = page_step * (step_pages // PAGES_PER_BLOCK)

        @pl.when(page_step == 0)
        def _():
            bsum_ref[...] = jnp.zeros(bsum_ref.shape, F32)

        page_tasks = list(range(0, ATTN_DIM, LANES))
    else:
        q_ref, k_ref, v_ref, o_ref, kb_s, vt_s, means_s, sel_s, qb_s, m_s, l_s, acc_s = refs

    def run_page_tasks(n):
        for lo in page_tasks[:n]:
            _page_sum_rows(page_refs, bsum_ref, first_block, lo)
        del page_tasks[:n]

    tq = MOBA_BLOCK
    nblk = kb_s.shape[0]
    own = pl.program_id(2)
    pair_cols = [slice(pr * LANES, (pr + 1) * LANES) for pr in range(MOBA_PAIRS_PER_STEP)]
    head_rows = [slice(h * ATTN_HEAD_DIM, (h + 1) * ATTN_HEAD_DIM) for h in range(MOBA_HEADS_PER_STEP)]

    @pl.when(own == 0)
    def _():
        for j in range(nblk):
            for pr in range(MOBA_PAIRS_PER_STEP):
                ksum = jnp.zeros((1, LANES), F32)
                for t in range(MOBA_BLOCK // LANES):
                    cols = slice(j * MOBA_BLOCK + t * LANES, j * MOBA_BLOCK + (t + 1) * LANES)
                    kblk = k_ref[pair_cols[pr], cols].T
                    kb_s[j, t * LANES:(t + 1) * LANES, pair_cols[pr]] = kblk.astype(BF16)
                    ksum = ksum + jnp.sum(kblk, axis=0, keepdims=True)
                means_s[j:j + 1, pair_cols[pr]] = ksum * (1.0 / MOBA_BLOCK)
            vt_s[j] = v_ref[:, j * MOBA_BLOCK:(j + 1) * MOBA_BLOCK].astype(BF16)

    lane = lax.broadcasted_iota(jnp.int32, (1, LANES), 1)
    blk_i = lax.broadcasted_iota(jnp.int32, (nblk, tq), 0)
    causal = (lax.broadcasted_iota(jnp.int32, (MOBA_BLOCK, tq), 0)
              <= lax.broadcasted_iota(jnp.int32, (MOBA_BLOCK, tq), 1))
    heads = range(MOBA_HEADS_PER_STEP)
    cols_of = [pair_cols[h // HEADS_PER_VREG] for h in heads]
    qms = [jnp.where((lane // ATTN_HEAD_DIM) == h % HEADS_PER_VREG, q_ref[:, cols_of[h]], 0.0) for h in heads]
    qbs = [(qm * ATTN_SCALE).astype(BF16) for qm in qms]
    for h in heads:
        qb_s[h] = qbs[h]
    raw_own = [_dot_nt(kb_s[own, :, cols_of[h]], qbs[h]) for h in heads]
    run_page_tasks(2)
    scores = [_dot_nt_hp(means_s[:, cols_of[h]], qms[h]) for h in heads]
    run_page_tasks(2)
    for h in heads:
        s_t = jnp.where(blk_i < own, scores[h], NEG_INF)
        for j in range(nblk):
            sj = s_t[j:j + 1, :]
            beats = jnp.where(s_t > sj, 1.0, 0.0) + jnp.where((s_t == sj) & (blk_i < j), 1.0, 0.0)
            rank = jnp.sum(beats, axis=0, keepdims=True)
            sel_s[h * nblk + j] = jnp.where(rank < float(MOBA_TOPK), 1.0, 0.0) * jnp.where(j < own, 1.0, 0.0)
    ps = []
    for h in heads:
        lg = jnp.where(causal, raw_own[h], NEG_INF)
        m = jnp.max(lg, axis=0, keepdims=True)
        p = jnp.exp(lg - m)
        m_s[h] = m
        l_s[h] = jnp.sum(p, axis=0, keepdims=True)
        ps.append(p.astype(BF16))
    for h in heads:
        acc_s[h] = _dot(vt_s[own, head_rows[h], :], ps[h])

    def past_blocks(js, between=lambda: None):
        raw = [[_dot_nt(kb_s[j, :, cols_of[h]], qb_s[h]) for h in heads] for j in js]
        between()
        ps, alphas = [], []
        for h in heads:
            lgs = [jnp.where(sel_s[h * nblk + j] > 0.5, raw[i][h], NEG_INF) for i, j in enumerate(js)]
            m_old = m_s[h]
            m_new = m_old
            for lg in lgs:
                m_new = jnp.maximum(m_new, jnp.max(lg, axis=0, keepdims=True))
            alpha = jnp.exp(m_old - m_new)
            pp = [jnp.exp(lg - m_new) for lg in lgs]
            l = alpha * l_s[h]
            for p in pp:
                l = l + jnp.sum(p, axis=0, keepdims=True)
            m_s[h] = m_new
            l_s[h] = l
            ps.append([p.astype(BF16) for p in pp])
            alphas.append(alpha)
        pvs = [[_dot(vt_s[j, head_rows[h], :], ps[h][i]) for i, j in enumerate(js)] for h in heads]
        between()
        for h in heads:
            acc = alphas[h] * acc_s[h]
            for pv in pvs[h]:
                acc = acc + pv
            acc_s[h] = acc

    def pair_body(jj, carry):
        past_blocks([2 * jj, 2 * jj + 1])
        return carry

    if page_tasks:
        half = len(page_tasks) // 2

        @pl.when(own >= 2)
        def _():
            past_blocks([0, 1], between=lambda: run_page_tasks(half))

        page_tasks.extend(range(ATTN_DIM - 2 * half * LANES, ATTN_DIM, LANES))

        @pl.when(own < 2)
        def _():
            run_page_tasks(len(page_tasks))

        lax.fori_loop(1, own // 2, pair_body, 0)
    else:
        lax.fori_loop(0, own // 2, pair_body, 0)

    @pl.when(own % 2 == 1)
    def _():
        past_blocks([own - 1])
    for pr in range(MOBA_PAIRS_PER_STEP):
        heads = range(pr * HEADS_PER_VREG, (pr + 1) * HEADS_PER_VREG)
        o_t = jnp.concatenate([acc_s[h] / l_s[h] for h in heads], axis=0)
        o_ref[:, pair_cols[pr]] = o_t.T.astype(o_ref.dtype)


def _moba_prompt(q, k_t, v_t, nb, seq, paged=None):
    assert seq % MOBA_BLOCK == 0
    nblk = seq // MOBA_BLOCK
    gw = MOBA_STEP_CHANNELS
    n_groups = ATTN_DIM // gw
    q3, k3, v3 = q.reshape(nb, seq, ATTN_DIM), k_t, v_t
    kv_spec = pl.BlockSpec((None, gw, seq), lambda b, g, t, *_: (b, g, 0))
    q_spec = pl.BlockSpec((None, MOBA_BLOCK, gw), lambda b, g, t, *_: (b, t, g))
    scratch = [pltpu.VMEM((nblk, MOBA_BLOCK, gw), BF16),
               pltpu.VMEM((nblk, gw, MOBA_BLOCK), BF16),
               pltpu.VMEM((nblk, gw), F32),
               pltpu.VMEM((MOBA_HEADS_PER_STEP * nblk, 1, MOBA_BLOCK), F32),
               pltpu.VMEM((MOBA_HEADS_PER_STEP, MOBA_BLOCK, LANES), BF16),
               pltpu.VMEM((MOBA_HEADS_PER_STEP, 1, MOBA_BLOCK), F32),
               pltpu.VMEM((MOBA_HEADS_PER_STEP, 1, MOBA_BLOCK), F32),
               pltpu.VMEM((MOBA_HEADS_PER_STEP, ATTN_HEAD_DIM, MOBA_BLOCK), F32)]
    grid = (nb, n_groups, nblk)
    out_attn = jax.ShapeDtypeStruct((nb, seq, ATTN_DIM), BF16)
    if paged is None:
        out = pl.pallas_call(
            functools.partial(_moba_prompt_kernel, page_steps=None, step_pages=0),
            out_shape=out_attn, grid=grid, in_specs=[q_spec, kv_spec, kv_spec], out_specs=q_spec,
            scratch_shapes=scratch, compiler_params=_params(3), name="moba_prompt",
        )(q3, k3, v3)
        return out.reshape(nb * seq, ATTN_DIM)
    cache_t, page_table, step_pages = paged
    nbs, n_pages = page_table.shape
    page_steps = nb * n_groups * nblk // nbs
    assert nbs * page_steps == nb * n_groups * nblk and page_steps * step_pages == n_pages
    assert step_pages % PAGES_PER_BLOCK == 0 and n_pages // PAGES_PER_BLOCK <= LANES

    def step_of(b, g, t):
        return (b * n_groups + g) * nblk + t

    page_specs = _page_specs(lambda b, g, t, pt, r: pt[step_of(b, g, t) * step_pages + r], step_pages)
    out, bsum = pl.pallas_call(
        functools.partial(_moba_prompt_kernel, page_steps=page_steps, step_pages=step_pages),
        out_shape=[out_attn, jax.ShapeDtypeStruct((nbs, ATTN_DIM, LANES), F32)],
        grid_spec=pltpu.PrefetchScalarGridSpec(
            num_scalar_prefetch=1, grid=grid, in_specs=[q_spec, kv_spec, kv_spec] + page_specs,
            out_specs=[q_spec, pl.BlockSpec((None, ATTN_DIM, LANES),
                                            lambda b, g, t, pt: (step_of(b, g, t) // page_steps, 0, 0))],
            scratch_shapes=scratch),
        compiler_params=_params(3),
        name="moba_prompt_pages",
    )(page_table.reshape(-1), q3, k3, v3, *([cache_t] * step_pages))
    return out.reshape(nb * seq, ATTN_DIM), bsum


PAGES_PER_STEP = 16
PAGES_PER_BLOCK = MOBA_BLOCK // PAGE_SIZE


def _page_sum_rows(page_refs, o_ref, first_block, lo):
    lane = lax.broadcasted_iota(jnp.int32, (1, LANES), 1)
    rows = slice(lo, lo + LANES)
    acc = o_ref[rows, :]
    for t in range(len(page_refs) // PAGES_PER_BLOCK):
        pages = page_refs[t * PAGES_PER_BLOCK][rows, :]
        for u in range(1, PAGES_PER_BLOCK):
            pages = pages + page_refs[t * PAGES_PER_BLOCK + u][rows, :]
        acc = jnp.where(lane == first_block + t, jnp.sum(pages, axis=-1, keepdims=True), acc)
    o_ref[rows, :] = acc


def _page_sums_kernel(pt_ref, *refs):
    o_ref = refs[-1]
    i = pl.program_id(1)

    @pl.when(i == 0)
    def _():
        o_ref[...] = jnp.zeros(o_ref.shape, F32)

    for lo in range(0, ATTN_DIM, LANES):
        _page_sum_rows(refs[:-1], o_ref, i * (PAGES_PER_STEP // PAGES_PER_BLOCK), lo)


def _page_specs(index_of, n=PAGES_PER_STEP):
    return [pl.BlockSpec((None, ATTN_DIM, PAGE_SIZE), functools.partial(
        lambda *a, r: (index_of(*a, r), 0, 0), r=r)) for r in range(n)]


def _page_sums(cache_t, page_table):
    nb, n_pages = page_table.shape
    assert n_pages % PAGES_PER_STEP == 0 and n_pages // PAGES_PER_BLOCK <= LANES
    specs = _page_specs(lambda b, i, pt, r: pt[b, i * PAGES_PER_STEP + r])
    return pl.pallas_call(
        _page_sums_kernel,
        out_shape=jax.ShapeDtypeStruct((nb, ATTN_DIM, LANES), F32),
        grid_spec=pltpu.PrefetchScalarGridSpec(
            num_scalar_prefetch=1, grid=(nb, n_pages // PAGES_PER_STEP), in_specs=specs,
            out_specs=pl.BlockSpec((None, ATTN_DIM, LANES), lambda b, i, pt: (b, 0, 0))),
        compiler_params=_params(2),
        name="page_sums",
    )(page_table, *([cache_t] * PAGES_PER_STEP))


def _sample_select_kernel(bsum_ref, q_ref, knew_ref, sel_ref, *, n_past):
    seqs = range(bsum_ref.shape[0])
    head = lax.broadcasted_iota(jnp.int32, (N_ATTN_HEADS, ATTN_DIM), 0)
    chan_head = lax.broadcasted_iota(jnp.int32, (N_ATTN_HEADS, ATTN_DIM), 1) // ATTN_HEAD_DIM
    q_bds = [jnp.where(head == chan_head, q_ref[i], 0.0) for i in seqs]
    scores = [_dot_hp(q_bds[i], bsum_ref[i] * (1.0 / MOBA_BLOCK)) for i in seqs]
    lane = lax.broadcasted_iota(jnp.int32, (N_ATTN_HEADS, LANES), 1)
    lane_f = lane.astype(F32)
    own = n_past
    removed = -jnp.inf
    for i in seqs:
        s_new = jnp.sum(q_bds[i] * (knew_ref[i] * (1.0 / MOBA_BLOCK)), axis=-1, keepdims=True)
        s = jnp.where(lane == n_past, s_new, scores[i])
        s = jnp.where(lane < own, s, NEG_INF)
        s = jnp.where(lane <= n_past, s, removed)
        out = jnp.zeros(s.shape, jnp.int32)
        for r in range(MOBA_TOPK):
            mx = jnp.max(s, axis=-1, keepdims=True)
            idx = jnp.min(jnp.where(s == mx, lane_f, float(LANES)), axis=-1, keepdims=True)
            out = jnp.where(lane == r, idx.astype(jnp.int32), out)
            s = jnp.where(lane_f == idx, removed, s)
        sel_ref[i] = out


SELECT_SEQS_PER_STEP = 4


def _sample_select(bsum, q, k_new, n_past):
    nb = bsum.shape[0]
    assert n_past < LANES
    ns = SELECT_SEQS_PER_STEP if nb % SELECT_SEQS_PER_STEP == 0 else 1
    per_b = lambda *s: pl.BlockSpec((ns,) + s, lambda b: (b,) + (0,) * len(s))
    return pl.pallas_call(
        functools.partial(_sample_select_kernel, n_past=n_past),
        out_shape=jax.ShapeDtypeStruct((nb, N_ATTN_HEADS, LANES), jnp.int32),
        grid=(nb // ns,),
        in_specs=[per_b(ATTN_DIM, LANES), per_b(1, ATTN_DIM), per_b(1, ATTN_DIM)],
        out_specs=per_b(N_ATTN_HEADS, LANES),
        compiler_params=_params(1),
        name="moba_sample_select",
    )(bsum, q.reshape(nb, 1, ATTN_DIM), k_new.reshape(nb, 1, ATTN_DIM))


N_SEL_PAGES = MOBA_TOPK * PAGES_PER_BLOCK
ATTEND_HEADS = 8


def _sample_attend_kernel(pg_ref, ok_ref, q_ref, knew_ref, vnew_ref, *refs):
    o_ref = refs[-1]
    k_refs = refs[:ATTEND_HEADS * N_SEL_PAGES]
    v_refs = refs[ATTEND_HEADS * N_SEL_PAGES:2 * ATTEND_HEADS * N_SEL_PAGES]
    step = pl.program_id(0) * pl.num_programs(1) + pl.program_id(1)
    heads = range(ATTEND_HEADS)
    chans = [slice(hh * ATTN_HEAD_DIM, (hh + 1) * ATTN_HEAD_DIM) for hh in heads]
    qhs = [q_ref[:, chans[hh]] * ATTN_SCALE for hh in heads]
    q8s = [jnp.broadcast_to(qh, (SUBLANES, ATTN_HEAD_DIM)).astype(BF16) for qh in qhs]
    raw = [[_dot(q8s[hh], k_refs[hh * N_SEL_PAGES + r][...].astype(BF16))[0:1, :]
            for r in range(N_SEL_PAGES)] for hh in heads]
    ps, ls, p_news = [], [], []
    for hh in heads:
        oks = [ok_ref[(step * ATTEND_HEADS + hh) * MOBA_TOPK + t] for t in range(MOBA_TOPK)]
        lgs = [jnp.where(oks[r // PAGES_PER_BLOCK] > 0, raw[hh][r], NEG_INF) for r in range(N_SEL_PAGES)]
        lg_new = jnp.sum(qhs[hh] * knew_ref[:, chans[hh]], axis=-1, keepdims=True)
        m = lg_new
        for lg in lgs:
            m = jnp.maximum(m, jnp.max(lg, axis=-1, keepdims=True))
        p_new = jnp.exp(lg_new - m)
        pp = [jnp.exp(lg - m) for lg in lgs]
        l = p_new
        for p in pp:
            l = l + jnp.sum(p, axis=-1, keepdims=True)
        ps.append([jnp.broadcast_to(p, (SUBLANES, PAGE_SIZE)).astype(BF16) for p in pp])
        ls.append(l)
        p_news.append(p_new)
    pvs = [[_dot_nt(ps[hh][r], v_refs[hh * N_SEL_PAGES + r][...].astype(BF16))[0:1, :]
            for r in range(N_SEL_PAGES)] for hh in heads]
    outs = []
    for hh in heads:
        acc = p_news[hh] * vnew_ref[:, chans[hh]]
        for pv in pvs[hh]:
            acc = acc + pv
        outs.append(acc / ls[hh])
    o_ref[...] = jnp.concatenate(outs, axis=-1).astype(o_ref.dtype)


def _sample_attend(q, k_new, v_new, cache_k, cache_v, pages, ok):
    nb = q.shape[0]
    n_groups = N_ATTN_HEADS // ATTEND_HEADS
    row = pl.BlockSpec((None, 1, ATTEND_HEADS * ATTN_HEAD_DIM), lambda b, hg, pg, ok: (b, 0, hg))
    per_step = ATTEND_HEADS * N_SEL_PAGES

    def page_spec(i):
        return pl.BlockSpec((None, ATTN_HEAD_DIM, PAGE_SIZE),
                            lambda b, hg, pg, ok: (pg[(b * n_groups + hg) * per_step + i], 0, 0))

    page_specs = [page_spec(i) for i in range(per_step)]
    out = pl.pallas_call(
        _sample_attend_kernel,
        out_shape=jax.ShapeDtypeStruct((nb, 1, ATTN_DIM), BF16),
        grid_spec=pltpu.PrefetchScalarGridSpec(
            num_scalar_prefetch=2, grid=(nb, n_groups),
            in_specs=[row, row, row] + page_specs + page_specs, out_specs=row),
        compiler_params=_params(2),
        name="moba_sample_attend",
    )(pages, ok, q.reshape(nb, 1, ATTN_DIM), k_new.reshape(nb, 1, ATTN_DIM), v_new.reshape(nb, 1, ATTN_DIM),
      *([cache_k] * len(page_specs)), *([cache_v] * len(page_specs)))
    return out.reshape(nb, ATTN_DIM)


def _channel_major_pool(cache):
    return jnp.transpose(cache, (0, 1, 3, 4, 2)).reshape(-1, N_ATTN_HEADS, ATTN_HEAD_DIM, PAGE_SIZE)


def _moba_sample(q, k_new, v_new, ck, cv, page_table, bsum):
    nb, n_pages = page_table.shape
    n_past = n_pages * PAGE_SIZE // MOBA_BLOCK
    assert n_pages * PAGE_SIZE == n_past * MOBA_BLOCK
    sel = _sample_select(bsum, q, k_new, n_past)[:, :, :MOBA_TOPK]
    ck = ck.reshape(-1, ATTN_HEAD_DIM, PAGE_SIZE)
    cv = cv.reshape(-1, ATTN_HEAD_DIM, PAGE_SIZE)
    ok = (sel < n_past).astype(jnp.int32)
    blk = jnp.minimum(sel, n_past - 1)
    logical = blk[..., None] * PAGES_PER_BLOCK + jnp.arange(PAGES_PER_BLOCK)
    pages = jnp.take_along_axis(page_table, logical.reshape(nb, -1), axis=1)
    slabs = pages.reshape(nb, N_ATTN_HEADS, N_SEL_PAGES) * N_ATTN_HEADS + jnp.arange(N_ATTN_HEADS)[None, :, None]
    return _sample_attend(q, k_new, v_new, ck, cv, slabs.reshape(-1), ok.reshape(-1))


def _merge_ln_kernel(x_ref, g_ref, ys_ref, ya_ref, ga_ref, gb_ref, ws_ref, wa_ref, wo_ref, lng_ref, lnb_ref, o_ref):
    merged = (jax.nn.sigmoid(ga_ref[...]) * _dot(ys_ref[...], ws_ref[...])
              + jax.nn.sigmoid(gb_ref[...]) * _dot(ya_ref[...], wa_ref[...]))
    mix = _dot(merged.astype(BF16), wo_ref[...])
    y = DEEPNORM_ALPHA * x_ref[...] + g_ref[...] * mix
    o_ref[...] = _layer_norm(y, lng_ref[...], lnb_ref[...])


def _merge_ln(x, mods, k, y_ssm, y_attn, ga, gb, ws, wa, wo, lng, lnb, tm, rows_per_batch):
    n, d = x.shape
    rows = lambda w: pl.BlockSpec((tm, w), lambda i: (i, 0))
    return pl.pallas_call(
        _merge_ln_kernel,
        out_shape=jax.ShapeDtypeStruct((n, d), F32),
        grid=(n // tm,),
        in_specs=[rows(d)] + _mod_specs(mods, (k,), tm, rows_per_batch)
        + [rows(D_INNER), rows(ATTN_DIM), rows(d), rows(d),
           _resident(ws.shape), _resident(wa.shape), _resident(wo.shape), _resident((1, d)), _resident((1, d))],
        out_specs=rows(d),
        compiler_params=_params(1),
        name="merge_ln",
    )(x, mods, y_ssm, y_attn, ga, gb, ws, wa, wo, lng.reshape(1, d), lnb.reshape(1, d))


PROMPT_TM = 512
IN_PROJ_TM = 256


def _split_w_in(w_in):
    edges = (0, D_INNER, D_INNER + CONV_DIM, D_INNER + CONV_DIM + N_SSM_HEADS)
    edges = edges + tuple(edges[-1] + i * ATTN_DIM for i in range(1, 6))
    parts = [w_in[:, a:b] for a, b in zip(edges[:-1], edges[1:])]
    parts[2] = jnp.pad(parts[2], ((0, 0), (0, LANES - N_SSM_HEADS)))
    return [p.astype(BF16) for p in parts]


def kernel(x_prompt, x_sample, cache_k, cache_v, state_conv, state_ssm, page_table, c_prompt, c_sample, w_ada, b_ada, ln_g, ln_b, w_ffn1_gu, w_ffn1_down, w_ffn2_gu, w_ffn2_down, w_in, conv_w, conv_b, dt_bias, a_log, d_skip, ssm_norm_g, w_branch_ssm, w_branch_attn, w_out):
    assert w_in.shape[0] == DEPTH == 1
    nb, seq, d = x_prompt.shape
    nbs, dec_seq, _ = x_sample.shape
    assert dec_seq == 1
    l = 0
    w1gu, w1d = w_ffn1_gu[l].astype(BF16), w_ffn1_down[l].astype(BF16)
    w2gu, w2d = w_ffn2_gu[l].astype(BF16), w_ffn2_down[l].astype(BF16)
    w_parts = _split_w_in(w_in[l])
    ws, wa, wo = w_branch_ssm[l].astype(BF16), w_branch_attn[l].astype(BF16), w_out[l].astype(BF16)
    ssd_w = (conv_w[l], conv_b[l], dt_bias[l], a_log[l], d_skip[l], ssm_norm_g[l])

    mods = _ada_mods(jnp.concatenate([c_prompt, c_sample], axis=0), w_ada[l], b_ada[l])
    mods_p = mods[:nb].reshape(nb * 9, 1, d)
    mods_s = mods[nb:]

    xp = x_prompt.reshape(nb * seq, d)
    xp = _ffn_ln(xp, mods_p, (0, 1, 2), w1gu, w1d, ln_g[l, 0], ln_b[l, 0], PROMPT_TM, seq)
    kv_t = (False, False, False, False, True, True, False, False)
    w_parts_p = [w.T if t else w for w, t in zip(w_parts, kv_t)]
    z, xc, dt, q, k_t, v_t, ga, gb, conv_p = _in_proj(xp, mods_p, (3, 4), w_parts_p, kv_t, IN_PROJ_TM, seq,
                                                      conv=(1, conv_w[l], conv_b[l]))
    y_ssm, ssm_p = _ssd_prompt(xc, z, dt, nb, seq, *ssd_w[2:])
    ck, cv = _channel_major_pool(cache_k), _channel_major_pool(cache_v)
    ck_pages = ck.reshape(-1, ATTN_DIM, PAGE_SIZE)
    n_pages = page_table.shape[1]
    moba_steps = nb * (ATTN_DIM // MOBA_STEP_CHANNELS) * (seq // MOBA_BLOCK)
    if n_pages % PAGES_PER_STEP == 0 and nbs * (n_pages // PAGES_PER_STEP) == moba_steps:
        y_attn, bsum = _moba_prompt(q, k_t, v_t, nb, seq, paged=(ck_pages, page_table, PAGES_PER_STEP))
    else:
        y_attn = _moba_prompt(q, k_t, v_t, nb, seq)
        bsum = _page_sums(ck_pages, page_table)
    xp = _merge_ln(xp, mods_p, 5, y_ssm, y_attn, ga, gb, ws, wa, wo, ln_g[l, 1], ln_b[l, 1], PROMPT_TM, seq)
    xp = _ffn_ln(xp, mods_p, (6, 7, 8), w2gu, w2d, ln_g[l, 2], ln_b[l, 2], PROMPT_TM, seq)

    xs = x_sample.reshape(nbs, d)
    xs = _ffn_ln(xs, mods_s, (0, 1, 2), w1gu, w1d, ln_g[l, 0], ln_b[l, 0], nbs, 1)
    zs, xbcs, dts, qs, ks, vs, gas, gbs = _in_proj(xs, mods_s, (3, 4), w_parts, (False,) * len(w_parts), nbs, 1)
    y_ssm_s, conv_s, ssm_s = _ssd_sample(xbcs, zs, dts, state_conv[l], state_ssm[l], *ssd_w)
    y_attn_s = _moba_sample(qs, ks, vs, ck, cv, page_table, bsum)
    xs = _merge_ln(xs, mods_s, 5, y_ssm_s.reshape(nbs, D_INNER), y_attn_s, gas, gbs, ws, wa, wo,
                   ln_g[l, 1], ln_b[l, 1], nbs, 1)
    xs = _ffn_ln(xs, mods_s, (6, 7, 8), w2gu, w2d, ln_g[l, 2], ln_b[l, 2], nbs, 1)

    heads = (N_ATTN_HEADS, ATTN_HEAD_DIM)
    state = (N_SSM_HEADS, SSM_HEAD_DIM, SSM_STATE)
    to_rows = lambda a_t: jnp.transpose(a_t.reshape((1, nb) + heads + (seq,)), (0, 1, 4, 2, 3))
    return (xp.reshape(nb, seq, d), xs.reshape(nbs, 1, d), to_rows(k_t), to_rows(v_t),
            conv_p[None], ssm_p.reshape((1, nb) + state),
            ks.reshape((1, nbs, 1) + heads), vs.reshape((1, nbs, 1) + heads),
            conv_s[None], ssm_s.reshape((1, nbs) + state))
```

```python
import functools

import jax
import jax.numpy as jnp
from jax import lax
from jax.experimental import pallas as pl
from jax.experimental.pallas import tpu as pltpu

F32 = jnp.float32
BF16 = jnp.bfloat16

D_MODEL = 1024
D_INNER = 2048
SSM_HEAD_DIM = 64
N_SSM_HEADS = 32
N_SSM_GROUPS = 4
SSM_STATE = 128
CONV_WIDTH = 4
CONV_DIM = D_INNER + 2 * N_SSM_GROUPS * SSM_STATE
SSD_CHUNK = 128
ATTN_HEAD_DIM = 64
N_ATTN_HEADS = 16
ATTN_DIM = 1024
MOBA_BLOCK = 256
MOBA_TOPK = 3
PAGE_SIZE = 128
D_FF = 2816
DEPTH = 1
DEEPNORM_ALPHA = (2.0 * DEPTH) ** 0.25
LN_EPS = 1e-5
RMS_EPS = 1e-5
NEG_INF = -1e30
ATTN_SCALE = ATTN_HEAD_DIM ** -0.5

LANES = 128
SUBLANES = 8
VMEM_LIMIT = 56 * 1024 * 1024


def _dot(a, b):
    return jnp.dot(a, b, preferred_element_type=F32)


def _dot_nt(a, b):
    return lax.dot_general(a, b, (((1,), (1,)), ((), ())), preferred_element_type=F32)


def _split2(a):
    hi = a.astype(BF16)
    lo = (a - hi.astype(F32)).astype(BF16)
    return hi, lo


def _split3(a):
    hi = a.astype(BF16)
    r = a - hi.astype(F32)
    mid = r.astype(BF16)
    lo = (r - mid.astype(F32)).astype(BF16)
    return hi, mid, lo


def _dot_sel(a, sel_bf16):
    hi, mid, lo = _split3(a)
    return _dot(hi, sel_bf16) + _dot(mid, sel_bf16) + _dot(lo, sel_bf16)


def _dot_sel2(a, sel_bf16):
    hi, lo = _split2(a)
    return _dot(hi, sel_bf16) + _dot(lo, sel_bf16)


def _dot_hp(a, b):
    ah, al = _split2(a)
    bh, bl = _split2(b)
    return _dot(ah, bh) + _dot(al, bh) + _dot(ah, bl)


def _dot_nt_hp(a, b):
    ah, al = _split2(a)
    bh, bl = _split2(b)
    return _dot_nt(ah, bh) + _dot_nt(al, bh) + _dot_nt(ah, bl)


def _silu(x):
    half = 0.5 * x
    return half + half * jnp.tanh(half)


def _softplus(x):
    return jnp.maximum(x, 0.0) + jnp.log1p(jnp.exp(-jnp.abs(x)))


def _layer_norm(y, g, b):
    mu = jnp.mean(y, axis=-1, keepdims=True)
    yc = y - mu
    var = jnp.mean(yc * yc, axis=-1, keepdims=True)
    return yc * lax.rsqrt(var + LN_EPS) * g + b


def _resident(shape):
    nd = len(shape)
    return pl.BlockSpec(shape, lambda *_: (0,) * nd, pipeline_mode=pl.Buffered(1))


def _params(n_axes):
    return pltpu.CompilerParams(dimension_semantics=("arbitrary",) * n_axes, vmem_limit_bytes=VMEM_LIMIT)


def _ada_kernel(c_ref, w_ref, b_ref, o_ref):
    s = _silu(c_ref[...])
    o_ref[...] = _dot_hp(s, w_ref[...]) + b_ref[...]


def _ada_mods(c, w_ada, b_ada):
    m, d = c.shape
    n = w_ada.shape[1]
    tn = 1024
    return pl.pallas_call(
        _ada_kernel,
        out_shape=jax.ShapeDtypeStruct((m, n), F32),
        grid=(n // tn,),
        in_specs=[pl.BlockSpec((m, d), lambda j: (0, 0)),
                  pl.BlockSpec((d, tn), lambda j: (0, j)),
                  pl.BlockSpec((1, tn), lambda j: (0, j))],
        out_specs=pl.BlockSpec((m, tn), lambda j: (0, j)),
        compiler_params=_params(1),
        name="ada_mods",
    )(c, w_ada, b_ada.reshape(1, n))


def _mod_specs(mods, ks, tm, rows_per_batch):
    if mods.ndim == 3:
        tiles_per_batch = rows_per_batch // tm
        return [pl.BlockSpec((None, 1, D_MODEL), functools.partial(
            lambda i, k: ((i // tiles_per_batch) * 9 + k, 0, 0), k=k)) for k in ks]
    return [pl.BlockSpec((tm, D_MODEL), functools.partial(lambda i, k: (i, k), k=k)) for k in ks]


MXU_DIM = 256
FF_CHUNK = 6 * MXU_DIM
FF_EDGES = tuple(range(0, D_FF, FF_CHUNK)) + (D_FF,)


def _ffn_ln_kernel(x_ref, sh_ref, sc_ref, g_ref, wgu_ref, wd_ref, lng_ref, lnb_ref, o_ref):
    x = x_ref[...]
    h = (x * (1.0 + sc_ref[...]) + sh_ref[...]).astype(BF16)
    acc = jnp.zeros(x.shape, F32)
    for lo, hi in zip(FF_EDGES[:-1], FF_EDGES[1:]):
        gate = _dot(h, wgu_ref[:, lo:hi])
        up = _dot(h, wgu_ref[:, D_FF + lo:D_FF + hi])
        act = (_silu(gate) * up).astype(BF16)
        acc = acc + _dot(act, wd_ref[lo:hi, :])
    y = DEEPNORM_ALPHA * x + 0.5 * g_ref[...] * acc
    o_ref[...] = _layer_norm(y, lng_ref[...], lnb_ref[...])


def _ffn_ln(x, mods, ks, wgu, wd, lng, lnb, tm, rows_per_batch):
    n, d = x.shape
    row = pl.BlockSpec((tm, d), lambda i: (i, 0))
    return pl.pallas_call(
        _ffn_ln_kernel,
        out_shape=jax.ShapeDtypeStruct((n, d), F32),
        grid=(n // tm,),
        in_specs=[row] + _mod_specs(mods, ks, tm, rows_per_batch)
        + [_resident(wgu.shape), _resident(wd.shape), _resident((1, d)), _resident((1, d))],
        out_specs=row,
        compiler_params=_params(1),
        name="ffn_ln",
    )(x, mods, mods, mods, wgu, wd, lng.reshape(1, d), lnb.reshape(1, d))


CARRY = SUBLANES
PROJ_PIECE = 512
CONV_CHUNKS_PER_PIECE = 2


def _in_proj_kernel(x_ref, sh_ref, sc_ref, *refs, transposed, conv_index, tiles_per_batch):
    n_out = len(transposed)
    w_refs, o_refs = refs[:n_out], refs[n_out + 2 * (conv_index is not None):]
    h = (x_ref[...] * (1.0 + sc_ref[...]) + sh_ref[...]).astype(BF16)

    def project(idx, lo=None, hi=None):
        if transposed[idx]:
            return _dot_nt(w_refs[idx][lo:hi, :], h)
        return _dot(h, w_refs[idx][:, lo:hi])

    conv_chunks = []
    if conv_index is not None:
        cw_ref, cb_ref = refs[n_out:n_out + 2]
        state_ref, buf_s = refs[-2:]
        tm = x_ref.shape[0]
        tile = pl.program_id(0) % tiles_per_batch

        n_lane_tiles = CONV_DIM // LANES

        @pl.when(tile == 0)
        def _():
            buf_s[:, 0:CARRY, :] = jnp.zeros((n_lane_tiles, CARRY, LANES), F32)

        raw = project(conv_index)
        for c in range(n_lane_tiles):
            buf_s[c, CARRY:CARRY + tm, :] = raw[:, c * LANES:(c + 1) * LANES]

        def conv_chunk(c):
            cols = slice(c * LANES, (c + 1) * LANES)
            acc = cb_ref[:, cols]
            for j in range(CONV_WIDTH):
                off = CARRY - (CONV_WIDTH - 1) + j
                acc = acc + cw_ref[j:j + 1, cols] * buf_s[c, off:off + tm, :]
            o_refs[conv_index][:, cols] = _silu(acc)

        conv_chunks = list(range(n_lane_tiles))

    for idx in range(n_out):
        if idx == conv_index:
            continue
        width = w_refs[idx].shape[0] if transposed[idx] else w_refs[idx].shape[1]
        for lo in range(0, width, PROJ_PIECE):
            hi = min(lo + PROJ_PIECE, width)
            piece = project(idx, lo, hi).astype(o_refs[idx].dtype)
            if transposed[idx]:
                o_refs[idx][lo:hi, :] = piece
            else:
                o_refs[idx][:, lo:hi] = piece
            for chunk in conv_chunks[:CONV_CHUNKS_PER_PIECE]:
                conv_chunk(chunk)
            del conv_chunks[:CONV_CHUNKS_PER_PIECE]
    for chunk in conv_chunks:
        conv_chunk(chunk)

    if conv_index is not None:
        @pl.when(tile == tiles_per_batch - 1)
        def _():
            for c in range(n_lane_tiles):
                state_ref[:, c * LANES:(c + 1) * LANES] = buf_s[c, CARRY + tm - (CONV_WIDTH - 1):CARRY + tm, :]

        buf_s[:, 0:CARRY, :] = buf_s[:, tm:tm + CARRY, :]


def _in_proj(x, mods, ks, weights, transposed, tm, rows_per_batch, conv=None):
    n, d = x.shape
    nb = n // rows_per_batch
    tiles_per_batch = max(rows_per_batch // tm, 1)
    row = pl.BlockSpec((tm, d), lambda i: (i, 0))
    out_shape, out_specs = [], []
    for w, t in zip(weights, transposed):
        if t:
            out_shape.append(jax.ShapeDtypeStruct((nb, w.shape[0], rows_per_batch), F32))
            out_specs.append(pl.BlockSpec((None, w.shape[0], tm),
                                          lambda i: (i // tiles_per_batch, 0, i % tiles_per_batch)))
        else:
            out_shape.append(jax.ShapeDtypeStruct((n, w.shape[1]), F32))
            out_specs.append(pl.BlockSpec((tm, w.shape[1]), lambda i: (i, 0)))
    in_specs = [row] + _mod_specs(mods, ks, tm, rows_per_batch) + [_resident(w.shape) for w in weights]
    args = [x, mods, mods, *weights]
    scratch = []
    if conv is not None:
        conv_index, conv_w, conv_b = conv
        in_specs += [_resident((CONV_WIDTH, CONV_DIM)), _resident((1, CONV_DIM))]
        args += [conv_w, conv_b.reshape(1, CONV_DIM)]
        out_shape.append(jax.ShapeDtypeStruct((nb, CONV_WIDTH - 1, CONV_DIM), F32))
        out_specs.append(pl.BlockSpec((None, CONV_WIDTH - 1, CONV_DIM), lambda i: (i // tiles_per_batch, 0, 0)))
        scratch.append(pltpu.VMEM((CONV_DIM // LANES, CARRY + tm, LANES), F32))
    return pl.pallas_call(
        functools.partial(_in_proj_kernel, transposed=tuple(transposed),
                          conv_index=None if conv is None else conv[0], tiles_per_batch=tiles_per_batch),
        out_shape=out_shape,
        grid=(n // tm,),
        in_specs=in_specs,
        out_specs=out_specs,
        scratch_shapes=scratch,
        compiler_params=_params(1),
        name="in_proj",
    )(*args)


GROUP_W = D_INNER // N_SSM_GROUPS


def _gated_rmsnorm(y, z, norm_g):
    yz = y * _silu(z)
    outs = []
    for g in range(N_SSM_GROUPS):
        blk = yz[:, g * GROUP_W:(g + 1) * GROUP_W]
        ms = jnp.mean(blk * blk, axis=-1, keepdims=True)
        outs.append(blk * lax.rsqrt(ms + RMS_EPS))
    return jnp.concatenate(outs, axis=-1) * norm_g


def _ssd_prompt_kernel(xc_s, z_ref, dt_ref, dtb_ref, alog_ref, dskip_ref, ng_ref, exp_ref, y_ref, ssm_ref, st_s):
    c = pl.program_id(1)
    q = SSD_CHUNK

    @pl.when(c == 0)
    def _():
        st_s[...] = jnp.zeros(st_s.shape, F32)

    expand = exp_ref[...]
    dt = _softplus(dt_ref[...] + dtb_ref[...])
    da = dt * (-jnp.exp(alog_ref[...]))
    row_i = lax.broadcasted_iota(jnp.int32, (q, q), 0)
    col_i = lax.broadcasted_iota(jnp.int32, (q, q), 1)
    causal = row_i >= col_i
    tril = jnp.where(causal, 1.0, 0.0).astype(BF16)
    a_cum = _dot_tril(tril, da)
    a_src = a_cum - jnp.log(dt)
    a_src_t = a_src.T
    a_last = a_cum[q - 1:q, :]
    dec_end_x = _dot_sel2(jnp.exp(a_last - a_src), expand)
    exp_a_x = _dot_sel2(jnp.exp(a_cum), expand)
    chunk_decay_x = _dot_sel(jnp.broadcast_to(jnp.exp(a_last), (SUBLANES, LANES)), expand)[0:1, :]

    xs = xc_s[:, 0:D_INNER]
    xs_b = xs.astype(BF16)
    xdec_b = (xs * dec_end_x).astype(BF16)
    lane = lax.broadcasted_iota(jnp.int32, (1, LANES), 1)
    zero_b = jnp.zeros((), BF16)

    y_parts = []
    for g in range(N_SSM_GROUPS):
        b_g = xc_s[:, D_INNER + g * SSM_STATE:D_INNER + (g + 1) * SSM_STATE]
        c_g = xc_s[:, D_INNER + (N_SSM_GROUPS + g) * SSM_STATE:D_INNER + (N_SSM_GROUPS + g + 1) * SSM_STATE]
        b_gb = b_g.astype(BF16)
        c_gb = c_g.astype(BF16)
        cb = _dot_nt(c_gb, b_gb)
        gcols = slice(g * GROUP_W, (g + 1) * GROUP_W)
        st_old = st_s[:, gcols]
        y_inter = _dot(c_gb, st_old.astype(BF16)) * exp_a_x[:, gcols]
        st_s[:, gcols] = chunk_decay_x[:, gcols] * st_old + _dot(b_g.T.astype(BF16), xdec_b[:, gcols])
        pair_out = []
        heads_per_group = N_SSM_HEADS // N_SSM_GROUPS
        for pr in range(heads_per_group // 2):
            pcols = slice(g * GROUP_W + pr * LANES, g * GROUP_W + (pr + 1) * LANES)
            x_pair = xs_b[:, pcols]
            acc = None
            for hh in range(2):
                h = g * heads_per_group + pr * 2 + hh
                seg = a_cum[:, h:h + 1] - a_src_t[h:h + 1, :]
                w = (cb * jnp.exp(jnp.where(causal, seg, -jnp.inf))).astype(BF16)
                x_h = jnp.where((lane // SSM_HEAD_DIM) == hh, x_pair, zero_b)
                t = _dot(w, x_h)
                acc = t if acc is None else acc + t
            pair_out.append(acc)
        y_parts.append(jnp.concatenate(pair_out, axis=-1) + y_inter)
    y = jnp.concatenate(y_parts, axis=-1) + dskip_ref[...] * xs
    y_ref[...] = _gated_rmsnorm(y, z_ref[...], ng_ref[...]).astype(y_ref.dtype)

    @pl.when(c == pl.num_programs(1) - 1)
    def _():
        for r in range(D_INNER // LANES):
            ssm_ref[r * LANES:(r + 1) * LANES, :] = st_s[:, r * LANES:(r + 1) * LANES].T


def _dot_tril(tril_bf16, a):
    hi, mid, lo = _split3(a)
    return _dot(tril_bf16, hi) + _dot(tril_bf16, mid) + _dot(tril_bf16, lo)


def _pad_lanes(v):
    return jnp.zeros((1, LANES), F32).at[0, :v.shape[0]].set(v)


def _expand_matrix():
    h = jnp.arange(LANES)[:, None]
    ch = jnp.arange(D_INNER)[None, :] // SSM_HEAD_DIM
    return (h == ch).astype(BF16)


def _ssd_prompt(xc, z, dt, nb, seq, dt_bias, a_log, d_skip, norm_g):
    nc = seq // SSD_CHUNK
    rows = lambda w: pl.BlockSpec((SSD_CHUNK, w), lambda b, c: (b * nc + c, 0))
    return pl.pallas_call(
        _ssd_prompt_kernel,
        out_shape=[jax.ShapeDtypeStruct((nb * seq, D_INNER), BF16),
                   jax.ShapeDtypeStruct((nb, D_INNER, SSM_STATE), F32)],
        grid=(nb, nc),
        in_specs=[rows(CONV_DIM), rows(D_INNER), rows(LANES), _resident((1, LANES)),
                  _resident((1, LANES)), _resident((1, D_INNER)), _resident((1, D_INNER)),
                  _resident((LANES, D_INNER))],
        out_specs=[rows(D_INNER), pl.BlockSpec((None, D_INNER, SSM_STATE), lambda b, c: (b, 0, 0))],
        scratch_shapes=[pltpu.VMEM((SSM_STATE, D_INNER), F32)],
        compiler_params=_params(2),
        name="ssd_prompt",
    )(xc, z, dt, _pad_lanes(dt_bias), _pad_lanes(a_log),
      jnp.repeat(d_skip, SSM_HEAD_DIM).reshape(1, D_INNER), norm_g.reshape(1, D_INNER), _expand_matrix())


def _column_block(row_vec, r):
    return jnp.broadcast_to(row_vec[:, r * LANES:(r + 1) * LANES], (LANES, LANES)).T


def _ssd_sample_kernel(xbc_ref, z_ref, dt_ref, buf_ref, h0_ref, cw_ref, cb_ref, dtb_ref, alog_ref, dskip_ref,
                       ng_ref, exp_ref, y_ref, conv_ref, ssm_ref):
    xr = xbc_ref[...]
    acc = cb_ref[...] + cw_ref[CONV_WIDTH - 1:CONV_WIDTH, :] * xr
    for j in range(CONV_WIDTH - 1):
        acc = acc + cw_ref[j:j + 1, :] * buf_ref[j:j + 1, :]
    xc = _silu(acc)
    for j in range(CONV_WIDTH - 2):
        conv_ref[j:j + 1, :] = buf_ref[j + 1:j + 2, :]
    conv_ref[CONV_WIDTH - 2:CONV_WIDTH - 1, :] = xr

    expand = exp_ref[...]
    dt = _softplus(dt_ref[...] + dtb_ref[...])
    dec = jnp.exp(dt * (-jnp.exp(alog_ref[...])))
    both = jnp.concatenate([jnp.broadcast_to(dt, (SUBLANES // 2, LANES)),
                            jnp.broadcast_to(dec, (SUBLANES // 2, LANES))], axis=0)
    both_x = _dot_sel(both, expand)
    dt_x = both_x[0:1, :]
    dec_x = both_x[SUBLANES // 2:SUBLANES // 2 + 1, :]
    xs = xc[:, 0:D_INNER]
    xdt = xs * dt_x

    y_parts = []
    blocks_per_group = GROUP_W // LANES
    for g in range(N_SSM_GROUPS):
        b_g = xc[:, D_INNER + g * SSM_STATE:D_INNER + (g + 1) * SSM_STATE]
        c_g = xc[:, D_INNER + (N_SSM_GROUPS + g) * SSM_STATE:D_INNER + (N_SSM_GROUPS + g + 1) * SSM_STATE]
        c8 = jnp.broadcast_to(c_g, (SUBLANES, SSM_STATE)).astype(BF16)
        for rb in range(blocks_per_group):
            r = g * blocks_per_group + rb
            rows = slice(r * LANES, (r + 1) * LANES)
            h_new = _column_block(dec_x, r) * h0_ref[rows, :] + _column_block(xdt, r) * b_g
            ssm_ref[rows, :] = h_new
            y_parts.append(_dot_nt(c8, h_new.astype(BF16))[0:1, :])
    y = jnp.concatenate(y_parts, axis=-1) + dskip_ref[...] * xs
    y_ref[...] = _gated_rmsnorm(y, z_ref[...], ng_ref[...]).astype(y_ref.dtype)


def _ssd_sample(xbc, z, dt, state_conv, state_ssm, conv_w, conv_b, dt_bias, a_log, d_skip, norm_g):
    nb = xbc.shape[0]
    per_b = lambda *s: pl.BlockSpec((None,) + s, lambda b: (b,) + (0,) * len(s))
    return pl.pallas_call(
        _ssd_sample_kernel,
        out_shape=[jax.ShapeDtypeStruct((nb, 1, D_INNER), BF16),
                   jax.ShapeDtypeStruct((nb, CONV_WIDTH - 1, CONV_DIM), F32),
                   jax.ShapeDtypeStruct((nb, D_INNER, SSM_STATE), F32)],
        grid=(nb,),
        in_specs=[per_b(1, CONV_DIM), per_b(1, D_INNER), per_b(1, LANES),
                  per_b(CONV_WIDTH - 1, CONV_DIM), per_b(D_INNER, SSM_STATE),
                  _resident((CONV_WIDTH, CONV_DIM)), _resident((1, CONV_DIM)), _resident((1, LANES)),
                  _resident((1, LANES)), _resident((1, D_INNER)), _resident((1, D_INNER)),
                  _resident((LANES, D_INNER))],
        out_specs=[per_b(1, D_INNER), per_b(CONV_WIDTH - 1, CONV_DIM), per_b(D_INNER, SSM_STATE)],
        compiler_params=_params(1),
        name="ssd_sample",
    )(xbc.reshape(nb, 1, CONV_DIM), z.reshape(nb, 1, D_INNER), dt.reshape(nb, 1, LANES), state_conv,
      state_ssm.reshape(nb, D_INNER, SSM_STATE), conv_w, conv_b.reshape(1, CONV_DIM), _pad_lanes(dt_bias),
      _pad_lanes(a_log), jnp.repeat(d_skip, SSM_HEAD_DIM).reshape(1, D_INNER), norm_g.reshape(1, D_INNER),
      _expand_matrix())


HEADS_PER_VREG = LANES // ATTN_HEAD_DIM
MOBA_PAIRS_PER_STEP = 2
MOBA_HEADS_PER_STEP = MOBA_PAIRS_PER_STEP * HEADS_PER_VREG
MOBA_STEP_CHANNELS = MOBA_PAIRS_PER_STEP * LANES


def _moba_prompt_kernel(*refs, page_steps, step_pages):
    page_tasks = []
    if page_steps is not None:
        page_refs, bsum_ref = refs[4:4 + step_pages], refs[5 + step_pages]
        q_ref, k_ref, v_ref, o_ref = refs[1:4] + (refs[4 + step_pages],)
        kb_s, vt_s, means_s, sel_s, qb_s, m_s, l_s, acc_s = refs[6 + step_pages:]
        step = ((pl.program_id(0) * pl.num_programs(1) + pl.program_id(1)) * pl.num_programs(2)
                + pl.program_id(2))
        page_step = step % page_steps
        first_block = page_step * (step_pages // PAGES_PER_BLOCK)

        @pl.when(page_step == 0)
        def _():
            bsum_ref[...] = jnp.zeros(bsum_ref.shape, F32)

        page_tasks = list(range(0, ATTN_DIM, LANES))
    else:
        q_ref, k_ref, v_ref, o_ref, kb_s, vt_s, means_s, sel_s, qb_s, m_s, l_s, acc_s = refs

    def run_page_tasks(n):
        for lo in page_tasks[:n]:
            _page_sum_rows(page_refs, bsum_ref, first_block, lo)
        del page_tasks[:n]

    tq = MOBA_BLOCK
    nblk = kb_s.shape[0]
    own = pl.program_id(2)
    pair_cols = [slice(pr * LANES, (pr + 1) * LANES) for pr in range(MOBA_PAIRS_PER_STEP)]
    head_rows = [slice(h * ATTN_HEAD_DIM, (h + 1) * ATTN_HEAD_DIM) for h in range(MOBA_HEADS_PER_STEP)]

    @pl.when(own == 0)
    def _():
        for j in range(nblk):
            for pr in range(MOBA_PAIRS_PER_STEP):
                ksum = jnp.zeros((1, LANES), F32)
                for t in range(MOBA_BLOCK // LANES):
                    cols = slice(j * MOBA_BLOCK + t * LANES, j * MOBA_BLOCK + (t + 1) * LANES)
                    kblk = k_ref[pair_cols[pr], cols].T
                    kb_s[j, t * LANES:(t + 1) * LANES, pair_cols[pr]] = kblk.astype(BF16)
                    ksum = ksum + jnp.sum(kblk, axis=0, keepdims=True)
                means_s[j:j + 1, pair_cols[pr]] = ksum * (1.0 / MOBA_BLOCK)
            vt_s[j] = v_ref[:, j * MOBA_BLOCK:(j + 1) * MOBA_BLOCK].astype(BF16)

    lane = lax.broadcasted_iota(jnp.int32, (1, LANES), 1)
    blk_i = lax.broadcasted_iota(jnp.int32, (nblk, tq), 0)
    causal = (lax.broadcasted_iota(jnp.int32, (MOBA_BLOCK, tq), 0)
              <= lax.broadcasted_iota(jnp.int32, (MOBA_BLOCK, tq), 1))
    heads = range(MOBA_HEADS_PER_STEP)
    cols_of = [pair_cols[h // HEADS_PER_VREG] for h in heads]
    qms = [jnp.where((lane // ATTN_HEAD_DIM) == h % HEADS_PER_VREG, q_ref[:, cols_of[h]], 0.0) for h in heads]
    qbs = [(qm * ATTN_SCALE).astype(BF16) for qm in qms]
    for h in heads:
        qb_s[h] = qbs[h]
    raw_own = [_dot_nt(kb_s[own, :, cols_of[h]], qbs[h]) for h in heads]
    run_page_tasks(2)
    scores = [_dot_nt_hp(means_s[:, cols_of[h]], qms[h]) for h in heads]
    run_page_tasks(2)
    for h in heads:
        s_t = jnp.where(blk_i < own, scores[h], NEG_INF)
        for j in range(nblk):
            sj = s_t[j:j + 1, :]
            beats = jnp.where(s_t > sj, 1.0, 0.0) + jnp.where((s_t == sj) & (blk_i < j), 1.0, 0.0)
            rank = jnp.sum(beats, axis=0, keepdims=True)
            sel_s[h * nblk + j] = jnp.where(rank < float(MOBA_TOPK), 1.0, 0.0) * jnp.where(j < own, 1.0, 0.0)
    ps = []
    for h in heads:
        lg = jnp.where(causal, raw_own[h], NEG_INF)
        m = jnp.max(lg, axis=0, keepdims=True)
        p = jnp.exp(lg - m)
        m_s[h] = m
        l_s[h] = jnp.sum(p, axis=0, keepdims=True)
        ps.append(p.astype(BF16))
    for h in heads:
        acc_s[h] = _dot(vt_s[own, head_rows[h], :], ps[h])

    def past_blocks(js, between=lambda: None):
        raw = [[_dot_nt(kb_s[j, :, cols_of[h]], qb_s[h]) for h in heads] for j in js]
        between()
        ps, alphas = [], []
        for h in heads:
            lgs = [jnp.where(sel_s[h * nblk + j] > 0.5, raw[i][h], NEG_INF) for i, j in enumerate(js)]
            m_old = m_s[h]
            m_new = m_old
            for lg in lgs:
                m_new = jnp.maximum(m_new, jnp.max(lg, axis=0, keepdims=True))
            alpha = jnp.exp(m_old - m_new)
            pp = [jnp.exp(lg - m_new) for lg in lgs]
            l = alpha * l_s[h]
            for p in pp:
                l = l + jnp.sum(p, axis=0, keepdims=True)
            m_s[h] = m_new
            l_s[h] = l
            ps.append([p.astype(BF16) for p in pp])
            alphas.append(alpha)
        pvs = [[_dot(vt_s[j, head_rows[h], :], ps[h][i]) for i, j in enumerate(js)] for h in heads]
        between()
        for h in heads:
            acc = alphas[h] * acc_s[h]
            for pv in pvs[h]:
                acc = acc + pv
            acc_s[h] = acc

    def pair_body(jj, carry):
        past_blocks([2 * jj, 2 * jj + 1])
        return carry

    if page_tasks:
        half = len(page_tasks) // 2

        @pl.when(own >= 2)
        def _():
            past_blocks([0, 1], between=lambda: run_page_tasks(half))

        page_tasks.extend(range(ATTN_DIM - 2 * half * LANES, ATTN_DIM, LANES))

        @pl.when(own < 2)
        def _():
            run_page_tasks(len(page_tasks))

        lax.fori_loop(1, own // 2, pair_body, 0)
    else:
        lax.fori_loop(0, own // 2, pair_body, 0)

    @pl.when(own % 2 == 1)
    def _():
        past_blocks([own - 1])
    for pr in range(MOBA_PAIRS_PER_STEP):
        heads = range(pr * HEADS_PER_VREG, (pr + 1) * HEADS_PER_VREG)
        o_t = jnp.concatenate([acc_s[h] / l_s[h] for h in heads], axis=0)
        o_ref[:, pair_cols[pr]] = o_t.T.astype(o_ref.dtype)


def _moba_prompt(q, k_t, v_t, nb, seq, paged=None):
    assert seq % MOBA_BLOCK == 0
    nblk = seq // MOBA_BLOCK
    gw = MOBA_STEP_CHANNELS
    n_groups = ATTN_DIM // gw
    q3, k3, v3 = q.reshape(nb, seq, ATTN_DIM), k_t, v_t
    kv_spec = pl.BlockSpec((None, gw, seq), lambda b, g, t, *_: (b, g, 0))
    q_spec = pl.BlockSpec((None, MOBA_BLOCK, gw), lambda b, g, t, *_: (b, t, g))
    scratch = [pltpu.VMEM((nblk, MOBA_BLOCK, gw), BF16),
               pltpu.VMEM((nblk, gw, MOBA_BLOCK), BF16),
               pltpu.VMEM((nblk, gw), F32),
               pltpu.VMEM((MOBA_HEADS_PER_STEP * nblk, 1, MOBA_BLOCK), F32),
               pltpu.VMEM((MOBA_HEADS_PER_STEP, MOBA_BLOCK, LANES), BF16),
               pltpu.VMEM((MOBA_HEADS_PER_STEP, 1, MOBA_BLOCK), F32),
               pltpu.VMEM((MOBA_HEADS_PER_STEP, 1, MOBA_BLOCK), F32),
               pltpu.VMEM((MOBA_HEADS_PER_STEP, ATTN_HEAD_DIM, MOBA_BLOCK), F32)]
    grid = (nb, n_groups, nblk)
    out_attn = jax.ShapeDtypeStruct((nb, seq, ATTN_DIM), BF16)
    if paged is None:
        out = pl.pallas_call(
            functools.partial(_moba_prompt_kernel, page_steps=None, step_pages=0),
            out_shape=out_attn, grid=grid, in_specs=[q_spec, kv_spec, kv_spec], out_specs=q_spec,
            scratch_shapes=scratch, compiler_params=_params(3), name="moba_prompt",
        )(q3, k3, v3)
        return out.reshape(nb * seq, ATTN_DIM)
    cache_t, page_table, step_pages = paged
    nbs, n_pages = page_table.shape
    page_steps = nb * n_groups * nblk // nbs
    assert nbs * page_steps == nb * n_groups * nblk and page_steps * step_pages == n_pages
    assert step_pages % PAGES_PER_BLOCK == 0 and n_pages // PAGES_PER_BLOCK <= LANES

    def step_of(b, g, t):
        return (b * n_groups + g) * nblk + t

    page_specs = _page_specs(lambda b, g, t, pt, r: pt[step_of(b, g, t) * step_pages + r], step_pages)
    out, bsum = pl.pallas_call(
        functools.partial(_moba_prompt_kernel, page_steps=page_steps, step_pages=step_pages),
        out_shape=[out_attn, jax.ShapeDtypeStruct((nbs, ATTN_DIM, LANES), F32)],
        grid_spec=pltpu.PrefetchScalarGridSpec(
            num_scalar_prefetch=1, grid=grid, in_specs=[q_spec, kv_spec, kv_spec] + page_specs,
            out_specs=[q_spec, pl.BlockSpec((None, ATTN_DIM, LANES),
                                            lambda b, g, t, pt: (step_of(b, g, t) // page_steps, 0, 0))],
            scratch_shapes=scratch),
        compiler_params=_params(3),
        name="moba_prompt_pages",
    )(page_table.reshape(-1), q3, k3, v3, *([cache_t] * step_pages))
    return out.reshape(nb * seq, ATTN_DIM), bsum


PAGES_PER_STEP = 16
PAGES_PER_BLOCK = MOBA_BLOCK // PAGE_SIZE


def _page_sum_rows(page_refs, o_ref, first_block, lo):
    lane = lax.broadcasted_iota(jnp.int32, (1, LANES), 1)
    rows = slice(lo, lo + LANES)
    acc = o_ref[rows, :]
    for t in range(len(page_refs) // PAGES_PER_BLOCK):
        pages = page_refs[t * PAGES_PER_BLOCK][rows, :]
        for u in range(1, PAGES_PER_BLOCK):
            pages = pages + page_refs[t * PAGES_PER_BLOCK + u][rows, :]
        acc = jnp.where(lane == first_block + t, jnp.sum(pages, axis=-1, keepdims=True), acc)
    o_ref[rows, :] = acc


def _page_sums_kernel(pt_ref, *refs):
    o_ref = refs[-1]
    i = pl.program_id(1)

    @pl.when(i == 0)
    def _():
        o_ref[...] = jnp.zeros(o_ref.shape, F32)

    for lo in range(0, ATTN_DIM, LANES):
        _page_sum_rows(refs[:-1], o_ref, i * (PAGES_PER_STEP // PAGES_PER_BLOCK), lo)


def _page_specs(index_of, n=PAGES_PER_STEP):
    return [pl.BlockSpec((None, ATTN_DIM, PAGE_SIZE), functools.partial(
        lambda *a, r: (index_of(*a, r), 0, 0), r=r)) for r in range(n)]


def _page_sums(cache_t, page_table):
    nb, n_pages = page_table.shape
    assert n_pages % PAGES_PER_STEP == 0 and n_pages // PAGES_PER_BLOCK <= LANES
    specs = _page_specs(lambda b, i, pt, r: pt[b, i * PAGES_PER_STEP + r])
    return pl.pallas_call(
        _page_sums_kernel,
        out_shape=jax.ShapeDtypeStruct((nb, ATTN_DIM, LANES), F32),
        grid_spec=pltpu.PrefetchScalarGridSpec(
            num_scalar_prefetch=1, grid=(nb, n_pages // PAGES_PER_STEP), in_specs=specs,
            out_specs=pl.BlockSpec((None, ATTN_DIM, LANES), lambda b, i, pt: (b, 0, 0))),
        compiler_params=_params(2),
        name="page_sums",
    )(page_table, *([cache_t] * PAGES_PER_STEP))


def _sample_select_kernel(bsum_ref, q_ref, knew_ref, sel_ref, *, n_past):
    seqs = range(bsum_ref.shape[0])
    head = lax.broadcasted_iota(jnp.int32, (N_ATTN_HEADS, ATTN_DIM), 0)
    chan_head = lax.broadcasted_iota(jnp.int32, (N_ATTN_HEADS, ATTN_DIM), 1) // ATTN_HEAD_DIM
    q_bds = [jnp.where(head == chan_head, q_ref[i], 0.0) for i in seqs]
    scores = [_dot_hp(q_bds[i], bsum_ref[i] * (1.0 / MOBA_BLOCK)) for i in seqs]
    lane = lax.broadcasted_iota(jnp.int32, (N_ATTN_HEADS, LANES), 1)
    lane_f = lane.astype(F32)
    own = n_past
    removed = -jnp.inf
    for i in seqs:
        s_new = jnp.sum(q_bds[i] * (knew_ref[i] * (1.0 / MOBA_BLOCK)), axis=-1, keepdims=True)
        s = jnp.where(lane == n_past, s_new, scores[i])
        s = jnp.where(lane < own, s, NEG_INF)
        s = jnp.where(lane <= n_past, s, removed)
        out = jnp.zeros(s.shape, jnp.int32)
        for r in range(MOBA_TOPK):
            mx = jnp.max(s, axis=-1, keepdims=True)
            idx = jnp.min(jnp.where(s == mx, lane_f, float(LANES)), axis=-1, keepdims=True)
            out = jnp.where(lane == r, idx.astype(jnp.int32), out)
            s = jnp.where(lane_f == idx, removed, s)
        sel_ref[i] = out


SELECT_SEQS_PER_STEP = 4


def _sample_select(bsum, q, k_new, n_past):
    nb = bsum.shape[0]
    assert n_past < LANES
    ns = SELECT_SEQS_PER_STEP if nb % SELECT_SEQS_PER_STEP == 0 else 1
    per_b = lambda *s: pl.BlockSpec((ns,) + s, lambda b: (b,) + (0,) * len(s))
    return pl.pallas_call(
        functools.partial(_sample_select_kernel, n_past=n_past),
        out_shape=jax.ShapeDtypeStruct((nb, N_ATTN_HEADS, LANES), jnp.int32),
        grid=(nb // ns,),
        in_specs=[per_b(ATTN_DIM, LANES), per_b(1, ATTN_DIM), per_b(1, ATTN_DIM)],
        out_specs=per_b(N_ATTN_HEADS, LANES),
        compiler_params=_params(1),
        name="moba_sample_select",
    )(bsum, q.reshape(nb, 1, ATTN_DIM), k_new.reshape(nb, 1, ATTN_DIM))


N_SEL_PAGES = MOBA_TOPK * PAGES_PER_BLOCK
ATTEND_HEADS = 8


def _sample_attend_kernel(pg_ref, ok_ref, q_ref, knew_ref, vnew_ref, *refs):
    o_ref = refs[-1]
    k_refs = refs[:ATTEND_HEADS * N_SEL_PAGES]
    v_refs = refs[ATTEND_HEADS * N_SEL_PAGES:2 * ATTEND_HEADS * N_SEL_PAGES]
    step = pl.program_id(0) * pl.num_programs(1) + pl.program_id(1)
    heads = range(ATTEND_HEADS)
    chans = [slice(hh * ATTN_HEAD_DIM, (hh + 1) * ATTN_HEAD_DIM) for hh in heads]
    qhs = [q_ref[:, chans[hh]] * ATTN_SCALE for hh in heads]
    q8s = [jnp.broadcast_to(qh, (SUBLANES, ATTN_HEAD_DIM)).astype(BF16) for qh in qhs]
    raw = [[_dot(q8s[hh], k_refs[hh * N_SEL_PAGES + r][...].astype(BF16))[0:1, :]
            for r in range(N_SEL_PAGES)] for hh in heads]
    ps, ls, p_news = [], [], []
    for hh in heads:
        oks = [ok_ref[(step * ATTEND_HEADS + hh) * MOBA_TOPK + t] for t in range(MOBA_TOPK)]
        lgs = [jnp.where(oks[r // PAGES_PER_BLOCK] > 0, raw[hh][r], NEG_INF) for r in range(N_SEL_PAGES)]
        lg_new = jnp.sum(qhs[hh] * knew_ref[:, chans[hh]], axis=-1, keepdims=True)
        m = lg_new
        for lg in lgs:
            m = jnp.maximum(m, jnp.max(lg, axis=-1, keepdims=True))
        p_new = jnp.exp(lg_new - m)
        pp = [jnp.exp(lg - m) for lg in lgs]
        l = p_new
        for p in pp:
            l = l + jnp.sum(p, axis=-1, keepdims=True)
        ps.append([jnp.broadcast_to(p, (SUBLANES, PAGE_SIZE)).astype(BF16) for p in pp])
        ls.append(l)
        p_news.append(p_new)
    pvs = [[_dot_nt(ps[hh][r], v_refs[hh * N_SEL_PAGES + r][...].astype(BF16))[0:1, :]
            for r in range(N_SEL_PAGES)] for hh in heads]
    outs = []
    for hh in heads:
        acc = p_news[hh] * vnew_ref[:, chans[hh]]
        for pv in pvs[hh]:
            acc = acc + pv
        outs.append(acc / ls[hh])
    o_ref[...] = jnp.concatenate(outs, axis=-1).astype(o_ref.dtype)


def _sample_attend(q, k_new, v_new, cache_k, cache_v, pages, ok):
    nb = q.shape[0]
    n_groups = N_ATTN_HEADS // ATTEND_HEADS
    row = pl.BlockSpec((None, 1, ATTEND_HEADS * ATTN_HEAD_DIM), lambda b, hg, pg, ok: (b, 0, hg))
    per_step = ATTEND_HEADS * N_SEL_PAGES

    def page_spec(i):
        return pl.BlockSpec((None, ATTN_HEAD_DIM, PAGE_SIZE),
                            lambda b, hg, pg, ok: (pg[(b * n_groups + hg) * per_step + i], 0, 0))

    page_specs = [page_spec(i) for i in range(per_step)]
    out = pl.pallas_call(
        _sample_attend_kernel,
        out_shape=jax.ShapeDtypeStruct((nb, 1, ATTN_DIM), BF16),
        grid_spec=pltpu.PrefetchScalarGridSpec(
            num_scalar_prefetch=2, grid=(nb, n_groups),
            in_specs=[row, row, row] + page_specs + page_specs, out_specs=row),
        compiler_params=_params(2),
        name="moba_sample_attend",
    )(pages, ok, q.reshape(nb, 1, ATTN_DIM), k_new.reshape(nb, 1, ATTN_DIM), v_new.reshape(nb, 1, ATTN_DIM),
      *([cache_k] * len(page_specs)), *([cache_v] * len(page_specs)))
    return out.reshape(nb, ATTN_DIM)


def _channel_major_pool(cache):
    return jnp.transpose(cache, (0, 1, 3, 4, 2)).reshape(-1, N_ATTN_HEADS, ATTN_HEAD_DIM, PAGE_SIZE)


def _moba_sample(q, k_new, v_new, ck, cv, page_table, bsum):
    nb, n_pages = page_table.shape
    n_past = n_pages * PAGE_SIZE // MOBA_BLOCK
    assert n_pages * PAGE_SIZE == n_past * MOBA_BLOCK
    sel = _sample_select(bsum, q, k_new, n_past)[:, :, :MOBA_TOPK]
    ck = ck.reshape(-1, ATTN_HEAD_DIM, PAGE_SIZE)
    cv = cv.reshape(-1, ATTN_HEAD_DIM, PAGE_SIZE)
    ok = (sel < n_past).astype(jnp.int32)
    blk = jnp.minimum(sel, n_past - 1)
    logical = blk[..., None] * PAGES_PER_BLOCK + jnp.arange(PAGES_PER_BLOCK)
    pages = jnp.take_along_axis(page_table, logical.reshape(nb, -1), axis=1)
    slabs = pages.reshape(nb, N_ATTN_HEADS, N_SEL_PAGES) * N_ATTN_HEADS + jnp.arange(N_ATTN_HEADS)[None, :, None]
    return _sample_attend(q, k_new, v_new, ck, cv, slabs.reshape(-1), ok.reshape(-1))


def _merge_ln_kernel(x_ref, g_ref, ys_ref, ya_ref, ga_ref, gb_ref, ws_ref, wa_ref, wo_ref, lng_ref, lnb_ref, o_ref):
    merged = (jax.nn.sigmoid(ga_ref[...]) * _dot(ys_ref[...], ws_ref[...])
              + jax.nn.sigmoid(gb_ref[...]) * _dot(ya_ref[...], wa_ref[...]))
    mix = _dot(merged.astype(BF16), wo_ref[...])
    y = DEEPNORM_ALPHA * x_ref[...] + g_ref[...] * mix
    o_ref[...] = _layer_norm(y, lng_ref[...], lnb_ref[...])


def _merge_ln(x, mods, k, y_ssm, y_attn, ga, gb, ws, wa, wo, lng, lnb, tm, rows_per_batch):
    n, d = x.shape
    rows = lambda w: pl.BlockSpec((tm, w), lambda i: (i, 0))
    return pl.pallas_call(
        _merge_ln_kernel,
        out_shape=jax.ShapeDtypeStruct((n, d), F32),
        grid=(n // tm,),
        in_specs=[rows(d)] + _mod_specs(mods, (k,), tm, rows_per_batch)
        + [rows(D_INNER), rows(ATTN_DIM), rows(d), rows(d),
           _resident(ws.shape), _resident(wa.shape), _resident(wo.shape), _resident((1, d)), _resident((1, d))],
        out_specs=rows(d),
        compiler_params=_params(1),
        name="merge_ln",
    )(x, mods, y_ssm, y_attn, ga, gb, ws, wa, wo, lng.reshape(1, d), lnb.reshape(1, d))


def _merge_ffn_ln_kernel(x_ref, g2_ref, ys_ref, ya_ref, ga_ref, gb_ref, ws_ref, wa_ref, wo_ref, lng2_ref, lnb2_ref,
                         sh_ref, sc_ref, g3_ref, wgu_ref, wd_ref, lng3_ref, lnb3_ref, o_ref):
    merged = (jax.nn.sigmoid(ga_ref[...]) * _dot(ys_ref[...], ws_ref[...])
              + jax.nn.sigmoid(gb_ref[...]) * _dot(ya_ref[...], wa_ref[...]))
    mix = _dot(merged.astype(BF16), wo_ref[...])
    x = _layer_norm(DEEPNORM_ALPHA * x_ref[...] + g2_ref[...] * mix, lng2_ref[...], lnb2_ref[...])
    h = (x * (1.0 + sc_ref[...]) + sh_ref[...]).astype(BF16)
    acc = jnp.zeros(x.shape, F32)
    for lo, hi in zip(FF_EDGES[:-1], FF_EDGES[1:]):
        gate = _dot(h, wgu_ref[:, lo:hi])
        up = _dot(h, wgu_ref[:, D_FF + lo:D_FF + hi])
        act = (_silu(gate) * up).astype(BF16)
        acc = acc + _dot(act, wd_ref[lo:hi, :])
    y = DEEPNORM_ALPHA * x + 0.5 * g3_ref[...] * acc
    o_ref[...] = _layer_norm(y, lng3_ref[...], lnb3_ref[...])


def _merge_ffn_ln(x, mods, y_ssm, y_attn, ga, gb, ws, wa, wo, lng2, lnb2, wgu, wd, lng3, lnb3, tm, rows_per_batch):
    n, d = x.shape
    rows = lambda w: pl.BlockSpec((tm, w), lambda i: (i, 0))
    vec = lambda v: v.reshape(1, d)
    return pl.pallas_call(
        _merge_ffn_ln_kernel,
        out_shape=jax.ShapeDtypeStruct((n, d), F32),
        grid=(n // tm,),
        in_specs=[rows(d)] + _mod_specs(mods, (5,), tm, rows_per_batch)
        + [rows(D_INNER), rows(ATTN_DIM), rows(d), rows(d),
           _resident(ws.shape), _resident(wa.shape), _resident(wo.shape), _resident((1, d)), _resident((1, d))]
        + _mod_specs(mods, (6, 7, 8), tm, rows_per_batch)
        + [_resident(wgu.shape), _resident(wd.shape), _resident((1, d)), _resident((1, d))],
        out_specs=rows(d),
        compiler_params=_params(1),
        name="merge_ffn_ln",
    )(x, mods, y_ssm, y_attn, ga, gb, ws, wa, wo, vec(lng2), vec(lnb2), mods, mods, mods, wgu, wd, vec(lng3), vec(lnb3))


MERGE_FFN_TM = 256


PROMPT_TM = 512
IN_PROJ_TM = 256


def _split_w_in(w_in):
    edges = (0, D_INNER, D_INNER + CONV_DIM, D_INNER + CONV_DIM + N_SSM_HEADS)
    edges = edges + tuple(edges[-1] + i * ATTN_DIM for i in range(1, 6))
    parts = [w_in[:, a:b] for a, b in zip(edges[:-1], edges[1:])]
    parts[2] = jnp.pad(parts[2], ((0, 0), (0, LANES - N_SSM_HEADS)))
    return [p.astype(BF16) for p in parts]


def kernel(x_prompt, x_sample, cache_k, cache_v, state_conv, state_ssm, page_table, c_prompt, c_sample, w_ada, b_ada, ln_g, ln_b, w_ffn1_gu, w_ffn1_down, w_ffn2_gu, w_ffn2_down, w_in, conv_w, conv_b, dt_bias, a_log, d_skip, ssm_norm_g, w_branch_ssm, w_branch_attn, w_out):
    assert w_in.shape[0] == DEPTH == 1
    nb, seq, d = x_prompt.shape
    nbs, dec_seq, _ = x_sample.shape
    assert dec_seq == 1
    l = 0
    w1gu, w1d = w_ffn1_gu[l].astype(BF16), w_ffn1_down[l].astype(BF16)
    w2gu, w2d = w_ffn2_gu[l].astype(BF16), w_ffn2_down[l].astype(BF16)
    w_parts = _split_w_in(w_in[l])
    ws, wa, wo = w_branch_ssm[l].astype(BF16), w_branch_attn[l].astype(BF16), w_out[l].astype(BF16)
    ssd_w = (conv_w[l], conv_b[l], dt_bias[l], a_log[l], d_skip[l], ssm_norm_g[l])

    mods = _ada_mods(jnp.concatenate([c_prompt, c_sample], axis=0), w_ada[l], b_ada[l])
    mods_p = mods[:nb].reshape(nb * 9, 1, d)
    mods_s = mods[nb:]

    xp = x_prompt.reshape(nb * seq, d)
    xp = _ffn_ln(xp, mods_p, (0, 1, 2), w1gu, w1d, ln_g[l, 0], ln_b[l, 0], PROMPT_TM, seq)
    kv_t = (False, False, False, False, True, True, False, False)
    w_parts_p = [w.T if t else w for w, t in zip(w_parts, kv_t)]
    z, xc, dt, q, k_t, v_t, ga, gb, conv_p = _in_proj(xp, mods_p, (3, 4), w_parts_p, kv_t, IN_PROJ_TM, seq,
                                                      conv=(1, conv_w[l], conv_b[l]))
    y_ssm, ssm_p = _ssd_prompt(xc, z, dt, nb, seq, *ssd_w[2:])
    ck, cv = _channel_major_pool(cache_k), _channel_major_pool(cache_v)
    ck_pages = ck.reshape(-1, ATTN_DIM, PAGE_SIZE)
    n_pages = page_table.shape[1]
    moba_steps = nb * (ATTN_DIM // MOBA_STEP_CHANNELS) * (seq // MOBA_BLOCK)
    if n_pages % PAGES_PER_STEP == 0 and nbs * (n_pages // PAGES_PER_STEP) == moba_steps:
        y_attn, bsum = _moba_prompt(q, k_t, v_t, nb, seq, paged=(ck_pages, page_table, PAGES_PER_STEP))
    else:
        y_attn = _moba_prompt(q, k_t, v_t, nb, seq)
        bsum = _page_sums(ck_pages, page_table)
    xp = _merge_ffn_ln(xp, mods_p, y_ssm, y_attn, ga, gb, ws, wa, wo, ln_g[l, 1], ln_b[l, 1],
                       w2gu, w2d, ln_g[l, 2], ln_b[l, 2], MERGE_FFN_TM, seq)

    xs = x_sample.reshape(nbs, d)
    xs = _ffn_ln(xs, mods_s, (0, 1, 2), w1gu, w1d, ln_g[l, 0], ln_b[l, 0], nbs, 1)
    zs, xbcs, dts, qs, ks, vs, gas, gbs = _in_proj(xs, mods_s, (3, 4), w_parts, (False,) * len(w_parts), nbs, 1)
    y_ssm_s, conv_s, ssm_s = _ssd_sample(xbcs, zs, dts, state_conv[l], state_ssm[l], *ssd_w)
    y_attn_s = _moba_sample(qs, ks, vs, ck, cv, page_table, bsum)
    xs = _merge_ln(xs, mods_s, 5, y_ssm_s.reshape(nbs, D_INNER), y_attn_s, gas, gbs, ws, wa, wo,
                   ln_g[l, 1], ln_b[l, 1], nbs, 1)
    xs = _ffn_ln(xs, mods_s, (6, 7, 8), w2gu, w2d, ln_g[l, 2], ln_b[l, 2], nbs, 1)

    heads = (N_ATTN_HEADS, ATTN_HEAD_DIM)
    state = (N_SSM_HEADS, SSM_HEAD_DIM, SSM_STATE)
    to_rows = lambda a_t: jnp.transpose(a_t.reshape((1, nb) + heads + (seq,)), (0, 1, 4, 2, 3))
    return (xp.reshape(nb, seq, d), xs.reshape(nbs, 1, d), to_rows(k_t), to_rows(v_t),
            conv_p[None], ssm_p.reshape((1, nb) + state),
            ks.reshape((1, nbs, 1) + heads), vs.reshape((1, nbs, 1) + heads),
            conv_s[None], ssm_s.reshape((1, nbs) + state))
```
